```python
import jax, jax.numpy as jnp
from jax import lax
import numpy as np

D_MODEL = 2048
BATCH = 4
SEQ = 4096
DEPTH = 2

GRID_W = 64
CTX_LEN = 256
BLOCK = 128
HEAD_DIM = 128
ROPE_BASE = 10000.0
EPS = 1e-6
A_HEADS = 8
A_KV_HEADS = 2
WINDOW = 128
B_HEADS = 8
B_KV_HEADS = 2
MLA_HEADS = 16
MLA_Q_RANK = 512
MLA_KV_RANK = 512
MLA_NOPE = 128
MLA_ROPE = 64
MLA_V = 128
N_EXPERTS = 16
EXPERT_FF = 2048
CAPACITY_FACTOR = 2
N_MOD = 6

A_Q = A_HEADS * HEAD_DIM
A_KV = A_KV_HEADS * HEAD_DIM
B_Q = B_HEADS * HEAD_DIM
B_KV = B_KV_HEADS * HEAD_DIM
EVEN_IN = A_Q + 2 * A_KV + B_Q + 2 * B_KV
EVEN_MIX = A_Q + B_Q
EVEN_SPLITS = (A_Q, A_Q + A_KV, A_Q + 2 * A_KV, A_Q + 2 * A_KV + B_Q, A_Q + 2 * A_KV + B_Q + B_KV)
ODD_IN = MLA_Q_RANK + MLA_KV_RANK + MLA_ROPE
ODD_MIX = MLA_HEADS * MLA_V

kernel_name = "hybrid_diffusion_prefix_trunk"


def rmsnorm(x, g):
    xf = x.astype(jnp.float32)
    y = xf * lax.rsqrt(jnp.mean(xf * xf, axis=-1, keepdims=True) + EPS)
    return (y * g.astype(jnp.float32)).astype(x.dtype)


def modulate(x, shift, scale):
    return x * (1 + scale) + shift


def axial_rope_tables(n_rows, rot_dim):
    row = jnp.repeat(jnp.arange(n_rows), GRID_W)
    col = jnp.tile(jnp.arange(GRID_W), n_rows)
    n_freq = rot_dim // 4
    inv = ROPE_BASE ** (-jnp.arange(n_freq, dtype=jnp.float32) / n_freq)
    ang = jnp.concatenate([row[:, None] * inv, col[:, None] * inv], axis=-1)
    return jnp.cos(ang), jnp.sin(ang)


def apply_rope(x, rope):
    cos, sin = rope
    half = x.shape[-1] // 2
    x1, x2 = x[..., :half], x[..., half:]
    c = cos[:, None, :].astype(x.dtype)
    s = sin[:, None, :].astype(x.dtype)
    return jnp.concatenate([x1 * c - x2 * s, x1 * s + x2 * c], axis=-1)


def block_swept_attention(q, k, v, scale):
    b, t = q.shape[:2]
    qb = q.reshape((b, t // BLOCK, BLOCK) + q.shape[2:]).swapaxes(0, 1)

    def one_block(q_blk):
        s = jnp.einsum('bqhgd,bkhd->bhgqk', q_blk, k).astype(jnp.float32) * scale
        p = jax.nn.softmax(s, axis=-1).astype(v.dtype)
        return jnp.einsum('bhgqk,bkhd->bqhgd', p, v)

    o = lax.map(one_block, qb)
    return o.swapaxes(0, 1).reshape(b, t, -1)


def sink_attention(q, k, v, sink, scale):
    b, t, hkv, g, _ = q.shape
    s = jnp.einsum('bqhgd,bkhd->bhgqk', q, k).astype(jnp.float32) * scale
    s_sink = jnp.broadcast_to(sink.astype(jnp.float32).reshape(1, hkv, g, 1, 1), s.shape[:-1] + (1,))
    p = jax.nn.softmax(jnp.concatenate([s, s_sink], axis=-1), axis=-1)[..., :-1].astype(v.dtype)
    return jnp.einsum('bhgqk,bkhd->bqhgd', p, v).reshape(b, t, -1)


def windowed_sink_attention(q, k, v, k_ctx, v_ctx, sink, scale):
    b, t, hkv, g, d = q.shape
    nb = t // BLOCK
    n_ctx = k_ctx.shape[1]
    qb = q.reshape(b, nb, BLOCK, hkv, g, d)

    def band(z):
        zp = jnp.pad(z, ((0, 0), (BLOCK, BLOCK), (0, 0), (0, 0))).reshape(b, nb + 2, BLOCK, hkv, z.shape[-1])
        return jnp.concatenate([zp[:, :-2], zp[:, 1:-1], zp[:, 2:]], axis=2)

    kw, vw = band(k), band(v)
    blk = jnp.arange(nb)[:, None, None]
    qpos = blk * BLOCK + jnp.arange(BLOCK)[None, :, None]
    kpos = (blk - 1) * BLOCK + jnp.arange(3 * BLOCK)[None, None, :]
    valid = (jnp.abs(kpos - qpos) <= WINDOW) & (kpos >= 0) & (kpos < t)
    s_win = jnp.einsum('bnqhgd,bnkhd->bnhgqk', qb, kw).astype(jnp.float32) * scale
    s_win = jnp.where(valid[None, :, None, None], s_win, -jnp.inf)
    s_ctx = jnp.einsum('bnqhgd,bkhd->bnhgqk', qb, k_ctx).astype(jnp.float32) * scale
    s_sink = jnp.broadcast_to(sink.astype(jnp.float32).reshape(1, 1, hkv, g, 1, 1), s_ctx.shape[:-1] + (1,))
    p = jax.nn.softmax(jnp.concatenate([s_ctx, s_win, s_sink], axis=-1), axis=-1).astype(v.dtype)
    p_ctx, p_win = p[..., :n_ctx], p[..., n_ctx:n_ctx + 3 * BLOCK]
    o = jnp.einsum('bnhgqk,bkhd->bnqhgd', p_ctx, v_ctx) + jnp.einsum('bnhgqk,bnkhd->bnqhgd', p_win, vw)
    return o.reshape(b, t, -1)


def even_mixer(a_lat, a_ctx, w_in, w_out, sink, qk_norm, rope, with_ctx):
    scale = HEAD_DIM ** -0.5
    ga, gb = A_HEADS // A_KV_HEADS, B_HEADS // B_KV_HEADS

    def project(a, pos):
        b, t, _ = a.shape
        aq, ak, av, bq, bk, bv = jnp.split(a @ w_in, EVEN_SPLITS, axis=-1)
        aq = aq.reshape(b, t, A_HEADS, HEAD_DIM)
        ak = ak.reshape(b, t, A_KV_HEADS, HEAD_DIM)
        av = av.reshape(b, t, A_KV_HEADS, HEAD_DIM)
        bq = rmsnorm(bq.reshape(b, t, B_HEADS, HEAD_DIM), qk_norm[0])
        bk = rmsnorm(bk.reshape(b, t, B_KV_HEADS, HEAD_DIM), qk_norm[1])
        bv = bv.reshape(b, t, B_KV_HEADS, HEAD_DIM)
        if pos is not None:
            aq, ak, bq, bk = apply_rope(aq, pos), apply_rope(ak, pos), apply_rope(bq, pos), apply_rope(bk, pos)
        aq = aq.reshape(b, t, A_KV_HEADS, ga, HEAD_DIM)
        bq = bq.reshape(b, t, B_KV_HEADS, gb, HEAD_DIM)
        return aq, ak, av, bq, bk, bv

    aq_l, ak_l, av_l, bq_l, bk_l, bv_l = project(a_lat, rope)
    aq_c, ak_c, av_c, bq_c, bk_c, bv_c = project(a_ctx, None)
    o_a = windowed_sink_attention(aq_l, ak_l, av_l, ak_c, av_c, sink, scale)
    o_b = block_swept_attention(bq_l, jnp.concatenate([bk_c, bk_l], axis=1),
                                jnp.concatenate([bv_c, bv_l], axis=1), scale)
    y_lat = jnp.concatenate([o_a, o_b], axis=-1) @ w_out
    y_ctx = None
    if with_ctx:
        o_a_c = sink_attention(aq_c, ak_c, av_c, sink, scale)
        o_b_c = block_swept_attention(bq_c, bk_c, bv_c, scale)
        y_ctx = jnp.concatenate([o_a_c, o_b_c], axis=-1) @ w_out
    return y_lat, y_ctx


def odd_mixer(a_lat, a_ctx, w_in, q_norm, kv_norm, w_uq, w_ukv, w_out, rope, with_ctx):
    scale = (MLA_NOPE + MLA_ROPE) ** -0.5

    def project(a, pos):
        b, t, _ = a.shape
        cq, ckv, kr = jnp.split(a @ w_in, (MLA_Q_RANK, MLA_Q_RANK + MLA_KV_RANK), axis=-1)
        q = (rmsnorm(cq, q_norm) @ w_uq).reshape(b, t, MLA_HEADS, MLA_NOPE + MLA_ROPE)
        kv = (rmsnorm(ckv, kv_norm) @ w_ukv).reshape(b, t, MLA_HEADS, MLA_NOPE + MLA_V)
        q_nope, q_rope = q[..., :MLA_NOPE], q[..., MLA_NOPE:]
        k_nope, v = kv[..., :MLA_NOPE], kv[..., MLA_NOPE:]
        kr = kr[:, :, None, :]
        if pos is not None:
            q_rope, kr = apply_rope(q_rope, pos), apply_rope(kr, pos)
        q = jnp.concatenate([q_nope, q_rope], axis=-1)[:, :, :, None, :]
        k = jnp.concatenate([k_nope, jnp.broadcast_to(kr, (b, t, MLA_HEADS, MLA_ROPE))], axis=-1)
        return q, k, v

    q_l, k_l, v_l = project(a_lat, rope)
    q_c, k_c, v_c = project(a_ctx, None)
    o_l = block_swept_attention(q_l, jnp.concatenate([k_c, k_l], axis=1),
                                jnp.concatenate([v_c, v_l], axis=1), scale)
    y_lat = o_l @ w_out
    y_ctx = block_swept_attention(q_c, k_c, v_c, scale) @ w_out if with_ctx else None
    return y_lat, y_ctx


def expert_choice_ffn(h, w_router, w_gate, w_up, w_down):
    b, n, d = h.shape
    cap = CAPACITY_FACTOR * n // N_EXPERTS
    logits = jnp.einsum('bnd,de->bne', h, w_router).astype(jnp.float32)
    aff = jax.nn.softmax(logits, axis=-1).swapaxes(1, 2)
    gate, idx = lax.top_k(aff, cap)
    xg = jax.vmap(lambda hb, ib: hb[ib])(h, idx)
    hid = jax.nn.silu(jnp.einsum('becd,edf->becf', xg, w_gate)) * jnp.einsum('becd,edf->becf', xg, w_up)
    y = jnp.einsum('becf,efd->becd', hid, w_down) * gate[..., None].astype(h.dtype)
    return jax.vmap(lambda ib, yb: jnp.zeros((n, d), yb.dtype).at[ib.reshape(-1)].add(yb.reshape(-1, d)))(idx, y)


def setup_inputs(seed: int = 0) -> dict:
    key = jax.random.key(seed)
    ks = jax.random.split(key, 22)
    n_even = (DEPTH + 1) // 2
    n_odd = DEPTH // 2
    f32 = jnp.float32

    def nrm(k, shape, fan_in):
        return jax.random.normal(k, shape, f32) * fan_in ** -0.5

    def gain(k, shape):
        return 1.0 + 0.1 * jax.random.normal(k, shape, f32)

    return {
        "x": jax.random.normal(ks[0], (BATCH, SEQ, D_MODEL), f32),
        "c": jax.random.normal(ks[1], (BATCH, D_MODEL), f32),
        "ctx": jax.random.normal(ks[2], (BATCH, CTX_LEN, D_MODEL), f32),
        "c_ctx": jax.random.normal(ks[3], (D_MODEL,), f32),
        "mod_w": 0.5 * nrm(ks[4], (DEPTH, D_MODEL, N_MOD * D_MODEL), D_MODEL),
        "mod_b": 0.02 * jax.random.normal(ks[5], (DEPTH, N_MOD * D_MODEL), f32),
        "norm_g": gain(ks[6], (DEPTH, 4, D_MODEL)),
        "ev_w_in": nrm(ks[7], (n_even, D_MODEL, EVEN_IN), D_MODEL),
        "ev_w_out": nrm(ks[8], (n_even, EVEN_MIX, D_MODEL), EVEN_MIX),
        "ev_sink": jax.random.normal(ks[9], (n_even, A_HEADS), f32),
        "ev_qk_norm": gain(ks[10], (n_even, 2, HEAD_DIM)),
        "od_w_in": nrm(ks[11], (n_odd, D_MODEL, ODD_IN), D_MODEL),
        "od_q_norm": gain(ks[12], (n_odd, MLA_Q_RANK)),
        "od_kv_norm": gain(ks[13], (n_odd, MLA_KV_RANK)),
        "od_w_uq": nrm(ks[14], (n_odd, MLA_Q_RANK, MLA_HEADS * (MLA_NOPE + MLA_ROPE)), MLA_Q_RANK),
        "od_w_ukv": nrm(ks[15], (n_odd, MLA_KV_RANK, MLA_HEADS * (MLA_NOPE + MLA_V)), MLA_KV_RANK),
        "od_w_out": nrm(ks[16], (n_odd, ODD_MIX, D_MODEL), ODD_MIX),
        "router_w": nrm(ks[17], (DEPTH, D_MODEL, N_EXPERTS), D_MODEL),
        "exp_w_gate": nrm(ks[18], (DEPTH, N_EXPERTS, D_MODEL, EXPERT_FF), D_MODEL),
        "exp_w_up": nrm(ks[19], (DEPTH, N_EXPERTS, D_MODEL, EXPERT_FF), D_MODEL),
        "exp_w_down": nrm(ks[20], (DEPTH, N_EXPERTS, EXPERT_FF, D_MODEL), EXPERT_FF),
    }


def reference(x, c, ctx, c_ctx, mod_w, mod_b, norm_g, ev_w_in, ev_w_out, ev_sink, ev_qk_norm,
              od_w_in, od_q_norm, od_kv_norm, od_w_uq, od_w_ukv, od_w_out,
              router_w, exp_w_gate, exp_w_up, exp_w_down):
    n_rows = x.shape[1] // GRID_W
    rope_head = axial_rope_tables(n_rows, HEAD_DIM)
    rope_mla = axial_rope_tables(n_rows, MLA_ROPE)
    h_lat, h_ctx = x, ctx
    for layer in range(DEPTH):
        with_ctx = layer < DEPTH - 1
        g = norm_g[layer]
        i = layer // 2
        m_lat = [m[:, None, :] for m in jnp.split(jax.nn.silu(c) @ mod_w[layer] + mod_b[layer], N_MOD, axis=-1)]
        m_ctx = jnp.split(jax.nn.silu(c_ctx) @ mod_w[layer] + mod_b[layer], N_MOD, axis=-1)
        a_lat = modulate(rmsnorm(h_lat, g[0]), m_lat[0], m_lat[1])
        a_ctx = modulate(rmsnorm(h_ctx, g[0]), m_ctx[0], m_ctx[1])
        if layer % 2 == 0:
            y_lat, y_ctx = even_mixer(a_lat, a_ctx, ev_w_in[i], ev_w_out[i], ev_sink[i], ev_qk_norm[i],
                                      rope_head, with_ctx)
        else:
            y_lat, y_ctx = odd_mixer(a_lat, a_ctx, od_w_in[i], od_q_norm[i], od_kv_norm[i], od_w_uq[i],
                                     od_w_ukv[i], od_w_out[i], rope_mla, with_ctx)
        h_lat = h_lat + m_lat[2] * rmsnorm(y_lat, g[1])
        f_lat = expert_choice_ffn(modulate(rmsnorm(h_lat, g[2]), m_lat[3], m_lat[4]),
                                  router_w[layer], exp_w_gate[layer], exp_w_up[layer], exp_w_down[layer])
        h_lat = h_lat + m_lat[5] * rmsnorm(f_lat, g[3])
        if with_ctx:
            h_ctx = h_ctx + m_ctx[2] * rmsnorm(y_ctx, g[1])
            f_ctx = expert_choice_ffn(modulate(rmsnorm(h_ctx, g[2]), m_ctx[3], m_ctx[4]),
                                      router_w[layer], exp_w_gate[layer], exp_w_up[layer], exp_w_down[layer])
            h_ctx = h_ctx + m_ctx[5] * rmsnorm(f_ctx, g[3])
    return h_lat
```

```python
import functools

import jax
import jax.numpy as jnp
from jax import lax
from jax.experimental import pallas as pl
from jax.experimental.pallas import tpu as pltpu

F32 = jnp.float32
BF16 = jnp.bfloat16
I32 = jnp.int32

EPS = 1e-6
GRID_W = 64
WINDOW = 128
ROPE_BASE = 10000.0
HEAD_DIM = 128
A_HEADS = 8
A_KV_HEADS = 2
B_HEADS = 8
B_KV_HEADS = 2
MLA_HEADS = 16
MLA_Q_RANK = 512
MLA_KV_RANK = 512
MLA_NOPE = 128
MLA_ROPE = 64
MLA_V = 128
MLA_QK_PAD = 256
N_EXPERTS = 16
CAPACITY_FACTOR = 2
N_MOD = 6
MOD_ROWS = 8

LANES = 128
BF16_SUBLANES = 16
VMEM_LIMIT_BYTES = 56 * 1024 * 1024

EVEN_COLS = (A_HEADS + 2 * A_KV_HEADS + B_HEADS + 2 * B_KV_HEADS) * HEAD_DIM
AQ0, AK0, AV0 = 0, A_HEADS, A_HEADS + A_KV_HEADS
BQ0 = A_HEADS + 2 * A_KV_HEADS
BK0 = BQ0 + B_HEADS
BV0 = BK0 + B_KV_HEADS


def _params(*sem):
    return pltpu.CompilerParams(dimension_semantics=sem, vmem_limit_bytes=VMEM_LIMIT_BYTES)


def _tile(n, cap):
    t = min(n, cap)
    while n % t:
        t -= 1
    return t


def _const_spec(shape):
    nd = len(shape)
    return pl.BlockSpec(shape, lambda *_: (0,) * nd)


def _rms(x, g):
    return x * lax.rsqrt(jnp.mean(x * x, axis=-1, keepdims=True) + EPS) * g


def _silu(x):
    return x / (1.0 + jnp.exp(-x))


def _dot(a, b):
    return jnp.dot(a, b, preferred_element_type=F32)


def _dot_nt(a, b):
    return lax.dot_general(a, b, (((1,), (1,)), ((), ())), preferred_element_type=F32)


def _mod_kernel(cs_ref, w_ref, b_ref, o_ref):
    s = _silu(cs_ref[...])
    o_ref[0] = jnp.dot(s, w_ref[0], precision=lax.Precision.HIGHEST, preferred_element_type=F32) + b_ref[0]


def _modulation(cs, mod_w, mod_b):
    depth, d, n6 = mod_w.shape
    tn = _tile(n6, 1024)
    return pl.pallas_call(
        _mod_kernel,
        grid=(depth, n6 // tn),
        in_specs=[
            _const_spec((MOD_ROWS, d)),
            pl.BlockSpec((1, d, tn), lambda l, j: (l, 0, j)),
            pl.BlockSpec((1, 1, tn), lambda l, j: (l, 0, j)),
        ],
        out_specs=pl.BlockSpec((1, MOD_ROWS, tn), lambda l, j: (l, 0, j)),
        out_shape=jax.ShapeDtypeStruct((depth, MOD_ROWS, n6), F32),
        compiler_params=_params("arbitrary", "arbitrary"),
        name="modulation",
    )(cs, mod_w, mod_b.reshape(depth, 1, n6))


def _mod_spec(d, row_fn):
    return pl.BlockSpec((None, 1, d), lambda i: (row_fn(i), 0, 0))


def _even_inproj_kernel(*refs, rope, scale):
    if rope:
        x_ref, g_ref, sh_ref, sc_ref, w_ref, qkg_ref, cos_ref, sin_ref, o_ref = refs
        cos, sin = cos_ref[...], sin_ref[...]
    else:
        x_ref, g_ref, sh_ref, sc_ref, w_ref, qkg_ref, o_ref = refs
    a = _rms(x_ref[...], g_ref[...]) * (1.0 + sc_ref[...]) + sh_ref[...]
    ab = a.astype(BF16)
    n_heads = EVEN_COLS // HEAD_DIM
    for j in range(n_heads // 2):
        acc = _dot(ab, w_ref[:, j * 2 * HEAD_DIM:(j + 1) * 2 * HEAD_DIM])
        for hh in range(2):
            head = 2 * j + hh
            v = acc[:, hh * HEAD_DIM:(hh + 1) * HEAD_DIM]
            is_q = head < AK0 or BQ0 <= head < BK0
            is_v = AV0 <= head < BQ0 or head >= BV0
            if BQ0 <= head < BK0:
                v = _rms(v, qkg_ref[0:1, :])
            elif BK0 <= head < BV0:
                v = _rms(v, qkg_ref[1:2, :])
            if rope and not is_v:
                v = v * cos + pltpu.roll(v, HEAD_DIM // 2, 1) * sin
            if is_q:
                v = v * scale
            o_ref[:, head * HEAD_DIM:(head + 1) * HEAD_DIM] = v.astype(BF16)


def _even_inproj(h, g, shift, scale_m, w_bf, qk_gain, rope_tabs, row_fn, seq):
    r, d = h.shape
    tm = _tile(seq, 512)
    rope = rope_tabs is not None
    in_specs = [
        pl.BlockSpec((tm, d), lambda i: (i, 0)),
        _const_spec((1, d)),
        _mod_spec(d, lambda i: row_fn(i * tm)),
        _mod_spec(d, lambda i: row_fn(i * tm)),
        _const_spec((d, EVEN_COLS)),
        _const_spec((2, HEAD_DIM)),
    ]
    args = [h, g, shift, scale_m, w_bf, qk_gain]
    if rope:
        nt = seq // tm
        in_specs += [pl.BlockSpec((tm, HEAD_DIM), lambda i: (i % nt, 0))] * 2
        args += list(rope_tabs)
    return pl.pallas_call(
        functools.partial(_even_inproj_kernel, rope=rope, scale=HEAD_DIM ** -0.5),
        grid=(r // tm,),
        in_specs=in_specs,
        out_specs=pl.BlockSpec((tm, EVEN_COLS), lambda i: (i, 0)),
        out_shape=jax.ShapeDtypeStruct((r, EVEN_COLS), BF16),
        compiler_params=_params("arbitrary"),
        name="even_inproj",
    )(*args)


def _rope_pad(v, c, s1, s2):
    return v * c + pltpu.roll(v, LANES - MLA_ROPE // 2, 1) * s1 + pltpu.roll(v, MLA_ROPE // 2, 1) * s2


def _mla_proj_kernel(*refs, rope, want_q, scale):
    refs = list(refs)
    x_ref, g_ref, sh_ref, sc_ref, win_ref, wkr_ref, qn_ref, kvn_ref = refs[:8]
    refs = refs[8:]
    if want_q:
        wuq_ref = refs.pop(0)
    wukv_ref = refs.pop(0)
    if rope:
        c, s1, s2 = refs[0][...], refs[1][...], refs[2][...]
        refs = refs[3:]
    if want_q:
        q_ref = refs.pop(0)
    k_ref, v_ref = refs
    a = _rms(x_ref[...], g_ref[...]) * (1.0 + sc_ref[...]) + sh_ref[...]
    ab = a.astype(BF16)
    low = _dot(ab, win_ref[...])
    kr = _dot(ab, wkr_ref[...])
    if rope:
        kr = _rope_pad(kr, c, s1, s2)
    krb = kr.astype(BF16)
    if want_q:
        cq = _rms(low[:, :MLA_Q_RANK], qn_ref[...]).astype(BF16)
        for h in range(MLA_HEADS):
            acc = _dot(cq, wuq_ref[:, h * MLA_QK_PAD:(h + 1) * MLA_QK_PAD])
            qr = acc[:, MLA_NOPE:]
            if rope:
                qr = _rope_pad(qr, c, s1, s2)
            q_ref[:, h * MLA_QK_PAD:h * MLA_QK_PAD + MLA_NOPE] = (acc[:, :MLA_NOPE] * scale).astype(BF16)
            q_ref[:, h * MLA_QK_PAD + MLA_NOPE:(h + 1) * MLA_QK_PAD] = (qr * scale).astype(BF16)
    ckv = _rms(low[:, MLA_Q_RANK:], kvn_ref[...]).astype(BF16)
    hw = MLA_NOPE + MLA_V
    for h in range(MLA_HEADS):
        acc = _dot(ckv, wukv_ref[:, h * hw:(h + 1) * hw])
        k_ref[:, h * MLA_QK_PAD:h * MLA_QK_PAD + MLA_NOPE] = acc[:, :MLA_NOPE].astype(BF16)
        k_ref[:, h * MLA_QK_PAD + MLA_NOPE:(h + 1) * MLA_QK_PAD] = krb
        v_ref[:, h * MLA_V:(h + 1) * MLA_V] = acc[:, MLA_NOPE:].astype(BF16)


def _mla_proj(h, g, shift, scale_m, w_in_bf, w_kr_bf, q_norm, kv_norm, w_uq_bf, w_ukv_bf, rope_tabs,
              row_fn, seq, want_q):
    r, d = h.shape
    tm = _tile(seq, 512)
    rope = rope_tabs is not None
    in_specs = [
        pl.BlockSpec((tm, d), lambda i: (i, 0)),
        _const_spec((1, d)),
        _mod_spec(d, lambda i: row_fn(i * tm)),
        _mod_spec(d, lambda i: row_fn(i * tm)),
        _const_spec(w_in_bf.shape),
        _const_spec(w_kr_bf.shape),
        _const_spec((1, MLA_Q_RANK)),
        _const_spec((1, MLA_KV_RANK)),
    ]
    args = [h, g, shift, scale_m, w_in_bf, w_kr_bf, q_norm, kv_norm]
    if want_q:
        in_specs.append(_const_spec(w_uq_bf.shape))
        args.append(w_uq_bf)
    in_specs.append(_const_spec(w_ukv_bf.shape))
    args.append(w_ukv_bf)
    if rope:
        nt = seq // tm
        in_specs += [pl.BlockSpec((tm, LANES), lambda i: (i % nt, 0))] * 3
        args += list(rope_tabs)
    kcols = MLA_HEADS * MLA_QK_PAD
    vcols = MLA_HEADS * MLA_V
    out_specs = [pl.BlockSpec((tm, kcols), lambda i: (i, 0)), pl.BlockSpec((tm, vcols), lambda i: (i, 0))]
    out_shape = [jax.ShapeDtypeStruct((r, kcols), BF16), jax.ShapeDtypeStruct((r, vcols), BF16)]
    if want_q:
        out_specs.insert(0, pl.BlockSpec((tm, kcols), lambda i: (i, 0)))
        out_shape.insert(0, jax.ShapeDtypeStruct((r, kcols), BF16))
    return pl.pallas_call(
        functools.partial(_mla_proj_kernel, rope=rope, want_q=want_q,
                          scale=(MLA_NOPE + MLA_ROPE) ** -0.5),
        grid=(r // tm,),
        in_specs=in_specs,
        out_specs=out_specs,
        out_shape=out_shape,
        compiler_params=_params("arbitrary"),
        name="mla_proj",
    )(*args)


def _attn_kernel(*refs, n_seg, band, use_sink, tq, seq):
    refs = list(refs)
    if use_sink:
        sink_ref = refs.pop(0)
    q_ref = refs.pop(0)
    o_ref = refs.pop()
    q = q_ref[...]
    segs = []
    for s_i in range(n_seg):
        k_ref, v_ref = refs[2 * s_i], refs[2 * s_i + 1]
        if band and s_i == n_seg - 1:
            wk = tq + 2 * WINDOW
            q0 = pl.program_id(2) * tq
            start = pl.multiple_of(jnp.clip(q0 - WINDOW, 0, seq - wk), LANES)
            s = _dot_nt(q, k_ref[pl.ds(start, wk), :])
            qpos = q0 + lax.broadcasted_iota(I32, (tq, wk), 0)
            kpos = start + lax.broadcasted_iota(I32, (tq, wk), 1)
            s = jnp.where(jnp.abs(kpos - qpos) <= WINDOW, s, -jnp.inf)
            segs.append((s, v_ref[pl.ds(start, wk), :]))
        else:
            segs.append((_dot_nt(q, k_ref[...]), v_ref[...]))
    m = segs[0][0].max(axis=-1, keepdims=True)
    for s, _ in segs[1:]:
        m = jnp.maximum(m, s.max(axis=-1, keepdims=True))
    if use_sink:
        sink = sink_ref[pl.program_id(1)]
        m = jnp.maximum(m, sink)
    den = None
    out = None
    for s, v in segs:
        p = jnp.exp(s - m)
        ps = p.sum(axis=-1, keepdims=True)
        po = _dot(p.astype(BF16), v)
        den = ps if den is None else den + ps
        out = po if out is None else out + po
    if use_sink:
        den = den + jnp.exp(sink - m)
    o_ref[...] = (out / den).astype(o_ref.dtype)


def _attention(q_arr, q_col0, segs, sink, *, batch, seq, n_heads, group, dq, dv, band, tq_cap):
    tq = _tile(seq, tq_cap)
    nq = seq // tq
    use_sink = sink is not None
    in_specs, args = [], []
    if use_sink:
        in_specs.append(pl.BlockSpec(memory_space=pltpu.SMEM))
        args.append(sink)
    in_specs.append(pl.BlockSpec((tq, dq), lambda b, h, i: (b * nq + i, q_col0 + h)))
    args.append(q_arr)
    for k_arr, k_col0, v_arr, v_col0, slen in segs:
        in_specs.append(pl.BlockSpec((slen, dq), lambda b, h, i, c=k_col0: (b, c + h // group)))
        in_specs.append(pl.BlockSpec((slen, dv), lambda b, h, i, c=v_col0: (b, c + h // group)))
        args += [k_arr, v_arr]
    return pl.pallas_call(
        functools.partial(_attn_kernel, n_seg=len(segs), band=band, use_sink=use_sink, tq=tq, seq=seq),
        grid=(batch, n_heads, nq),
        in_specs=in_specs,
        out_specs=pl.BlockSpec((tq, dv), lambda b, h, i: (b * nq + i, h)),
        out_shape=jax.ShapeDtypeStruct((batch * seq, n_heads * dv), BF16),
        compiler_params=_params("arbitrary", "arbitrary", "arbitrary"),
        name="attention",
    )(*args)


def _outproj_kernel(*refs, n_in):
    o_refs = refs[:n_in]
    (w_ref, h_ref, g1_ref, gate_ref, g2_ref, sh_ref, sc_ref, wr_ref, hn_ref, xf_ref, aff_ref) = refs[n_in:]
    y = None
    r0 = 0
    for o_ref in o_refs:
        kk = o_ref.shape[1]
        part = _dot(o_ref[...], w_ref[r0:r0 + kk, :])
        y = part if y is None else y + part
        r0 += kk
    hn = h_ref[...] + gate_ref[...] * _rms(y, g1_ref[...])
    hn_ref[...] = hn
    xf = _rms(hn, g2_ref[...]) * (1.0 + sc_ref[...]) + sh_ref[...]
    xf_ref[...] = xf.astype(BF16)
    logits = lax.dot_general(wr_ref[...], xf, (((1,), (1,)), ((), ())),
                             precision=lax.Precision.HIGHEST, preferred_element_type=F32)
    e = jnp.exp(logits - logits.max(axis=0, keepdims=True))
    aff_ref[...] = e / e.sum(axis=0, keepdims=True)


def _outproj(o_list, w_bf, h, g1, gate, g2, shift, scale_m, wr_t, row_fn, seq):
    r, d = h.shape
    tm = _tile(seq, 512)
    n_e = wr_t.shape[0]
    in_specs = [pl.BlockSpec((tm, o.shape[1]), lambda i: (i, 0)) for o in o_list]
    in_specs += [
        _const_spec(w_bf.shape),
        pl.BlockSpec((tm, d), lambda i: (i, 0)),
        _const_spec((1, d)),
        _mod_spec(d, lambda i: row_fn(i * tm)),
        _const_spec((1, d)),
        _mod_spec(d, lambda i: row_fn(i * tm)),
        _mod_spec(d, lambda i: row_fn(i * tm)),
        _const_spec((n_e, d)),
    ]
    return pl.pallas_call(
        functools.partial(_outproj_kernel, n_in=len(o_list)),
        grid=(r // tm,),
        in_specs=in_specs,
        out_specs=[
            pl.BlockSpec((tm, d), lambda i: (i, 0)),
            pl.BlockSpec((tm, d), lambda i: (i, 0)),
            pl.BlockSpec((n_e, tm), lambda i: (0, i)),
        ],
        out_shape=[
            jax.ShapeDtypeStruct((r, d), F32),
            jax.ShapeDtypeStruct((r, d), BF16),
            jax.ShapeDtypeStruct((n_e, r), F32),
        ],
        compiler_params=_params("arbitrary"),
        name="outproj",
    )(*o_list, w_bf, h, g1, gate, g2, shift, scale_m, wr_t)


def _route_kernel(aff_ref, posm_ref, pose_ref, gsel_ref, *, cap):
    n_e, n = aff_ref.shape
    bits = pltpu.bitcast(aff_ref[...], I32)
    capf = float(cap)

    def search(_, carry):
        lo, hi = carry
        mid = lo + ((hi - lo + 1) >> 1)
        cnt = jnp.sum(jnp.where(bits >= mid, 1.0, 0.0), axis=1, keepdims=True)
        ok = cnt >= capf
        return jnp.where(ok, mid, lo), jnp.where(ok, hi, mid - 1)

    lo0 = jnp.zeros((n_e, 1), I32)
    hi0 = jnp.full((n_e, 1), 0x7F800000, I32)
    thr, _ = lax.fori_loop(0, 31, search, (lo0, hi0))
    need = capf - jnp.sum(jnp.where(bits > thr, 1.0, 0.0), axis=1, keepdims=True)
    upper = jnp.where(lax.broadcasted_iota(I32, (LANES, LANES), 0) < lax.broadcasted_iota(I32, (LANES, LANES), 1),
                      1.0, 0.0).astype(BF16)
    run_eq = jnp.zeros((n_e, 1), F32)
    run_sel = jnp.zeros((n_e, 1), F32)
    for j in range(n // LANES):
        sl = slice(j * LANES, (j + 1) * LANES)
        a = aff_ref[:, sl]
        bb = pltpu.bitcast(a, I32)
        eq = jnp.where(bb == thr, 1.0, 0.0)
        rank = _dot(eq.astype(BF16), upper) + run_eq
        run_eq = run_eq + eq.sum(axis=1, keepdims=True)
        sel = (bb > thr) | ((bb == thr) & (rank < need))
        self_f = jnp.where(sel, 1.0, 0.0)
        pos = _dot(self_f.astype(BF16), upper) + run_sel
        run_sel = run_sel + self_f.sum(axis=1, keepdims=True)
        pos_i = pos.astype(I32)
        pose_ref[0, :, sl] = pos_i
        posm_ref[0, :, sl] = jnp.where(sel, pos_i, -1)
        gsel_ref[0, :, sl] = jnp.where(sel, a, 0.0)


def _route(aff, batch, n, cap):
    n_e = aff.shape[0]
    spec = pl.BlockSpec((1, n_e, n), lambda b: (b, 0, 0))
    return pl.pallas_call(
        functools.partial(_route_kernel, cap=cap),
        grid=(batch,),
        in_specs=[pl.BlockSpec((n_e, n), lambda b: (0, b))],
        out_specs=[spec, spec, spec],
        out_shape=[
            jax.ShapeDtypeStruct((batch, n_e, n), I32),
            jax.ShapeDtypeStruct((batch, n_e, n), I32),
            jax.ShapeDtypeStruct((batch, n_e, n), F32),
        ],
        compiler_params=_params("arbitrary"),
        name="route",
    )(aff)


def _gather_kernel(cs_ref, posm_ref, gsel_ref, x_ref, xg_ref, gs_ref, acc_ref, gacc_ref, *, ts, tc, n_chunks,
                   n_e):
    b, e, j = pl.program_id(0), pl.program_id(1), pl.program_id(2)
    s0 = j * ts
    base = (b * n_e + e) * (n_chunks + 1)
    acc_ref[...] = jnp.zeros_like(acc_ref)
    gacc_ref[...] = jnp.zeros_like(gacc_ref)
    slot = s0 + lax.broadcasted_iota(I32, (ts, tc), 0)
    for c in range(n_chunks):
        lo, hi = cs_ref[base + c], cs_ref[base + c + 1]

        @pl.when((lo < s0 + ts) & (hi > s0))
        def _():
            hit = posm_ref[0, 0, c:c + 1, :] == slot
            onehot = jnp.where(hit, 1.0, 0.0).astype(BF16)
            acc_ref[...] += _dot(onehot, x_ref[c * tc:(c + 1) * tc, :])
            gacc_ref[...] += jnp.where(hit, gsel_ref[0, 0, c:c + 1, :], 0.0).sum(axis=1, keepdims=True)

    xg_ref[0] = acc_ref[...].astype(BF16)
    gs_ref[0] = gacc_ref[...]


def _gather(xf, posm, pose, gsel, batch, n, cap):
    d = xf.shape[1]
    n_e = posm.shape[1]
    tc = _tile(n, 256)
    ts = _tile(cap, 256)
    n_chunks = n // tc
    cstart = jnp.concatenate([pose[:, :, ::tc], jnp.full((batch, n_e, 1), cap, I32)], axis=-1).reshape(-1)
    posm4 = posm.reshape(batch, n_e, n_chunks, tc)
    gsel4 = gsel.reshape(batch, n_e, n_chunks, tc)
    nj = cap // ts
    grid_spec = pltpu.PrefetchScalarGridSpec(
        num_scalar_prefetch=1,
        grid=(batch, n_e, nj),
        in_specs=[
            pl.BlockSpec((1, 1, n_chunks, tc), lambda b, e, j, cs: (b, e, 0, 0)),
            pl.BlockSpec((1, 1, n_chunks, tc), lambda b, e, j, cs: (b, e, 0, 0)),
            pl.BlockSpec((n, d), lambda b, e, j, cs: (b, 0)),
        ],
        out_specs=[
            pl.BlockSpec((1, ts, d), lambda b, e, j, cs: (e, b * nj + j, 0)),
            pl.BlockSpec((1, ts, 1), lambda b, e, j, cs: (e, b * nj + j, 0)),
        ],
        scratch_shapes=[pltpu.VMEM((ts, d), F32), pltpu.VMEM((ts, 1), F32)],
    )
    return pl.pallas_call(
        functools.partial(_gather_kernel, ts=ts, tc=tc, n_chunks=n_chunks, n_e=n_e),
        grid_spec=grid_spec,
        out_shape=[
            jax.ShapeDtypeStruct((n_e, batch * cap, d), BF16),
            jax.ShapeDtypeStruct((n_e, batch * cap, 1), F32),
        ],
        compiler_params=_params("arbitrary", "arbitrary", "arbitrary"),
        name="moe_gather",
    )(cstart, posm4, gsel4, xf)


def _ffn_kernel(x_ref, wg_ref, wu_ref, wd_ref, gs_ref, y_ref, acc_ref):
    f = pl.program_id(2)

    @pl.when(f == 0)
    def _():
        acc_ref[...] = jnp.zeros_like(acc_ref)

    x = x_ref[0]
    hg = _dot(x, wg_ref[0].astype(BF16))
    hu = _dot(x, wu_ref[0].astype(BF16))
    hid = (_silu(hg) * hu).astype(BF16)
    acc_ref[...] += _dot(hid, wd_ref[0].astype(BF16))

    @pl.when(f == pl.num_programs(2) - 1)
    def _():
        y_ref[0] = (acc_ref[...] * gs_ref[0]).astype(BF16)


def _expert_ffn(xg, gslot, w_gate, w_up, w_down):
    n_e, m, d = xg.shape
    ff = w_gate.shape[2]
    tm = _tile(m, 1024)
    tf = _tile(ff, 512)
    return pl.pallas_call(
        _ffn_kernel,
        grid=(n_e, m // tm, ff // tf),
        in_specs=[
            pl.BlockSpec((1, tm, d), lambda e, i, f: (e, i, 0)),
            pl.BlockSpec((1, d, tf), lambda e, i, f: (e, 0, f)),
            pl.BlockSpec((1, d, tf), lambda e, i, f: (e, 0, f)),
            pl.BlockSpec((1, tf, d), lambda e, i, f: (e, f, 0)),
            pl.BlockSpec((1, tm, 1), lambda e, i, f: (e, i, 0)),
        ],
        out_specs=pl.BlockSpec((1, tm, d), lambda e, i, f: (e, i, 0)),
        out_shape=jax.ShapeDtypeStruct((n_e, m, d), BF16),
        scratch_shapes=[pltpu.VMEM((tm, d), F32)],
        compiler_params=_params("arbitrary", "arbitrary", "arbitrary"),
        name="expert_ffn",
    )(xg, w_gate, w_up, w_down, gslot)


def _combine_kernel(cs_ref, posm_ref, y_ref, h_ref, g_ref, gate_ref, o_ref, *, tt, win, cap, n_e, n_tiles):
    b, i = pl.program_id(0), pl.program_id(1)
    acc = None
    for e in range(n_e):
        if cap <= win:
            a0 = 0
            yw = y_ref[e]
        else:
            s0 = cs_ref[(b * n_e + e) * n_tiles + i]
            a0 = jnp.minimum((s0 // BF16_SUBLANES) * BF16_SUBLANES, cap - win)
            a0 = pl.multiple_of(a0, BF16_SUBLANES)
            yw = y_ref[e, pl.ds(a0, win), :]
        wn = yw.shape[0]
        slot = a0 + lax.broadcasted_iota(I32, (wn, tt), 0)
        onehot = jnp.where(posm_ref[0, e:e + 1, :] == slot, 1.0, 0.0).astype(BF16)
        part = lax.dot_general(onehot, yw, (((0,), (0,)), ((), ())), preferred_element_type=F32)
        acc = part if acc is None else acc + part
    o_ref[...] = h_ref[...] + gate_ref[...] * _rms(acc, g_ref[...])


def _combine(y, posm, pose, h, g, gate, row_fn, batch, n, cap):
    n_e = y.shape[0]
    d = y.shape[2]
    tt = LANES
    win = tt + BF16_SUBLANES
    n_tiles = n // tt
    cstart = pose[:, :, ::tt].reshape(-1)
    grid_spec = pltpu.PrefetchScalarGridSpec(
        num_scalar_prefetch=1,
        grid=(batch, n_tiles),
        in_specs=[
            pl.BlockSpec((1, n_e, tt), lambda b, i, cs: (b, 0, i)),
            pl.BlockSpec((n_e, cap, d), lambda b, i, cs: (0, b, 0), pipeline_mode=pl.Buffered(1)),
            pl.BlockSpec((tt, d), lambda b, i, cs: (b * n_tiles + i, 0)),
            pl.BlockSpec((1, d), lambda b, i, cs: (0, 0)),
            pl.BlockSpec((None, 1, d), lambda b, i, cs: (row_fn(b * n), 0, 0)),
        ],
        out_specs=pl.BlockSpec((tt, d), lambda b, i, cs: (b * n_tiles + i, 0)),
    )
    return pl.pallas_call(
        functools.partial(_combine_kernel, tt=tt, win=win, cap=cap, n_e=n_e, n_tiles=n_tiles),
        grid_spec=grid_spec,
        out_shape=jax.ShapeDtypeStruct(h.shape, F32),
        compiler_params=_params("arbitrary", "arbitrary"),
        name="moe_combine",
    )(cstart, posm, y, h, g, gate)


def _moe(xf, aff, h, g3, gate, w_gate, w_up, w_down, row_fn, batch, n):
    cap = CAPACITY_FACTOR * n // N_EXPERTS
    posm, pose, gsel = _route(aff, batch, n, cap)
    xg, gslot = _gather(xf, posm, pose, gsel, batch, n, cap)
    y = _expert_ffn(xg, gslot, w_gate, w_up, w_down)
    return _combine(y, posm, pose, h, g3, gate, row_fn, batch, n, cap)


def _rope_angles(seq, rot_dim):
    n_rows = seq // GRID_W
    row = jnp.repeat(jnp.arange(n_rows), GRID_W)
    col = jnp.tile(jnp.arange(GRID_W), n_rows)
    n_freq = rot_dim // 4
    inv = ROPE_BASE ** (-jnp.arange(n_freq, dtype=F32) / n_freq)
    ang = jnp.concatenate([row[:, None] * inv, col[:, None] * inv], axis=-1)
    return jnp.cos(ang), jnp.sin(ang)


def kernel(x, c, ctx, c_ctx, mod_w, mod_b, norm_g, ev_w_in, ev_w_out, ev_sink, ev_qk_norm, od_w_in, od_q_norm,
           od_kv_norm, od_w_uq, od_w_ukv, od_w_out, router_w, exp_w_gate, exp_w_up, exp_w_down):
    batch, seq, d = x.shape
    n_ctx = ctx.shape[1]
    depth = mod_w.shape[0]
    assert batch < MOD_ROWS and seq % LANES == 0 and n_ctx % LANES == 0 and seq >= 4 * WINDOW

    cs = jnp.concatenate([c, c_ctx[None, :], jnp.zeros((MOD_ROWS - batch - 1, d), F32)], axis=0)
    mods = _modulation(cs, mod_w, mod_b).reshape(depth, MOD_ROWS, N_MOD, 1, d)

    def lat_row(r):
        return r // seq

    def ctx_row(r):
        return batch

    cos_h, sin_h = _rope_angles(seq, HEAD_DIM)
    rope_even = (jnp.concatenate([cos_h, cos_h], axis=-1), jnp.concatenate([-sin_h, sin_h], axis=-1))
    cos_m, sin_m = _rope_angles(seq, MLA_ROPE)
    half = MLA_ROPE // 2
    zeros = lambda w: jnp.zeros((seq, w), F32)
    rope_mla = (
        jnp.concatenate([cos_m, cos_m, zeros(LANES - MLA_ROPE)], axis=-1),
        jnp.concatenate([-sin_m, zeros(LANES - half)], axis=-1),
        jnp.concatenate([zeros(half), sin_m, zeros(LANES - MLA_ROPE)], axis=-1),
    )

    h_lat = x.reshape(batch * seq, d)
    h_ctx = ctx.reshape(batch * n_ctx, d)
    for layer in range(depth):
        with_ctx = layer < depth - 1
        i = layer // 2
        g = norm_g[layer].reshape(4, 1, d)
        m = [mods[layer, :, k] for k in range(N_MOD)]
        wr_t = router_w[layer].T
        if layer % 2 == 0:
            w_in = ev_w_in[i].astype(BF16)
            w_out = ev_w_out[i].astype(BF16)
            qkv_l = _even_inproj(h_lat, g[0], m[0], m[1], w_in, ev_qk_norm[i], rope_even, lat_row, seq)
            qkv_c = _even_inproj(h_ctx, g[0], m[0], m[1], w_in, ev_qk_norm[i], None, ctx_row, n_ctx)
            ga, gb = A_HEADS // A_KV_HEADS, B_HEADS // B_KV_HEADS
            o_a = _attention(qkv_l, AQ0, [(qkv_c, AK0, qkv_c, AV0, n_ctx), (qkv_l, AK0, qkv_l, AV0, seq)],
                             ev_sink[i], batch=batch, seq=seq, n_heads=A_HEADS, group=ga, dq=HEAD_DIM,
                             dv=HEAD_DIM, band=True, tq_cap=256)
            o_b = _attention(qkv_l, BQ0, [(qkv_c, BK0, qkv_c, BV0, n_ctx), (qkv_l, BK0, qkv_l, BV0, seq)],
                             None, batch=batch, seq=seq, n_heads=B_HEADS, group=gb, dq=HEAD_DIM,
                             dv=HEAD_DIM, band=False, tq_cap=512)
            o_lat = [o_a, o_b]
            if with_ctx:
                o_ac = _attention(qkv_c, AQ0, [(qkv_c, AK0, qkv_c, AV0, n_ctx)], ev_sink[i], batch=batch,
                                  seq=n_ctx, n_heads=A_HEADS, group=ga, dq=HEAD_DIM, dv=HEAD_DIM, band=False,
                                  tq_cap=512)
                o_bc = _attention(qkv_c, BQ0, [(qkv_c, BK0, qkv_c, BV0, n_ctx)], None, batch=batch,
                                  seq=n_ctx, n_heads=B_HEADS, group=gb, dq=HEAD_DIM, dv=HEAD_DIM, band=False,
                                  tq_cap=512)
                o_ctx = [o_ac, o_bc]
        else:
            w_in = od_w_in[i]
            w_qkv = w_in[:, :MLA_Q_RANK + MLA_KV_RANK].astype(BF16)
            w_kr = jnp.pad(w_in[:, MLA_Q_RANK + MLA_KV_RANK:], ((0, 0), (0, LANES - MLA_ROPE))).astype(BF16)
            w_uq = od_w_uq[i].reshape(MLA_Q_RANK, MLA_HEADS, MLA_NOPE + MLA_ROPE)
            w_uq = jnp.pad(w_uq, ((0, 0), (0, 0), (0, MLA_QK_PAD - MLA_NOPE - MLA_ROPE)))
            w_uq = w_uq.reshape(MLA_Q_RANK, MLA_HEADS * MLA_QK_PAD).astype(BF16)
            w_ukv = od_w_ukv[i].astype(BF16)
            w_out = od_w_out[i].astype(BF16)
            qn, kvn = od_q_norm[i].reshape(1, -1), od_kv_norm[i].reshape(1, -1)
            q_l, k_l, v_l = _mla_proj(h_lat, g[0], m[0], m[1], w_qkv, w_kr, qn, kvn, w_uq, w_ukv, rope_mla,
                                      lat_row, seq, True)
            proj_c = _mla_proj(h_ctx, g[0], m[0], m[1], w_qkv, w_kr, qn, kvn, w_uq, w_ukv, None, ctx_row,
                               n_ctx, with_ctx)
            k_c, v_c = proj_c[-2], proj_c[-1]
            o_lat = [_attention(q_l, 0, [(k_c, 0, v_c, 0, n_ctx), (k_l, 0, v_l, 0, seq)], None, batch=batch,
                                seq=seq, n_heads=MLA_HEADS, group=1, dq=MLA_QK_PAD, dv=MLA_V, band=False,
                                tq_cap=512)]
            if with_ctx:
                o_ctx = [_attention(proj_c[0], 0, [(k_c, 0, v_c, 0, n_ctx)], None, batch=batch, seq=n_ctx,
                                    n_heads=MLA_HEADS, group=1, dq=MLA_QK_PAD, dv=MLA_V, band=False,
                                    tq_cap=512)]
        ew = (exp_w_gate[layer], exp_w_up[layer], exp_w_down[layer])
        h_lat, xf, aff = _outproj(o_lat, w_out, h_lat, g[1], m[2], g[2], m[3], m[4], wr_t, lat_row, seq)
        h_lat = _moe(xf, aff, h_lat, g[3], m[5], *ew, lat_row, batch, seq)
        if with_ctx:
            h_ctx, xf, aff = _outproj(o_ctx, w_out, h_ctx, g[1], m[2], g[2], m[3], m[4], wr_t, ctx_row, n_ctx)
            h_ctx = _moe(xf, aff, h_ctx, g[3], m[5], *ew, ctx_row, batch, n_ctx)
    return h_lat.reshape(batch, seq, d)
```

```python
import functools

import jax
import jax.numpy as jnp
from jax import lax
from jax.experimental import pallas as pl
from jax.experimental.pallas import tpu as pltpu

F32 = jnp.float32
BF16 = jnp.bfloat16
I32 = jnp.int32

EPS = 1e-6
GRID_W = 64
WINDOW = 128
ROPE_BASE = 10000.0
HEAD_DIM = 128
A_HEADS = 8
A_KV_HEADS = 2
B_HEADS = 8
B_KV_HEADS = 2
MLA_HEADS = 16
MLA_Q_RANK = 512
MLA_KV_RANK = 512
MLA_NOPE = 128
MLA_ROPE = 64
MLA_V = 128
MLA_QK_PAD = 256
N_EXPERTS = 16
CAPACITY_FACTOR = 2
N_MOD = 6
MOD_ROWS = 8

LANES = 128
BF16_SUBLANES = 16
VMEM_LIMIT_BYTES = 60 * 1024 * 1024
FFN_SUB = 256

LOG2E = 1.4426950408889634

EVEN_IN_COLS = (A_HEADS + 2 * A_KV_HEADS + B_HEADS + 2 * B_KV_HEADS) * HEAD_DIM
AQ0 = 0
BQ0 = AQ0 + A_HEADS
AK0 = BQ0 + B_HEADS
BK0 = AK0 + A_KV_HEADS
AV0 = BK0 + B_KV_HEADS
BV0 = AV0 + 2 * A_KV_HEADS
EVEN_SLOTS = BV0 + 2 * B_KV_HEADS
EVEN_COLS = EVEN_SLOTS * HEAD_DIM
_W_AK0 = A_HEADS
_W_AV0 = _W_AK0 + A_KV_HEADS
_W_BQ0 = _W_AV0 + A_KV_HEADS
_W_BK0 = _W_BQ0 + B_HEADS
_W_BV0 = _W_BK0 + B_KV_HEADS


def _even_head(wh):
    if wh < _W_AK0:
        return AQ0 + wh, "aq"
    if wh < _W_AV0:
        return AK0 + wh - _W_AK0, "ak"
    if wh < _W_BQ0:
        return AV0 + 2 * (wh - _W_AV0), "av"
    if wh < _W_BK0:
        return BQ0 + wh - _W_BQ0, "bq"
    if wh < _W_BV0:
        return BK0 + wh - _W_BK0, "bk"
    return BV0 + 2 * (wh - _W_BV0), "bv"


def _params(*sem):
    return pltpu.CompilerParams(dimension_semantics=sem, vmem_limit_bytes=VMEM_LIMIT_BYTES)


def _tile(n, cap):
    t = min(n, cap)
    while n % t:
        t -= 1
    return t


def _const_spec(shape):
    nd = len(shape)
    return pl.BlockSpec(shape, lambda *_: (0,) * nd)


def _rms(x, g):
    return x * lax.rsqrt(jnp.mean(x * x, axis=-1, keepdims=True) + EPS) * g


def _silu(x):
    return x / (1.0 + jnp.exp(-x))


def _dot(a, b):
    return jnp.dot(a, b, preferred_element_type=F32)


def _dot_nt(a, b):
    return lax.dot_general(a, b, (((1,), (1,)), ((), ())), preferred_element_type=F32)


def _mod_kernel(cs_ref, w_ref, b_ref, o_ref):
    s = _silu(cs_ref[...])
    o_ref[0] = jnp.dot(s, w_ref[0], precision=lax.Precision.HIGHEST, preferred_element_type=F32) + b_ref[0]


def _modulation(cs, mod_w, mod_b):
    depth, d, n6 = mod_w.shape
    tn = _tile(n6, 1024)
    return pl.pallas_call(
        _mod_kernel,
        grid=(depth, n6 // tn),
        in_specs=[
            _const_spec((MOD_ROWS, d)),
            pl.BlockSpec((1, d, tn), lambda l, j: (l, 0, j)),
            pl.BlockSpec((1, 1, tn), lambda l, j: (l, 0, j)),
        ],
        out_specs=pl.BlockSpec((1, MOD_ROWS, tn), lambda l, j: (l, 0, j)),
        out_shape=jax.ShapeDtypeStruct((depth, MOD_ROWS, n6), F32),
        compiler_params=_params("arbitrary", "arbitrary"),
        name="modulation",
    )(cs, mod_w, mod_b.reshape(depth, 1, n6))


def _mod_spec(d, row_fn):
    return pl.BlockSpec((None, 1, d), lambda i: (row_fn(i), 0, 0))


def _even_inproj_kernel(*refs, rope, scale):
    if rope:
        x_ref, g_ref, sh_ref, sc_ref, w_ref, qkg_ref, cos_ref, sin_ref, o_ref = refs
        cos, sin = cos_ref[...], sin_ref[...]
    else:
        x_ref, g_ref, sh_ref, sc_ref, w_ref, qkg_ref, o_ref = refs
    a = _rms(x_ref[...], g_ref[...]) * (1.0 + sc_ref[...]) + sh_ref[...]
    ab = a.astype(BF16)
    ones = jnp.ones((x_ref.shape[0], HEAD_DIM), BF16)
    for j in range(EVEN_IN_COLS // (2 * HEAD_DIM)):
        acc = _dot(ab, w_ref[:, j * 2 * HEAD_DIM:(j + 1) * 2 * HEAD_DIM])
        for hh in range(2):
            slot, kind = _even_head(2 * j + hh)
            v = acc[:, hh * HEAD_DIM:(hh + 1) * HEAD_DIM]
            if kind == "bq":
                v = _rms(v, qkg_ref[0:1, :])
            elif kind == "bk":
                v = _rms(v, qkg_ref[1:2, :])
            if rope and kind[1] != "v":
                v = v * cos + pltpu.roll(v, HEAD_DIM // 2, 1) * sin
            if kind[1] == "q":
                v = v * scale
            o_ref[:, slot * HEAD_DIM:(slot + 1) * HEAD_DIM] = v.astype(BF16)
            if kind[1] == "v":
                o_ref[:, (slot + 1) * HEAD_DIM:(slot + 2) * HEAD_DIM] = ones


def _even_inproj(h, g, shift, scale_m, w_bf, qk_gain, rope_tabs, row_fn, seq):
    r, d = h.shape
    tm = _tile(seq, 512)
    rope = rope_tabs is not None
    in_specs = [
        pl.BlockSpec((tm, d), lambda i: (i, 0)),
        _const_spec((1, d)),
        _mod_spec(d, lambda i: row_fn(i * tm)),
        _mod_spec(d, lambda i: row_fn(i * tm)),
        _const_spec((d, EVEN_IN_COLS)),
        _const_spec((2, HEAD_DIM)),
    ]
    args = [h, g, shift, scale_m, w_bf, qk_gain]
    if rope:
        nt = seq // tm
        in_specs += [pl.BlockSpec((tm, HEAD_DIM), lambda i: (i % nt, 0))] * 2
        args += list(rope_tabs)
    return pl.pallas_call(
        functools.partial(_even_inproj_kernel, rope=rope, scale=HEAD_DIM ** -0.5 * LOG2E),
        grid=(r // tm,),
        in_specs=in_specs,
        out_specs=pl.BlockSpec((tm, EVEN_COLS), lambda i: (i, 0)),
        out_shape=jax.ShapeDtypeStruct((r, EVEN_COLS), BF16),
        compiler_params=_params("arbitrary"),
        name="even_inproj",
    )(*args)


def _rope_pad(v, c, s1, s2):
    return v * c + pltpu.roll(v, LANES - MLA_ROPE // 2, 1) * s1 + pltpu.roll(v, MLA_ROPE // 2, 1) * s2


def _mla_proj_kernel(*refs, rope, want_q, scale):
    refs = list(refs)
    x_ref, g_ref, sh_ref, sc_ref, win_ref, wkr_ref, qn_ref, kvn_ref = refs[:8]
    refs = refs[8:]
    if want_q:
        wuq_ref = refs.pop(0)
    wukv_ref = refs.pop(0)
    if rope:
        c, s1, s2 = refs[0][...], refs[1][...], refs[2][...]
        refs = refs[3:]
    if want_q:
        q_ref = refs.pop(0)
    k_ref, v_ref = refs
    a = _rms(x_ref[...], g_ref[...]) * (1.0 + sc_ref[...]) + sh_ref[...]
    ab = a.astype(BF16)
    low = _dot(ab, win_ref[...])
    kr = _dot(ab, wkr_ref[...])
    if rope:
        kr = _rope_pad(kr, c, s1, s2)
    krb = kr.astype(BF16)
    if want_q:
        cq = _rms(low[:, :MLA_Q_RANK], qn_ref[...]).astype(BF16)
        for h in range(MLA_HEADS):
            acc = _dot(cq, wuq_ref[:, h * MLA_QK_PAD:(h + 1) * MLA_QK_PAD])
            qr = acc[:, MLA_NOPE:]
            if rope:
                qr = _rope_pad(qr, c, s1, s2)
            q_ref[:, h * MLA_QK_PAD:h * MLA_QK_PAD + MLA_NOPE] = (acc[:, :MLA_NOPE] * scale).astype(BF16)
            q_ref[:, h * MLA_QK_PAD + MLA_NOPE:(h + 1) * MLA_QK_PAD] = (qr * scale).astype(BF16)
    ckv = _rms(low[:, MLA_Q_RANK:], kvn_ref[...]).astype(BF16)
    hw = MLA_NOPE + MLA_V
    for h in range(MLA_HEADS):
        acc = _dot(ckv, wukv_ref[:, h * hw:(h + 1) * hw])
        k_ref[:, h * MLA_QK_PAD:h * MLA_QK_PAD + MLA_NOPE] = acc[:, :MLA_NOPE].astype(BF16)
        k_ref[:, h * MLA_QK_PAD + MLA_NOPE:(h + 1) * MLA_QK_PAD] = krb
        v_ref[:, 2 * h * MLA_V:(2 * h + 1) * MLA_V] = acc[:, MLA_NOPE:].astype(BF16)
        v_ref[:, (2 * h + 1) * MLA_V:(2 * h + 2) * MLA_V] = jnp.ones((x_ref.shape[0], MLA_V), BF16)


def _mla_proj(h, g, shift, scale_m, w_in_bf, w_kr_bf, q_norm, kv_norm, w_uq_bf, w_ukv_bf, rope_tabs,
              row_fn, seq, want_q):
    r, d = h.shape
    tm = _tile(seq, 512)
    rope = rope_tabs is not None
    in_specs = [
        pl.BlockSpec((tm, d), lambda i: (i, 0)),
        _const_spec((1, d)),
        _mod_spec(d, lambda i: row_fn(i * tm)),
        _mod_spec(d, lambda i: row_fn(i * tm)),
        _const_spec(w_in_bf.shape),
        _const_spec(w_kr_bf.shape),
        _const_spec((1, MLA_Q_RANK)),
        _const_spec((1, MLA_KV_RANK)),
    ]
    args = [h, g, shift, scale_m, w_in_bf, w_kr_bf, q_norm, kv_norm]
    if want_q:
        in_specs.append(_const_spec(w_uq_bf.shape))
        args.append(w_uq_bf)
    in_specs.append(_const_spec(w_ukv_bf.shape))
    args.append(w_ukv_bf)
    if rope:
        nt = seq // tm
        in_specs += [pl.BlockSpec((tm, LANES), lambda i: (i % nt, 0))] * 3
        args += list(rope_tabs)
    kcols = MLA_HEADS * MLA_QK_PAD
    vcols = MLA_HEADS * 2 * MLA_V
    out_specs = [pl.BlockSpec((tm, kcols), lambda i: (i, 0)), pl.BlockSpec((tm, vcols), lambda i: (i, 0))]
    out_shape = [jax.ShapeDtypeStruct((r, kcols), BF16), jax.ShapeDtypeStruct((r, vcols), BF16)]
    if want_q:
        out_specs.insert(0, pl.BlockSpec((tm, kcols), lambda i: (i, 0)))
        out_shape.insert(0, jax.ShapeDtypeStruct((r, kcols), BF16))
    return pl.pallas_call(
        functools.partial(_mla_proj_kernel, rope=rope, want_q=want_q,
                          scale=(MLA_NOPE + MLA_ROPE) ** -0.5 * LOG2E),
        grid=(r // tm,),
        in_specs=in_specs,
        out_specs=out_specs,
        out_shape=out_shape,
        compiler_params=_params("arbitrary"),
        name="mla_proj",
    )(*args)


def _attn_kernel(*refs, n_seg, band, use_sink, tq, seq, hp, group, dq, dv, tk):
    refs = list(refs)
    if use_sink:
        sink_ref = refs.pop(0)
    q_ref = refs.pop(0)
    o_ref = refs.pop()
    dvx = 2 * dv
    if band:
        wk = tq + 2 * WINDOW
        q0 = pl.program_id(2) * tq
        start = pl.multiple_of(jnp.clip(q0 - WINDOW, 0, seq - wk), LANES)
        dist = (lax.broadcasted_iota(I32, (tq, wk), 1) - lax.broadcasted_iota(I32, (tq, wk), 0)) + (start - q0)
        valid = jnp.abs(dist) <= WINDOW
    for j in range(hp):
        kv = j // group
        q = q_ref[:, j * dq:(j + 1) * dq]
        m = None
        acc = None
        for s_i in range(n_seg):
            k_ref, v_ref = refs[2 * s_i], refs[2 * s_i + 1]
            slen = k_ref.shape[0]
            masked = band and s_i == n_seg - 1
            chunks = [(start, wk)] if masked else [(c0, min(tk, slen - c0)) for c0 in range(0, slen, tk)]
            for c0, cl in chunks:
                s = _dot_nt(q, k_ref[pl.ds(c0, cl), kv * dq:(kv + 1) * dq])
                if masked:
                    s = jnp.where(valid, s, -jnp.inf)
                v = v_ref[pl.ds(c0, cl), kv * dvx:(kv + 1) * dvx]
                mc = s.max(axis=-1, keepdims=True)
                if m is None:
                    m = mc
                    acc = _dot(jnp.exp2(s - m).astype(BF16), v)
                else:
                    m_new = jnp.maximum(m, mc)
                    acc = jnp.exp2(m - m_new) * acc + _dot(jnp.exp2(s - m_new).astype(BF16), v)
                    m = m_new
        den = acc[:, dv:]
        if use_sink:
            sink = sink_ref[pl.program_id(1) * hp + j] * LOG2E
            den = den + jnp.exp2(sink - m)
        o_ref[:, j * dv:(j + 1) * dv] = (acc[:, :dv] / den).astype(o_ref.dtype)


def _attention(q_arr, q_col0, segs, sink, *, batch, seq, n_heads, group, dq, dv, band, tq_cap, hp, tk=512):
    tq = _tile(seq, tq_cap)
    nq = seq // tq
    use_sink = sink is not None
    if hp <= group:
        assert group % hp == 0
        kvp, kv_of, step_group = 1, (lambda hg: hg // (group // hp)), hp
    else:
        assert hp % group == 0
        kvp, kv_of, step_group = hp // group, (lambda hg: hg), group
    qw, kw, vw = hp * dq, kvp * dq, kvp * 2 * dv
    assert q_col0 % qw == 0
    in_specs, args = [], []
    if use_sink:
        in_specs.append(pl.BlockSpec(memory_space=pltpu.SMEM))
        args.append(sink)
    in_specs.append(pl.BlockSpec((tq, qw), lambda b, hg, i: (b * nq + i, q_col0 // qw + hg)))
    args.append(q_arr)
    for k_arr, k_col0, v_arr, v_col0, slen in segs:
        assert k_col0 % kw == 0 and v_col0 % vw == 0
        in_specs.append(pl.BlockSpec((slen, kw), lambda b, hg, i, c=k_col0 // kw: (b, c + kv_of(hg))))
        in_specs.append(pl.BlockSpec((slen, vw), lambda b, hg, i, c=v_col0 // vw: (b, c + kv_of(hg))))
        args += [k_arr, v_arr]
    return pl.pallas_call(
        functools.partial(_attn_kernel, n_seg=len(segs), band=band, use_sink=use_sink, tq=tq, seq=seq, hp=hp,
                          group=step_group, dq=dq, dv=dv, tk=tk),
        grid=(batch, n_heads // hp, nq),
        in_specs=in_specs,
        out_specs=pl.BlockSpec((tq, hp * dv), lambda b, hg, i: (b * nq + i, hg)),
        out_shape=jax.ShapeDtypeStruct((batch * seq, n_heads * dv), BF16),
        compiler_params=_params("arbitrary", "arbitrary", "arbitrary"),
        name="attention",
    )(*args)


def _outproj_kernel(*refs, n_in):
    o_refs = refs[:n_in]
    (w_ref, h_ref, g1_ref, gate_ref, g2_ref, sh_ref, sc_ref, wr_ref, hn_ref, xf_ref, aff_ref) = refs[n_in:]
    y = None
    r0 = 0
    for o_ref in o_refs:
        kk = o_ref.shape[1]
        part = _dot(o_ref[...], w_ref[r0:r0 + kk, :])
        y = part if y is None else y + part
        r0 += kk
    hn = h_ref[...] + gate_ref[...] * _rms(y, g1_ref[...])
    hn_ref[...] = hn
    xf = _rms(hn, g2_ref[...]) * (1.0 + sc_ref[...]) + sh_ref[...]
    xf_ref[...] = xf.astype(BF16)
    logits = lax.dot_general(wr_ref[...], xf, (((1,), (1,)), ((), ())),
                             precision=lax.Precision.HIGHEST, preferred_element_type=F32)
    e = jnp.exp(logits - logits.max(axis=0, keepdims=True))
    aff_ref[...] = e / e.sum(axis=0, keepdims=True)


def _outproj(o_list, w_bf, h, g1, gate, g2, shift, scale_m, wr_t, row_fn, seq):
    r, d = h.shape
    tm = _tile(seq, 512)
    n_e = wr_t.shape[0]
    in_specs = [pl.BlockSpec((tm, o.shape[1]), lambda i: (i, 0)) for o in o_list]
    in_specs += [
        _const_spec(w_bf.shape),
        pl.BlockSpec((tm, d), lambda i: (i, 0)),
        _const_spec((1, d)),
        _mod_spec(d, lambda i: row_fn(i * tm)),
        _const_spec((1, d)),
        _mod_spec(d, lambda i: row_fn(i * tm)),
        _mod_spec(d, lambda i: row_fn(i * tm)),
        _const_spec((n_e, d)),
    ]
    return pl.pallas_call(
        functools.partial(_outproj_kernel, n_in=len(o_list)),
        grid=(r // tm,),
        in_specs=in_specs,
        out_specs=[
            pl.BlockSpec((tm, d), lambda i: (i, 0)),
            pl.BlockSpec((tm, d), lambda i: (i, 0)),
            pl.BlockSpec((n_e, tm), lambda i: (0, i)),
        ],
        out_shape=[
            jax.ShapeDtypeStruct((r, d), F32),
            jax.ShapeDtypeStruct((r, d), BF16),
            jax.ShapeDtypeStruct((n_e, r), F32),
        ],
        compiler_params=_params("arbitrary"),
        name="outproj",
    )(*o_list, w_bf, h, g1, gate, g2, shift, scale_m, wr_t)


def _route_kernel(aff_ref, posm_ref, pose_ref, gsel_ref, *, cap):
    n_e, n = aff_ref.shape
    bits = pltpu.bitcast(aff_ref[...], I32)
    capf = float(cap)

    def search(_, carry):
        lo, hi = carry
        mid = lo + ((hi - lo + 1) >> 1)
        cnt = jnp.sum(jnp.where(bits >= mid, 1.0, 0.0), axis=1, keepdims=True)
        ok = cnt >= capf
        return jnp.where(ok, mid, lo), jnp.where(ok, hi, mid - 1)

    lo0 = jnp.zeros((n_e, 1), I32)
    hi0 = jnp.full((n_e, 1), 0x7F800000, I32)
    thr, _ = lax.fori_loop(0, 31, search, (lo0, hi0))
    need = capf - jnp.sum(jnp.where(bits > thr, 1.0, 0.0), axis=1, keepdims=True)
    upper = jnp.where(lax.broadcasted_iota(I32, (LANES, LANES), 0) < lax.broadcasted_iota(I32, (LANES, LANES), 1),
                      1.0, 0.0).astype(BF16)
    run_eq = jnp.zeros((n_e, 1), F32)
    run_sel = jnp.zeros((n_e, 1), F32)
    for j in range(n // LANES):
        sl = slice(j * LANES, (j + 1) * LANES)
        a = aff_ref[:, sl]
        bb = pltpu.bitcast(a, I32)
        eq = jnp.where(bb == thr, 1.0, 0.0)
        rank = _dot(eq.astype(BF16), upper) + run_eq
        run_eq = run_eq + eq.sum(axis=1, keepdims=True)
        sel = (bb > thr) | ((bb == thr) & (rank < need))
        self_f = jnp.where(sel, 1.0, 0.0)
        pos = _dot(self_f.astype(BF16), upper) + run_sel
        run_sel = run_sel + self_f.sum(axis=1, keepdims=True)
        pos_i = pos.astype(I32)
        pose_ref[0, :, sl] = pos_i
        posm_ref[0, :, sl] = jnp.where(sel, pos_i, -1)
        gsel_ref[0, :, sl] = jnp.where(sel, a, 0.0)


def _route(aff, batch, n, cap):
    n_e = aff.shape[0]
    spec = pl.BlockSpec((1, n_e, n), lambda b: (b, 0, 0))
    return pl.pallas_call(
        functools.partial(_route_kernel, cap=cap),
        grid=(batch,),
        in_specs=[pl.BlockSpec((n_e, n), lambda b: (0, b))],
        out_specs=[spec, spec, spec],
        out_shape=[
            jax.ShapeDtypeStruct((batch, n_e, n), I32),
            jax.ShapeDtypeStruct((batch, n_e, n), I32),
            jax.ShapeDtypeStruct((batch, n_e, n), F32),
        ],
        compiler_params=_params("arbitrary"),
        name="route",
    )(aff)


def _gather_kernel(cs_ref, posm_ref, gsel_ref, x_ref, xg_ref, gs_ref, acc_ref, gacc_ref, *, ts, tc, n_chunks,
                   n_e):
    b, e, j = pl.program_id(0), pl.program_id(1), pl.program_id(2)
    s0 = j * ts
    base = (b * n_e + e) * (n_chunks + 1)
    acc_ref[...] = jnp.zeros_like(acc_ref)
    gacc_ref[...] = jnp.zeros_like(gacc_ref)
    slot = s0 + lax.broadcasted_iota(I32, (ts, tc), 0)
    for c in range(n_chunks):
        lo, hi = cs_ref[base + c], cs_ref[base + c + 1]

        @pl.when((lo < s0 + ts) & (hi > s0))
        def _():
            hit = posm_ref[0, 0, c:c + 1, :] == slot
            onehot = jnp.where(hit, 1.0, 0.0).astype(BF16)
            acc_ref[...] += _dot(onehot, x_ref[c * tc:(c + 1) * tc, :])
            gacc_ref[...] += jnp.where(hit, gsel_ref[0, 0, c:c + 1, :], 0.0).sum(axis=1, keepdims=True)

    xg_ref[0] = acc_ref[...].astype(BF16)
    gs_ref[0] = gacc_ref[...]


def _gather(xf, posm, pose, gsel, batch, n, cap):
    d = xf.shape[1]
    n_e = posm.shape[1]
    tc = _tile(n, 256)
    ts = _tile(cap, 256)
    n_chunks = n // tc
    cstart = jnp.concatenate([pose[:, :, ::tc], jnp.full((batch, n_e, 1), cap, I32)], axis=-1).reshape(-1)
    posm4 = posm.reshape(batch, n_e, n_chunks, tc)
    gsel4 = gsel.reshape(batch, n_e, n_chunks, tc)
    nj = cap // ts
    grid_spec = pltpu.PrefetchScalarGridSpec(
        num_scalar_prefetch=1,
        grid=(batch, n_e, nj),
        in_specs=[
            pl.BlockSpec((1, 1, n_chunks, tc), lambda b, e, j, cs: (b, e, 0, 0)),
            pl.BlockSpec((1, 1, n_chunks, tc), lambda b, e, j, cs: (b, e, 0, 0)),
            pl.BlockSpec((n, d), lambda b, e, j, cs: (b, 0)),
        ],
        out_specs=[
            pl.BlockSpec((1, ts, d), lambda b, e, j, cs: (e, b * nj + j, 0)),
            pl.BlockSpec((1, ts, 1), lambda b, e, j, cs: (e, b * nj + j, 0)),
        ],
        scratch_shapes=[pltpu.VMEM((ts, d), F32), pltpu.VMEM((ts, 1), F32)],
    )
    return pl.pallas_call(
        functools.partial(_gather_kernel, ts=ts, tc=tc, n_chunks=n_chunks, n_e=n_e),
        grid_spec=grid_spec,
        out_shape=[
            jax.ShapeDtypeStruct((n_e, batch * cap, d), BF16),
            jax.ShapeDtypeStruct((n_e, batch * cap, 1), F32),
        ],
        compiler_params=_params("arbitrary", "arbitrary", "arbitrary"),
        name="moe_gather",
    )(cstart, posm4, gsel4, xf)


def _ffn_kernel(*refs, with_ctx):
    if with_ctx:
        xl_ref, xc_ref, wg_ref, wu_ref, wd_ref, gl_ref, gc_ref, yl_ref, yc_ref, accl_ref, accc_ref = refs
    else:
        xl_ref, wg_ref, wu_ref, wd_ref, gl_ref, yl_ref, accl_ref = refs
    i, f = pl.program_id(1), pl.program_id(2)
    last_f = pl.num_programs(2) - 1
    tf = wg_ref.shape[2]
    fw = min(tf, FFN_SUB)

    def run(x_ref, gs_ref, y_ref, acc_ref):
        @pl.when(f == 0)
        def _():
            acc_ref[...] = jnp.zeros_like(acc_ref)

        x = x_ref[0]
        hid = []
        for c0 in range(0, tf, fw):
            hg = _dot(x, wg_ref[0, :, c0:c0 + fw].astype(BF16))
            hu = _dot(x, wu_ref[0, :, c0:c0 + fw].astype(BF16))
            hid.append((_silu(hg) * hu).astype(BF16))
        acc_ref[...] += _dot(jnp.concatenate(hid, axis=1), wd_ref[0].astype(BF16))

        @pl.when(f == last_f)
        def _():
            y_ref[0] = (acc_ref[...] * gs_ref[0]).astype(BF16)

    run(xl_ref, gl_ref, yl_ref, accl_ref)
    if with_ctx:
        @pl.when(i == pl.num_programs(1) - 1)
        def _():
            run(xc_ref, gc_ref, yc_ref, accc_ref)


def _expert_ffn(layer, w_gate, w_up, w_down, xg, gslot, xg_c=None, gslot_c=None):
    n_e, m, d = xg.shape
    ff = w_gate.shape[3]
    tm = _tile(m, 1024)
    tf = _tile(ff, 512)
    with_ctx = xg_c is not None
    x_spec = pl.BlockSpec((1, tm, d), lambda e, i, f: (e, i, 0))
    g_spec = pl.BlockSpec((1, tm, 1), lambda e, i, f: (e, i, 0))
    w_specs = [
        pl.BlockSpec((None, 1, d, tf), lambda e, i, f: (layer, e, 0, f)),
        pl.BlockSpec((None, 1, d, tf), lambda e, i, f: (layer, e, 0, f)),
        pl.BlockSpec((None, 1, tf, d), lambda e, i, f: (layer, e, f, 0)),
    ]
    out_specs = [x_spec]
    out_shape = [jax.ShapeDtypeStruct((n_e, m, d), BF16)]
    scratch = [pltpu.VMEM((tm, d), F32)]
    if with_ctx:
        mc = xg_c.shape[1]
        xc_spec = pl.BlockSpec((1, mc, d), lambda e, i, f: (e, 0, 0))
        gc_spec = pl.BlockSpec((1, mc, 1), lambda e, i, f: (e, 0, 0))
        in_specs = [x_spec, xc_spec] + w_specs + [g_spec, gc_spec]
        args = [xg, xg_c, w_gate, w_up, w_down, gslot, gslot_c]
        out_specs.append(xc_spec)
        out_shape.append(jax.ShapeDtypeStruct((n_e, mc, d), BF16))
        scratch.append(pltpu.VMEM((mc, d), F32))
    else:
        in_specs = [x_spec] + w_specs + [g_spec]
        args = [xg, w_gate, w_up, w_down, gslot]
    return pl.pallas_call(
        functools.partial(_ffn_kernel, with_ctx=with_ctx),
        grid=(n_e, m // tm, ff // tf),
        in_specs=in_specs,
        out_specs=out_specs,
        out_shape=out_shape,
        scratch_shapes=scratch,
        compiler_params=_params("arbitrary", "arbitrary", "arbitrary"),
        name="expert_ffn",
    )(*args)


def _combine_kernel(cs_ref, posm_ref, y_ref, h_ref, g_ref, gate_ref, o_ref, *, tt, win, cap, n_e, n_tiles):
    b, i = pl.program_id(0), pl.program_id(1)
    acc = None
    for e in range(n_e):
        if cap <= win:
            a0 = 0
            yw = y_ref[e]
        else:
            s0 = cs_ref[(b * n_e + e) * n_tiles + i]
            a0 = jnp.minimum((s0 // BF16_SUBLANES) * BF16_SUBLANES, cap - win)
            a0 = pl.multiple_of(a0, BF16_SUBLANES)
            yw = y_ref[e, pl.ds(a0, win), :]
        wn = yw.shape[0]
        slot = a0 + lax.broadcasted_iota(I32, (wn, tt), 0)
        onehot = jnp.where(posm_ref[0, e:e + 1, :] == slot, 1.0, 0.0).astype(BF16)
        part = lax.dot_general(onehot, yw, (((0,), (0,)), ((), ())), preferred_element_type=F32)
        acc = part if acc is None else acc + part
    o_ref[...] = h_ref[...] + gate_ref[...] * _rms(acc, g_ref[...])


def _combine(y, posm, pose, h, g, gate, row_fn, batch, n, cap):
    n_e = y.shape[0]
    d = y.shape[2]
    tt = LANES
    win = tt + BF16_SUBLANES
    n_tiles = n // tt
    cstart = pose[:, :, ::tt].reshape(-1)
    grid_spec = pltpu.PrefetchScalarGridSpec(
        num_scalar_prefetch=1,
        grid=(batch, n_tiles),
        in_specs=[
            pl.BlockSpec((1, n_e, tt), lambda b, i, cs: (b, 0, i)),
            pl.BlockSpec((n_e, cap, d), lambda b, i, cs: (0, b, 0), pipeline_mode=pl.Buffered(1)),
            pl.BlockSpec((tt, d), lambda b, i, cs: (b * n_tiles + i, 0)),
            pl.BlockSpec((1, d), lambda b, i, cs: (0, 0)),
            pl.BlockSpec((None, 1, d), lambda b, i, cs: (row_fn(b * n), 0, 0)),
        ],
        out_specs=pl.BlockSpec((tt, d), lambda b, i, cs: (b * n_tiles + i, 0)),
    )
    return pl.pallas_call(
        functools.partial(_combine_kernel, tt=tt, win=win, cap=cap, n_e=n_e, n_tiles=n_tiles),
        grid_spec=grid_spec,
        out_shape=jax.ShapeDtypeStruct(h.shape, F32),
        compiler_params=_params("arbitrary", "arbitrary"),
        name="moe_combine",
    )(cstart, posm, y, h, g, gate)


def _moe_dispatch(xf, aff, batch, n):
    cap = CAPACITY_FACTOR * n // N_EXPERTS
    posm, pose, gsel = _route(aff, batch, n, cap)
    xg, gslot = _gather(xf, posm, pose, gsel, batch, n, cap)
    return xg, gslot, (posm, pose, cap)


def _rope_angles(seq, rot_dim):
    n_rows = seq // GRID_W
    row = jnp.repeat(jnp.arange(n_rows), GRID_W)
    col = jnp.tile(jnp.arange(GRID_W), n_rows)
    n_freq = rot_dim // 4
    inv = ROPE_BASE ** (-jnp.arange(n_freq, dtype=F32) / n_freq)
    ang = jnp.concatenate([row[:, None] * inv, col[:, None] * inv], axis=-1)
    return jnp.cos(ang), jnp.sin(ang)


def kernel(x, c, ctx, c_ctx, mod_w, mod_b, norm_g, ev_w_in, ev_w_out, ev_sink, ev_qk_norm, od_w_in, od_q_norm,
           od_kv_norm, od_w_uq, od_w_ukv, od_w_out, router_w, exp_w_gate, exp_w_up, exp_w_down):
    batch, seq, d = x.shape
    n_ctx = ctx.shape[1]
    depth = mod_w.shape[0]
    assert batch < MOD_ROWS and seq % LANES == 0 and n_ctx % LANES == 0 and seq >= 4 * WINDOW

    cs = jnp.concatenate([c, c_ctx[None, :], jnp.zeros((MOD_ROWS - batch - 1, d), F32)], axis=0)
    mods = _modulation(cs, mod_w, mod_b).reshape(depth, MOD_ROWS, N_MOD, 1, d)

    def lat_row(r):
        return r // seq

    def ctx_row(r):
        return batch

    cos_h, sin_h = _rope_angles(seq, HEAD_DIM)
    rope_even = (jnp.concatenate([cos_h, cos_h], axis=-1), jnp.concatenate([-sin_h, sin_h], axis=-1))
    cos_m, sin_m = _rope_angles(seq, MLA_ROPE)
    half = MLA_ROPE // 2
    zeros = lambda w: jnp.zeros((seq, w), F32)
    rope_mla = (
        jnp.concatenate([cos_m, cos_m, zeros(LANES - MLA_ROPE)], axis=-1),
        jnp.concatenate([-sin_m, zeros(LANES - half)], axis=-1),
        jnp.concatenate([zeros(half), sin_m, zeros(LANES - MLA_ROPE)], axis=-1),
    )

    h_lat = x.reshape(batch * seq, d)
    h_ctx = ctx.reshape(batch * n_ctx, d)
    for layer in range(depth):
        with_ctx = layer < depth - 1
        i = layer // 2
        g = norm_g[layer].reshape(4, 1, d)
        m = [mods[layer, :, k] for k in range(N_MOD)]
        wr_t = router_w[layer].T
        if layer % 2 == 0:
            w_in = ev_w_in[i].astype(BF16)
            w_out = ev_w_out[i].astype(BF16)
            qkv_l = _even_inproj(h_lat, g[0], m[0], m[1], w_in, ev_qk_norm[i], rope_even, lat_row, seq)
            qkv_c = _even_inproj(h_ctx, g[0], m[0], m[1], w_in, ev_qk_norm[i], None, ctx_row, n_ctx)
            ga, gb = A_HEADS // A_KV_HEADS, B_HEADS // B_KV_HEADS
            hd = HEAD_DIM
            a_kw = dict(batch=batch, n_heads=A_HEADS, group=ga, dq=hd, dv=hd)
            b_kw = dict(batch=batch, n_heads=B_HEADS, group=gb, dq=hd, dv=hd, band=False)
            a_ctx = (qkv_c, AK0 * hd, qkv_c, AV0 * hd, n_ctx)
            b_ctx = (qkv_c, BK0 * hd, qkv_c, BV0 * hd, n_ctx)
            o_a = _attention(qkv_l, AQ0 * hd, [a_ctx, (qkv_l, AK0 * hd, qkv_l, AV0 * hd, seq)], ev_sink[i],
                             seq=seq, band=True, tq_cap=256, hp=4, **a_kw)
            o_b = _attention(qkv_l, BQ0 * hd, [b_ctx, (qkv_l, BK0 * hd, qkv_l, BV0 * hd, seq)], None,
                             seq=seq, tq_cap=512, hp=2, **b_kw)
            o_lat = [o_a, o_b]
            if with_ctx:
                o_ac = _attention(qkv_c, AQ0 * hd, [a_ctx], ev_sink[i], seq=n_ctx, band=False, tq_cap=512,
                                  hp=4, **a_kw)
                o_bc = _attention(qkv_c, BQ0 * hd, [b_ctx], None, seq=n_ctx, tq_cap=512, hp=4, **b_kw)
                o_ctx = [o_ac, o_bc]
        else:
            w_in = od_w_in[i]
            w_qkv = w_in[:, :MLA_Q_RANK + MLA_KV_RANK].astype(BF16)
            w_kr = jnp.pad(w_in[:, MLA_Q_RANK + MLA_KV_RANK:], ((0, 0), (0, LANES - MLA_ROPE))).astype(BF16)
            w_uq = od_w_uq[i].reshape(MLA_Q_RANK, MLA_HEADS, MLA_NOPE + MLA_ROPE)
            w_uq = jnp.pad(w_uq, ((0, 0), (0, 0), (0, MLA_QK_PAD - MLA_NOPE - MLA_ROPE)))
            w_uq = w_uq.reshape(MLA_Q_RANK, MLA_HEADS * MLA_QK_PAD).astype(BF16)
            w_ukv = od_w_ukv[i].astype(BF16)
            w_out = od_w_out[i].astype(BF16)
            qn, kvn = od_q_norm[i].reshape(1, -1), od_kv_norm[i].reshape(1, -1)
            q_l, k_l, v_l = _mla_proj(h_lat, g[0], m[0], m[1], w_qkv, w_kr, qn, kvn, w_uq, w_ukv, rope_mla,
                                      lat_row, seq, True)
            proj_c = _mla_proj(h_ctx, g[0], m[0], m[1], w_qkv, w_kr, qn, kvn, w_uq, w_ukv, None, ctx_row,
                               n_ctx, with_ctx)
            k_c, v_c = proj_c[-2], proj_c[-1]
            m_kw = dict(batch=batch, n_heads=MLA_HEADS, group=1, dq=MLA_QK_PAD, dv=MLA_V, band=False, tq_cap=512,
                        hp=2)
            o_lat = [_attention(q_l, 0, [(k_c, 0, v_c, 0, n_ctx), (k_l, 0, v_l, 0, seq)], None, seq=seq, **m_kw)]
            if with_ctx:
                o_ctx = [_attention(proj_c[0], 0, [(k_c, 0, v_c, 0, n_ctx)], None, seq=n_ctx, **m_kw)]
        ew = (layer, exp_w_gate, exp_w_up, exp_w_down)
        h_lat, xf, aff = _outproj(o_lat, w_out, h_lat, g[1], m[2], g[2], m[3], m[4], wr_t, lat_row, seq)
        xg, gslot, (posm, pose, cap) = _moe_dispatch(xf, aff, batch, seq)
        if with_ctx:
            h_ctx, xf_c, aff_c = _outproj(o_ctx, w_out, h_ctx, g[1], m[2], g[2], m[3], m[4], wr_t, ctx_row,
                                          n_ctx)
            xg_c, gslot_c, (posm_c, pose_c, cap_c) = _moe_dispatch(xf_c, aff_c, batch, n_ctx)
            y, y_c = _expert_ffn(*ew, xg, gslot, xg_c, gslot_c)
            h_ctx = _combine(y_c, posm_c, pose_c, h_ctx, g[3], m[5], ctx_row, batch, n_ctx, cap_c)
        else:
            (y,) = _expert_ffn(*ew, xg, gslot)
        h_lat = _combine(y, posm, pose, h_lat, g[3], m[5], lat_row, batch, seq, cap)
    return h_lat.reshape(batch, seq, d)
```

```python
import functools

import jax
import jax.numpy as jnp
from jax import lax
from jax.experimental import pallas as pl
from jax.experimental.pallas import tpu as pltpu

F32 = jnp.float32
BF16 = jnp.bfloat16
I32 = jnp.int32

EPS = 1e-6
GRID_W = 64
WINDOW = 128
ROPE_BASE = 10000.0
HEAD_DIM = 128
A_HEADS = 8
A_KV_HEADS = 2
B_HEADS = 8
B_KV_HEADS = 2
MLA_HEADS = 16
MLA_Q_RANK = 512
MLA_KV_RANK = 512
MLA_NOPE = 128
MLA_ROPE = 64
MLA_V = 128
MLA_QK_PAD = 256
N_EXPERTS = 16
CAPACITY_FACTOR = 2
N_MOD = 6
MOD_ROWS = 8

LANES = 128
BF16_SUBLANES = 16
F32_SUBLANES = 8
VMEM_LIMIT_BYTES = 60 * 1024 * 1024
FFN_SUB = 256
OUTPROJ_SUB = 256

LOG2E = 1.4426950408889634

EVEN_IN_COLS = (A_HEADS + 2 * A_KV_HEADS + B_HEADS + 2 * B_KV_HEADS) * HEAD_DIM
AQ0 = 0
BQ0 = AQ0 + A_HEADS
AK0 = BQ0 + B_HEADS
BK0 = AK0 + A_KV_HEADS
AV0 = BK0 + B_KV_HEADS
BV0 = AV0 + 2 * A_KV_HEADS
EVEN_SLOTS = BV0 + 2 * B_KV_HEADS
EVEN_COLS = EVEN_SLOTS * HEAD_DIM
_W_AK0 = A_HEADS
_W_AV0 = _W_AK0 + A_KV_HEADS
_W_BQ0 = _W_AV0 + A_KV_HEADS
_W_BK0 = _W_BQ0 + B_HEADS
_W_BV0 = _W_BK0 + B_KV_HEADS


def _even_head(wh):
    if wh < _W_AK0:
        return AQ0 + wh, "aq"
    if wh < _W_AV0:
        return AK0 + wh - _W_AK0, "ak"
    if wh < _W_BQ0:
        return AV0 + 2 * (wh - _W_AV0), "av"
    if wh < _W_BK0:
        return BQ0 + wh - _W_BQ0, "bq"
    if wh < _W_BV0:
        return BK0 + wh - _W_BK0, "bk"
    return BV0 + 2 * (wh - _W_BV0), "bv"


def _params(*sem):
    return pltpu.CompilerParams(dimension_semantics=sem, vmem_limit_bytes=VMEM_LIMIT_BYTES)


def _tile(n, cap):
    t = min(n, cap)
    while n % t:
        t -= 1
    return t


def _const_spec(shape):
    nd = len(shape)
    return pl.BlockSpec(shape, lambda *_: (0,) * nd)


def _rms(x, g):
    return x * lax.rsqrt(jnp.mean(x * x, axis=-1, keepdims=True) + EPS) * g


def _silu(x):
    return x / (1.0 + jnp.exp(-x))


def _dot(a, b):
    return jnp.dot(a, b, preferred_element_type=F32)


def _dot_nt(a, b):
    return lax.dot_general(a, b, (((1,), (1,)), ((), ())), preferred_element_type=F32)


def _mod_kernel(cs_ref, w_ref, b_ref, o_ref):
    s = _silu(cs_ref[...])
    o_ref[0] = jnp.dot(s, w_ref[0], precision=lax.Precision.HIGHEST, preferred_element_type=F32) + b_ref[0]


def _modulation(cs, mod_w, mod_b):
    depth, d, n6 = mod_w.shape
    tn = _tile(n6, 1024)
    return pl.pallas_call(
        _mod_kernel,
        grid=(depth, n6 // tn),
        in_specs=[
            _const_spec((MOD_ROWS, d)),
            pl.BlockSpec((1, d, tn), lambda l, j: (l, 0, j)),
            pl.BlockSpec((1, 1, tn), lambda l, j: (l, 0, j)),
        ],
        out_specs=pl.BlockSpec((1, MOD_ROWS, tn), lambda l, j: (l, 0, j)),
        out_shape=jax.ShapeDtypeStruct((depth, MOD_ROWS, n6), F32),
        compiler_params=_params("arbitrary", "arbitrary"),
        name="modulation",
    )(cs, mod_w, mod_b.reshape(depth, 1, n6))


def _mod_spec(d, row_fn):
    return pl.BlockSpec((None, 1, d), lambda i: (row_fn(i), 0, 0))


def _even_inproj_kernel(*refs, rope, scale):
    if rope:
        x_ref, g_ref, sh_ref, sc_ref, w_ref, qkg_ref, cos_ref, sin_ref, o_ref = refs
        cos, sin = cos_ref[...], sin_ref[...]
    else:
        x_ref, g_ref, sh_ref, sc_ref, w_ref, qkg_ref, o_ref = refs
    a = _rms(x_ref[...], g_ref[...]) * (1.0 + sc_ref[...]) + sh_ref[...]
    ab = a.astype(BF16)
    ones = jnp.ones((x_ref.shape[0], HEAD_DIM), BF16)
    for j in range(EVEN_IN_COLS // (2 * HEAD_DIM)):
        acc = _dot(ab, w_ref[:, j * 2 * HEAD_DIM:(j + 1) * 2 * HEAD_DIM])
        for hh in range(2):
            slot, kind = _even_head(2 * j + hh)
            v = acc[:, hh * HEAD_DIM:(hh + 1) * HEAD_DIM]
            if kind == "bq":
                v = _rms(v, qkg_ref[0:1, :])
            elif kind == "bk":
                v = _rms(v, qkg_ref[1:2, :])
            if rope and kind[1] != "v":
                v = v * cos + pltpu.roll(v, HEAD_DIM // 2, 1) * sin
            if kind[1] == "q":
                v = v * scale
            o_ref[:, slot * HEAD_DIM:(slot + 1) * HEAD_DIM] = v.astype(BF16)
            if kind[1] == "v":
                o_ref[:, (slot + 1) * HEAD_DIM:(slot + 2) * HEAD_DIM] = ones


def _even_inproj(h, g, shift, scale_m, w_bf, qk_gain, rope_tabs, row_fn, seq):
    r, d = h.shape
    tm = _tile(seq, 512)
    rope = rope_tabs is not None
    in_specs = [
        pl.BlockSpec((tm, d), lambda i: (i, 0)),
        _const_spec((1, d)),
        _mod_spec(d, lambda i: row_fn(i * tm)),
        _mod_spec(d, lambda i: row_fn(i * tm)),
        _const_spec((d, EVEN_IN_COLS)),
        _const_spec((2, HEAD_DIM)),
    ]
    args = [h, g, shift, scale_m, w_bf, qk_gain]
    if rope:
        nt = seq // tm
        in_specs += [pl.BlockSpec((tm, HEAD_DIM), lambda i: (i % nt, 0))] * 2
        args += list(rope_tabs)
    return pl.pallas_call(
        functools.partial(_even_inproj_kernel, rope=rope, scale=HEAD_DIM ** -0.5 * LOG2E),
        grid=(r // tm,),
        in_specs=in_specs,
        out_specs=pl.BlockSpec((tm, EVEN_COLS), lambda i: (i, 0)),
        out_shape=jax.ShapeDtypeStruct((r, EVEN_COLS), BF16),
        compiler_params=_params("arbitrary"),
        name="even_inproj",
    )(*args)


def _rope_pad(v, c, s1, s2):
    return v * c + pltpu.roll(v, LANES - MLA_ROPE // 2, 1) * s1 + pltpu.roll(v, MLA_ROPE // 2, 1) * s2


def _mla_proj_kernel(*refs, rope, want_q, scale):
    refs = list(refs)
    x_ref, g_ref, sh_ref, sc_ref, win_ref, wkr_ref, qn_ref, kvn_ref = refs[:8]
    refs = refs[8:]
    if want_q:
        wuq_ref = refs.pop(0)
    wukv_ref = refs.pop(0)
    if rope:
        c, s1, s2 = refs[0][...], refs[1][...], refs[2][...]
        refs = refs[3:]
    if want_q:
        q_ref = refs.pop(0)
    k_ref, v_ref = refs
    a = _rms(x_ref[...], g_ref[...]) * (1.0 + sc_ref[...]) + sh_ref[...]
    ab = a.astype(BF16)
    low = _dot(ab, win_ref[...])
    kr = _dot(ab, wkr_ref[...])
    if rope:
        kr = _rope_pad(kr, c, s1, s2)
    krb = kr.astype(BF16)
    if want_q:
        cq = _rms(low[:, :MLA_Q_RANK], qn_ref[...]).astype(BF16)
        for h in range(MLA_HEADS):
            acc = _dot(cq, wuq_ref[:, h * MLA_QK_PAD:(h + 1) * MLA_QK_PAD])
            qr = acc[:, MLA_NOPE:]
            if rope:
                qr = _rope_pad(qr, c, s1, s2)
            q_ref[:, h * MLA_QK_PAD:h * MLA_QK_PAD + MLA_NOPE] = (acc[:, :MLA_NOPE] * scale).astype(BF16)
            q_ref[:, h * MLA_QK_PAD + MLA_NOPE:(h + 1) * MLA_QK_PAD] = (qr * scale).astype(BF16)
    ckv = _rms(low[:, MLA_Q_RANK:], kvn_ref[...]).astype(BF16)
    hw = MLA_NOPE + MLA_V
    for h in range(MLA_HEADS):
        acc = _dot(ckv, wukv_ref[:, h * hw:(h + 1) * hw])
        k_ref[:, h * MLA_QK_PAD:h * MLA_QK_PAD + MLA_NOPE] = acc[:, :MLA_NOPE].astype(BF16)
        k_ref[:, h * MLA_QK_PAD + MLA_NOPE:(h + 1) * MLA_QK_PAD] = krb
        v_ref[:, 2 * h * MLA_V:(2 * h + 1) * MLA_V] = acc[:, MLA_NOPE:].astype(BF16)
        v_ref[:, (2 * h + 1) * MLA_V:(2 * h + 2) * MLA_V] = jnp.ones((x_ref.shape[0], MLA_V), BF16)


def _mla_proj(h, g, shift, scale_m, w_in_bf, w_kr_bf, q_norm, kv_norm, w_uq_bf, w_ukv_bf, rope_tabs,
              row_fn, seq, want_q):
    r, d = h.shape
    tm = _tile(seq, 512)
    rope = rope_tabs is not None
    in_specs = [
        pl.BlockSpec((tm, d), lambda i: (i, 0)),
        _const_spec((1, d)),
        _mod_spec(d, lambda i: row_fn(i * tm)),
        _mod_spec(d, lambda i: row_fn(i * tm)),
        _const_spec(w_in_bf.shape),
        _const_spec(w_kr_bf.shape),
        _const_spec((1, MLA_Q_RANK)),
        _const_spec((1, MLA_KV_RANK)),
    ]
    args = [h, g, shift, scale_m, w_in_bf, w_kr_bf, q_norm, kv_norm]
    if want_q:
        in_specs.append(_const_spec(w_uq_bf.shape))
        args.append(w_uq_bf)
    in_specs.append(_const_spec(w_ukv_bf.shape))
    args.append(w_ukv_bf)
    if rope:
        nt = seq // tm
        in_specs += [pl.BlockSpec((tm, LANES), lambda i: (i % nt, 0))] * 3
        args += list(rope_tabs)
    kcols = MLA_HEADS * MLA_QK_PAD
    vcols = MLA_HEADS * 2 * MLA_V
    out_specs = [pl.BlockSpec((tm, kcols), lambda i: (i, 0)), pl.BlockSpec((tm, vcols), lambda i: (i, 0))]
    out_shape = [jax.ShapeDtypeStruct((r, kcols), BF16), jax.ShapeDtypeStruct((r, vcols), BF16)]
    if want_q:
        out_specs.insert(0, pl.BlockSpec((tm, kcols), lambda i: (i, 0)))
        out_shape.insert(0, jax.ShapeDtypeStruct((r, kcols), BF16))
    return pl.pallas_call(
        functools.partial(_mla_proj_kernel, rope=rope, want_q=want_q,
                          scale=(MLA_NOPE + MLA_ROPE) ** -0.5 * LOG2E),
        grid=(r // tm,),
        in_specs=in_specs,
        out_specs=out_specs,
        out_shape=out_shape,
        compiler_params=_params("arbitrary"),
        name="mla_proj",
    )(*args)


def _attn_kernel(*refs, n_seg, band, use_sink, tq, seq, hp, group, dq, dv, tk):
    refs = list(refs)
    if use_sink:
        sink_ref = refs.pop(0)
    q_ref = refs.pop(0)
    o_ref = refs.pop()
    dvx = 2 * dv
    if band:
        wk = tq + 2 * WINDOW
        q0 = pl.program_id(2) * tq
        start = pl.multiple_of(jnp.clip(q0 - WINDOW, 0, seq - wk), LANES)
        dist = (lax.broadcasted_iota(I32, (tq, wk), 1) - lax.broadcasted_iota(I32, (tq, wk), 0)) + (start - q0)
        valid = jnp.abs(dist) <= WINDOW
    for j in range(hp):
        kv = j // group
        q = q_ref[:, j * dq:(j + 1) * dq]
        m = None
        acc = None
        for s_i in range(n_seg):
            k_ref, v_ref = refs[2 * s_i], refs[2 * s_i + 1]
            slen = k_ref.shape[0]
            masked = band and s_i == n_seg - 1
            chunks = [(start, wk)] if masked else [(c0, min(tk, slen - c0)) for c0 in range(0, slen, tk)]
            for c0, cl in chunks:
                s = _dot_nt(q, k_ref[pl.ds(c0, cl), kv * dq:(kv + 1) * dq])
                if masked:
                    s = jnp.where(valid, s, -jnp.inf)
                v = v_ref[pl.ds(c0, cl), kv * dvx:(kv + 1) * dvx]
                mc = s.max(axis=-1, keepdims=True)
                if m is None:
                    m = mc
                    acc = _dot(jnp.exp2(s - m).astype(BF16), v)
                else:
                    m_new = jnp.maximum(m, mc)
                    acc = jnp.exp2(m - m_new) * acc + _dot(jnp.exp2(s - m_new).astype(BF16), v)
                    m = m_new
        den = acc[:, dv:]
        if use_sink:
            sink = sink_ref[pl.program_id(1) * hp + j] * LOG2E
            den = den + jnp.exp2(sink - m)
        o_ref[:, j * dv:(j + 1) * dv] = (acc[:, :dv] / den).astype(o_ref.dtype)


def _attention(q_arr, q_col0, segs, sink, *, batch, seq, n_heads, group, dq, dv, band, tq_cap, hp, tk=512):
    tq = _tile(seq, tq_cap)
    nq = seq // tq
    use_sink = sink is not None
    if hp <= group:
        assert group % hp == 0
        kvp, kv_of, step_group = 1, (lambda hg: hg // (group // hp)), hp
    else:
        assert hp % group == 0
        kvp, kv_of, step_group = hp // group, (lambda hg: hg), group
    qw, kw, vw = hp * dq, kvp * dq, kvp * 2 * dv
    assert q_col0 % qw == 0
    in_specs, args = [], []
    if use_sink:
        in_specs.append(pl.BlockSpec(memory_space=pltpu.SMEM))
        args.append(sink)
    in_specs.append(pl.BlockSpec((tq, qw), lambda b, hg, i: (b * nq + i, q_col0 // qw + hg)))
    args.append(q_arr)
    for k_arr, k_col0, v_arr, v_col0, slen in segs:
        assert k_col0 % kw == 0 and v_col0 % vw == 0
        in_specs.append(pl.BlockSpec((slen, kw), lambda b, hg, i, c=k_col0 // kw: (b, c + kv_of(hg))))
        in_specs.append(pl.BlockSpec((slen, vw), lambda b, hg, i, c=v_col0 // vw: (b, c + kv_of(hg))))
        args += [k_arr, v_arr]
    return pl.pallas_call(
        functools.partial(_attn_kernel, n_seg=len(segs), band=band, use_sink=use_sink, tq=tq, seq=seq, hp=hp,
                          group=step_group, dq=dq, dv=dv, tk=tk),
        grid=(batch, n_heads // hp, nq),
        in_specs=in_specs,
        out_specs=pl.BlockSpec((tq, hp * dv), lambda b, hg, i: (b * nq + i, hg)),
        out_shape=jax.ShapeDtypeStruct((batch * seq, n_heads * dv), BF16),
        compiler_params=_params("arbitrary", "arbitrary", "arbitrary"),
        name="attention",
    )(*args)


def _outproj_kernel(*refs, n_in):
    o_refs = refs[:n_in]
    (w_ref, h_ref, g1_ref, gate_ref, g2_ref, sh_ref, sc_ref, wr_ref, hn_ref, xf_ref, aff_ref) = refs[n_in:]
    tm = h_ref.shape[0]
    sub = min(tm, OUTPROJ_SUB)
    for r in range(0, tm, sub):
        rows = slice(r, r + sub)
        y = None
        k0 = 0
        for o_ref in o_refs:
            kk = o_ref.shape[1]
            part = _dot(o_ref[rows, :], w_ref[k0:k0 + kk, :])
            y = part if y is None else y + part
            k0 += kk
        hn = h_ref[rows, :] + gate_ref[...] * _rms(y, g1_ref[...])
        hn_ref[rows, :] = hn
        xf = _rms(hn, g2_ref[...]) * (1.0 + sc_ref[...]) + sh_ref[...]
        xf_ref[rows, :] = xf.astype(BF16)
        logits = lax.dot_general(wr_ref[...], xf, (((1,), (1,)), ((), ())),
                                 precision=lax.Precision.HIGHEST, preferred_element_type=F32)
        e = jnp.exp(logits - logits.max(axis=0, keepdims=True))
        aff_ref[:, rows] = e / e.sum(axis=0, keepdims=True)


def _outproj(o_list, w_bf, h, g1, gate, g2, shift, scale_m, wr_t, row_fn, seq):
    r, d = h.shape
    tm = _tile(seq, 512)
    n_e = wr_t.shape[0]
    in_specs = [pl.BlockSpec((tm, o.shape[1]), lambda i: (i, 0)) for o in o_list]
    in_specs += [
        _const_spec(w_bf.shape),
        pl.BlockSpec((tm, d), lambda i: (i, 0)),
        _const_spec((1, d)),
        _mod_spec(d, lambda i: row_fn(i * tm)),
        _const_spec((1, d)),
        _mod_spec(d, lambda i: row_fn(i * tm)),
        _mod_spec(d, lambda i: row_fn(i * tm)),
        _const_spec((n_e, d)),
    ]
    return pl.pallas_call(
        functools.partial(_outproj_kernel, n_in=len(o_list)),
        grid=(r // tm,),
        in_specs=in_specs,
        out_specs=[
            pl.BlockSpec((tm, d), lambda i: (i, 0)),
            pl.BlockSpec((tm, d), lambda i: (i, 0)),
            pl.BlockSpec((n_e, tm), lambda i: (0, i)),
        ],
        out_shape=[
            jax.ShapeDtypeStruct((r, d), F32),
            jax.ShapeDtypeStruct((r, d), BF16),
            jax.ShapeDtypeStruct((n_e, r), F32),
        ],
        compiler_params=_params("arbitrary"),
        name="outproj",
    )(*o_list, w_bf, h, g1, gate, g2, shift, scale_m, wr_t)


def _route_kernel(aff_ref, posm_ref, pose_ref, gsel_ref, *, cap):
    n_e, n = aff_ref.shape
    bits = pltpu.bitcast(aff_ref[...], I32)
    capf = float(cap)

    def search(_, carry):
        lo, hi = carry
        mid = lo + ((hi - lo + 1) >> 1)
        cnt = jnp.sum(jnp.where(bits >= mid, 1.0, 0.0), axis=1, keepdims=True)
        ok = cnt >= capf
        return jnp.where(ok, mid, lo), jnp.where(ok, hi, mid - 1)

    lo0 = jnp.zeros((n_e, 1), I32)
    hi0 = jnp.full((n_e, 1), 0x7F800000, I32)
    thr, _ = lax.fori_loop(0, 31, search, (lo0, hi0))
    need = capf - jnp.sum(jnp.where(bits > thr, 1.0, 0.0), axis=1, keepdims=True)
    upper = jnp.where(lax.broadcasted_iota(I32, (LANES, LANES), 0) < lax.broadcasted_iota(I32, (LANES, LANES), 1),
                      1.0, 0.0).astype(BF16)
    run_eq = jnp.zeros((n_e, 1), F32)
    run_sel = jnp.zeros((n_e, 1), F32)
    for j in range(n // LANES):
        sl = slice(j * LANES, (j + 1) * LANES)
        a = aff_ref[:, sl]
        bb = pltpu.bitcast(a, I32)
        eq = jnp.where(bb == thr, 1.0, 0.0)
        rank = _dot(eq.astype(BF16), upper) + run_eq
        run_eq = run_eq + eq.sum(axis=1, keepdims=True)
        sel = (bb > thr) | ((bb == thr) & (rank < need))
        self_f = jnp.where(sel, 1.0, 0.0)
        pos = _dot(self_f.astype(BF16), upper) + run_sel
        run_sel = run_sel + self_f.sum(axis=1, keepdims=True)
        pos_i = pos.astype(I32)
        pose_ref[0, :, sl] = pos_i
        posm_ref[0, :, sl] = jnp.where(sel, pos_i, -1)
        gsel_ref[0, :, sl] = jnp.where(sel, a, 0.0)


def _route(aff, batch, n, cap):
    n_e = aff.shape[0]
    spec = pl.BlockSpec((1, n_e, n), lambda b: (b, 0, 0))
    return pl.pallas_call(
        functools.partial(_route_kernel, cap=cap),
        grid=(batch,),
        in_specs=[pl.BlockSpec((n_e, n), lambda b: (0, b))],
        out_specs=[spec, spec, spec],
        out_shape=[
            jax.ShapeDtypeStruct((batch, n_e, n), I32),
            jax.ShapeDtypeStruct((batch, n_e, n), I32),
            jax.ShapeDtypeStruct((batch, n_e, n), F32),
        ],
        compiler_params=_params("arbitrary"),
        name="route",
    )(aff)


def _gather_kernel(cs_ref, posm_ref, gsel_ref, x_ref, xg_ref, gs_ref, acc_ref, gacc_ref, *, win, tc, n_chunks,
                   n_e, cap):
    b, e = pl.program_id(0), pl.program_id(1)
    base = (b * n_e + e) * (n_chunks + 1)
    acc_ref[...] = jnp.zeros_like(acc_ref)
    gacc_ref[...] = jnp.zeros_like(gacc_ref)
    row = lax.broadcasted_iota(I32, (win, tc), 0)
    for c in range(n_chunks):
        lo, hi = cs_ref[base + c], cs_ref[base + c + 1]
        a0 = (lo // F32_SUBLANES) * F32_SUBLANES

        def fill(w, carry, c=c, a0=a0):
            first = a0 + w * win
            start = pl.multiple_of(jnp.minimum(first, cap - win), F32_SUBLANES)
            slot = start + row
            hit = (posm_ref[0, 0, c:c + 1, :] == slot) & (slot >= first)
            onehot = jnp.where(hit, 1.0, 0.0).astype(BF16)
            acc_ref[pl.ds(start, win), :] += _dot(onehot, x_ref[c * tc:(c + 1) * tc, :])
            gacc_ref[pl.ds(start, win), :] += jnp.where(hit, gsel_ref[0, 0, c:c + 1, :], 0.0).sum(
                axis=1, keepdims=True)
            return carry

        lax.fori_loop(0, (hi - a0 + win - 1) // win, fill, 0)

    xg_ref[0] = acc_ref[...].astype(BF16)
    gs_ref[0] = gacc_ref[...]


def _gather(xf, posm, pose, gsel, batch, n, cap):
    d = xf.shape[1]
    n_e = posm.shape[1]
    tc = _tile(n, 256)
    win = min(cap, LANES)
    n_chunks = n // tc
    cstart = jnp.concatenate([pose[:, :, ::tc], jnp.full((batch, n_e, 1), cap, I32)], axis=-1).reshape(-1)
    posm4 = posm.reshape(batch, n_e, n_chunks, tc)
    gsel4 = gsel.reshape(batch, n_e, n_chunks, tc)
    grid_spec = pltpu.PrefetchScalarGridSpec(
        num_scalar_prefetch=1,
        grid=(batch, n_e),
        in_specs=[
            pl.BlockSpec((1, 1, n_chunks, tc), lambda b, e, cs: (b, e, 0, 0)),
            pl.BlockSpec((1, 1, n_chunks, tc), lambda b, e, cs: (b, e, 0, 0)),
            pl.BlockSpec((n, d), lambda b, e, cs: (b, 0)),
        ],
        out_specs=[
            pl.BlockSpec((1, cap, d), lambda b, e, cs: (e, b, 0)),
            pl.BlockSpec((1, cap, 1), lambda b, e, cs: (e, b, 0)),
        ],
        scratch_shapes=[pltpu.VMEM((cap, d), F32), pltpu.VMEM((cap, 1), F32)],
    )
    return pl.pallas_call(
        functools.partial(_gather_kernel, win=win, tc=tc, n_chunks=n_chunks, n_e=n_e, cap=cap),
        grid_spec=grid_spec,
        out_shape=[
            jax.ShapeDtypeStruct((n_e, batch * cap, d), BF16),
            jax.ShapeDtypeStruct((n_e, batch * cap, 1), F32),
        ],
        compiler_params=_params("arbitrary", "arbitrary"),
        name="moe_gather",
    )(cstart, posm4, gsel4, xf)


def _ffn_kernel(*refs, with_ctx):
    if with_ctx:
        xl_ref, xc_ref, wg_ref, wu_ref, wd_ref, gl_ref, gc_ref, yl_ref, yc_ref, accl_ref, accc_ref = refs
    else:
        xl_ref, wg_ref, wu_ref, wd_ref, gl_ref, yl_ref, accl_ref = refs
    i, f = pl.program_id(1), pl.program_id(2)
    last_f = pl.num_programs(2) - 1
    tf = wg_ref.shape[2]
    fw = min(tf, FFN_SUB)

    def run(x_ref, gs_ref, y_ref, acc_ref):
        @pl.when(f == 0)
        def _():
            acc_ref[...] = jnp.zeros_like(acc_ref)

        x = x_ref[0]
        hid = []
        for c0 in range(0, tf, fw):
            hg = _dot(x, wg_ref[0, :, c0:c0 + fw].astype(BF16))
            hu = _dot(x, wu_ref[0, :, c0:c0 + fw].astype(BF16))
            hid.append((_silu(hg) * hu).astype(BF16))
        acc_ref[...] += _dot(jnp.concatenate(hid, axis=1), wd_ref[0].astype(BF16))

        @pl.when(f == last_f)
        def _():
            y_ref[0] = (acc_ref[...] * gs_ref[0]).astype(BF16)

    run(xl_ref, gl_ref, yl_ref, accl_ref)
    if with_ctx:
        @pl.when(i == pl.num_programs(1) - 1)
        def _():
            run(xc_ref, gc_ref, yc_ref, accc_ref)


def _expert_ffn(layer, w_gate, w_up, w_down, xg, gslot, xg_c=None, gslot_c=None):
    n_e, m, d = xg.shape
    ff = w_gate.shape[3]
    tm = _tile(m, 1024)
    tf = _tile(ff, 512)
    with_ctx = xg_c is not None
    x_spec = pl.BlockSpec((1, tm, d), lambda e, i, f: (e, i, 0))
    g_spec = pl.BlockSpec((1, tm, 1), lambda e, i, f: (e, i, 0))
    w_specs = [
        pl.BlockSpec((None, 1, d, tf), lambda e, i, f: (layer, e, 0, f)),
        pl.BlockSpec((None, 1, d, tf), lambda e, i, f: (layer, e, 0, f)),
        pl.BlockSpec((None, 1, tf, d), lambda e, i, f: (layer, e, f, 0)),
    ]
    out_specs = [x_spec]
    out_shape = [jax.ShapeDtypeStruct((n_e, m, d), BF16)]
    scratch = [pltpu.VMEM((tm, d), F32)]
    if with_ctx:
        mc = xg_c.shape[1]
        xc_spec = pl.BlockSpec((1, mc, d), lambda e, i, f: (e, 0, 0))
        gc_spec = pl.BlockSpec((1, mc, 1), lambda e, i, f: (e, 0, 0))
        in_specs = [x_spec, xc_spec] + w_specs + [g_spec, gc_spec]
        args = [xg, xg_c, w_gate, w_up, w_down, gslot, gslot_c]
        out_specs.append(xc_spec)
        out_shape.append(jax.ShapeDtypeStruct((n_e, mc, d), BF16))
        scratch.append(pltpu.VMEM((mc, d), F32))
    else:
        in_specs = [x_spec] + w_specs + [g_spec]
        args = [xg, w_gate, w_up, w_down, gslot]
    return pl.pallas_call(
        functools.partial(_ffn_kernel, with_ctx=with_ctx),
        grid=(n_e, m // tm, ff // tf),
        in_specs=in_specs,
        out_specs=out_specs,
        out_shape=out_shape,
        scratch_shapes=scratch,
        compiler_params=_params("arbitrary", "arbitrary", "arbitrary"),
        name="expert_ffn",
    )(*args)


def _combine_kernel(cs_ref, posm_ref, y_ref, h_ref, g_ref, gate_ref, o_ref, *, tt, win, cap, n_e, n_tiles):
    b, i = pl.program_id(0), pl.program_id(1)
    acc = None
    for e in range(n_e):
        if cap <= win:
            a0 = 0
            yw = y_ref[e]
        else:
            s0 = cs_ref[(b * n_e + e) * n_tiles + i]
            a0 = jnp.minimum((s0 // BF16_SUBLANES) * BF16_SUBLANES, cap - win)
            a0 = pl.multiple_of(a0, BF16_SUBLANES)
            yw = y_ref[e, pl.ds(a0, win), :]
        wn = yw.shape[0]
        slot = a0 + lax.broadcasted_iota(I32, (wn, tt), 0)
        onehot = jnp.where(posm_ref[0, e:e + 1, :] == slot, 1.0, 0.0).astype(BF16)
        part = lax.dot_general(onehot, yw, (((0,), (0,)), ((), ())), preferred_element_type=F32)
        acc = part if acc is None else acc + part
    o_ref[...] = h_ref[...] + gate_ref[...] * _rms(acc, g_ref[...])


def _combine(y, posm, pose, h, g, gate, row_fn, batch, n, cap):
    n_e = y.shape[0]
    d = y.shape[2]
    tt = LANES
    win = tt + BF16_SUBLANES
    n_tiles = n // tt
    cstart = pose[:, :, ::tt].reshape(-1)
    grid_spec = pltpu.PrefetchScalarGridSpec(
        num_scalar_prefetch=1,
        grid=(batch, n_tiles),
        in_specs=[
            pl.BlockSpec((1, n_e, tt), lambda b, i, cs: (b, 0, i)),
            pl.BlockSpec((n_e, cap, d), lambda b, i, cs: (0, b, 0), pipeline_mode=pl.Buffered(1)),
            pl.BlockSpec((tt, d), lambda b, i, cs: (b * n_tiles + i, 0)),
            pl.BlockSpec((1, d), lambda b, i, cs: (0, 0)),
            pl.BlockSpec((None, 1, d), lambda b, i, cs: (row_fn(b * n), 0, 0)),
        ],
        out_specs=pl.BlockSpec((tt, d), lambda b, i, cs: (b * n_tiles + i, 0)),
    )
    return pl.pallas_call(
        functools.partial(_combine_kernel, tt=tt, win=win, cap=cap, n_e=n_e, n_tiles=n_tiles),
        grid_spec=grid_spec,
        out_shape=jax.ShapeDtypeStruct(h.shape, F32),
        compiler_params=_params("arbitrary", "arbitrary"),
        name="moe_combine",
    )(cstart, posm, y, h, g, gate)


def _moe_dispatch(xf, aff, batch, n):
    cap = CAPACITY_FACTOR * n // N_EXPERTS
    posm, pose, gsel = _route(aff, batch, n, cap)
    xg, gslot = _gather(xf, posm, pose, gsel, batch, n, cap)
    return xg, gslot, (posm, pose, cap)


def _rope_angles(seq, rot_dim):
    n_rows = seq // GRID_W
    row = jnp.repeat(jnp.arange(n_rows), GRID_W)
    col = jnp.tile(jnp.arange(GRID_W), n_rows)
    n_freq = rot_dim // 4
    inv = ROPE_BASE ** (-jnp.arange(n_freq, dtype=F32) / n_freq)
    ang = jnp.concatenate([row[:, None] * inv, col[:, None] * inv], axis=-1)
    return jnp.cos(ang), jnp.sin(ang)


def kernel(x, c, ctx, c_ctx, mod_w, mod_b, norm_g, ev_w_in, ev_w_out, ev_sink, ev_qk_norm, od_w_in, od_q_norm,
           od_kv_norm, od_w_uq, od_w_ukv, od_w_out, router_w, exp_w_gate, exp_w_up, exp_w_down):
    batch, seq, d = x.shape
    n_ctx = ctx.shape[1]
    depth = mod_w.shape[0]
    assert batch < MOD_ROWS and seq % LANES == 0 and n_ctx % LANES == 0 and seq >= 4 * WINDOW

    cs = jnp.concatenate([c, c_ctx[None, :], jnp.zeros((MOD_ROWS - batch - 1, d), F32)], axis=0)
    mods = _modulation(cs, mod_w, mod_b).reshape(depth, MOD_ROWS, N_MOD, 1, d)

    def lat_row(r):
        return r // seq

    def ctx_row(r):
        return batch

    cos_h, sin_h = _rope_angles(seq, HEAD_DIM)
    rope_even = (jnp.concatenate([cos_h, cos_h], axis=-1), jnp.concatenate([-sin_h, sin_h], axis=-1))
    cos_m, sin_m = _rope_angles(seq, MLA_ROPE)
    half = MLA_ROPE // 2
    zeros = lambda w: jnp.zeros((seq, w), F32)
    rope_mla = (
        jnp.concatenate([cos_m, cos_m, zeros(LANES - MLA_ROPE)], axis=-1),
        jnp.concatenate([-sin_m, zeros(LANES - half)], axis=-1),
        jnp.concatenate([zeros(half), sin_m, zeros(LANES - MLA_ROPE)], axis=-1),
    )

    h_lat = x.reshape(batch * seq, d)
    h_ctx = ctx.reshape(batch * n_ctx, d)
    for layer in range(depth):
        with_ctx = layer < depth - 1
        i = layer // 2
        g = norm_g[layer].reshape(4, 1, d)
        m = [mods[layer, :, k] for k in range(N_MOD)]
        wr_t = router_w[layer].T
        if layer % 2 == 0:
            w_in = ev_w_in[i].astype(BF16)
            w_out = ev_w_out[i].astype(BF16)
            qkv_l = _even_inproj(h_lat, g[0], m[0], m[1], w_in, ev_qk_norm[i], rope_even, lat_row, seq)
            qkv_c = _even_inproj(h_ctx, g[0], m[0], m[1], w_in, ev_qk_norm[i], None, ctx_row, n_ctx)
            ga, gb = A_HEADS // A_KV_HEADS, B_HEADS // B_KV_HEADS
            hd = HEAD_DIM
            a_kw = dict(batch=batch, n_heads=A_HEADS, group=ga, dq=hd, dv=hd)
            b_kw = dict(batch=batch, n_heads=B_HEADS, group=gb, dq=hd, dv=hd, band=False)
            a_ctx = (qkv_c, AK0 * hd, qkv_c, AV0 * hd, n_ctx)
            b_ctx = (qkv_c, BK0 * hd, qkv_c, BV0 * hd, n_ctx)
            o_a = _attention(qkv_l, AQ0 * hd, [a_ctx, (qkv_l, AK0 * hd, qkv_l, AV0 * hd, seq)], ev_sink[i],
                             seq=seq, band=True, tq_cap=256, hp=4, **a_kw)
            o_b = _attention(qkv_l, BQ0 * hd, [b_ctx, (qkv_l, BK0 * hd, qkv_l, BV0 * hd, seq)], None,
                             seq=seq, tq_cap=512, hp=2, **b_kw)
            o_lat = [o_a, o_b]
            if with_ctx:
                o_ac = _attention(qkv_c, AQ0 * hd, [a_ctx], ev_sink[i], seq=n_ctx, band=False, tq_cap=512,
                                  hp=4, **a_kw)
                o_bc = _attention(qkv_c, BQ0 * hd, [b_ctx], None, seq=n_ctx, tq_cap=512, hp=4, **b_kw)
                o_ctx = [o_ac, o_bc]
        else:
            w_in = od_w_in[i]
            w_qkv = w_in[:, :MLA_Q_RANK + MLA_KV_RANK].astype(BF16)
            w_kr = jnp.pad(w_in[:, MLA_Q_RANK + MLA_KV_RANK:], ((0, 0), (0, LANES - MLA_ROPE))).astype(BF16)
            w_uq = od_w_uq[i].reshape(MLA_Q_RANK, MLA_HEADS, MLA_NOPE + MLA_ROPE)
            w_uq = jnp.pad(w_uq, ((0, 0), (0, 0), (0, MLA_QK_PAD - MLA_NOPE - MLA_ROPE)))
            w_uq = w_uq.reshape(MLA_Q_RANK, MLA_HEADS * MLA_QK_PAD).astype(BF16)
            w_ukv = od_w_ukv[i].astype(BF16)
            w_out = od_w_out[i].astype(BF16)
            qn, kvn = od_q_norm[i].reshape(1, -1), od_kv_norm[i].reshape(1, -1)
            q_l, k_l, v_l = _mla_proj(h_lat, g[0], m[0], m[1], w_qkv, w_kr, qn, kvn, w_uq, w_ukv, rope_mla,
                                      lat_row, seq, True)
            proj_c = _mla_proj(h_ctx, g[0], m[0], m[1], w_qkv, w_kr, qn, kvn, w_uq, w_ukv, None, ctx_row,
                               n_ctx, with_ctx)
            k_c, v_c = proj_c[-2], proj_c[-1]
            m_kw = dict(batch=batch, n_heads=MLA_HEADS, group=1, dq=MLA_QK_PAD, dv=MLA_V, band=False, tq_cap=512,
                        hp=2)
            o_lat = [_attention(q_l, 0, [(k_c, 0, v_c, 0, n_ctx), (k_l, 0, v_l, 0, seq)], None, seq=seq, **m_kw)]
            if with_ctx:
                o_ctx = [_attention(proj_c[0], 0, [(k_c, 0, v_c, 0, n_ctx)], None, seq=n_ctx, **m_kw)]
        ew = (layer, exp_w_gate, exp_w_up, exp_w_down)
        h_lat, xf, aff = _outproj(o_lat, w_out, h_lat, g[1], m[2], g[2], m[3], m[4], wr_t, lat_row, seq)
        xg, gslot, (posm, pose, cap) = _moe_dispatch(xf, aff, batch, seq)
        if with_ctx:
            h_ctx, xf_c, aff_c = _outproj(o_ctx, w_out, h_ctx, g[1], m[2], g[2], m[3], m[4], wr_t, ctx_row,
                                          n_ctx)
            xg_c, gslot_c, (posm_c, pose_c, cap_c) = _moe_dispatch(xf_c, aff_c, batch, n_ctx)
            y, y_c = _expert_ffn(*ew, xg, gslot, xg_c, gslot_c)
            h_ctx = _combine(y_c, posm_c, pose_c, h_ctx, g[3], m[5], ctx_row, batch, n_ctx, cap_c)
        else:
            (y,) = _expert_ffn(*ew, xg, gslot)
        h_lat = _combine(y, posm, pose, h_lat, g[3], m[5], lat_row, batch, seq, cap)
    return h_lat.reshape(batch, seq, d)
```

```python
import functools

import jax
import jax.numpy as jnp
from jax import lax
from jax.experimental import pallas as pl
from jax.experimental.pallas import tpu as pltpu

F32 = jnp.float32
BF16 = jnp.bfloat16
I32 = jnp.int32

EPS = 1e-6
GRID_W = 64
WINDOW = 128
ROPE_BASE = 10000.0
HEAD_DIM = 128
A_HEADS = 8
A_KV_HEADS = 2
B_HEADS = 8
B_KV_HEADS = 2
MLA_HEADS = 16
MLA_Q_RANK = 512
MLA_KV_RANK = 512
MLA_NOPE = 128
MLA_ROPE = 64
MLA_V = 128
MLA_QK_PAD = 256
N_EXPERTS = 16
CAPACITY_FACTOR = 2
N_MOD = 6
MOD_ROWS = 8

LANES = 128
BF16_SUBLANES = 16
F32_SUBLANES = 8
VMEM_LIMIT_BYTES = 60 * 1024 * 1024
FFN_SUB = 256
OUTPROJ_SUB = 512

LOG2E = 1.4426950408889634

EVEN_IN_COLS = (A_HEADS + 2 * A_KV_HEADS + B_HEADS + 2 * B_KV_HEADS) * HEAD_DIM
AQ0 = 0
BQ0 = AQ0 + A_HEADS
AK0 = BQ0 + B_HEADS
BK0 = AK0 + A_KV_HEADS
AV0 = BK0 + B_KV_HEADS
BV0 = AV0 + 2 * A_KV_HEADS
EVEN_SLOTS = BV0 + 2 * B_KV_HEADS
EVEN_COLS = EVEN_SLOTS * HEAD_DIM
_W_AK0 = A_HEADS
_W_AV0 = _W_AK0 + A_KV_HEADS
_W_BQ0 = _W_AV0 + A_KV_HEADS
_W_BK0 = _W_BQ0 + B_HEADS
_W_BV0 = _W_BK0 + B_KV_HEADS


def _even_head(wh):
    if wh < _W_AK0:
        return AQ0 + wh, "aq"
    if wh < _W_AV0:
        return AK0 + wh - _W_AK0, "ak"
    if wh < _W_BQ0:
        return AV0 + 2 * (wh - _W_AV0), "av"
    if wh < _W_BK0:
        return BQ0 + wh - _W_BQ0, "bq"
    if wh < _W_BV0:
        return BK0 + wh - _W_BK0, "bk"
    return BV0 + 2 * (wh - _W_BV0), "bv"


def _params(*sem):
    return pltpu.CompilerParams(dimension_semantics=sem, vmem_limit_bytes=VMEM_LIMIT_BYTES)


def _tile(n, cap):
    t = min(n, cap)
    while n % t:
        t -= 1
    return t


def _const_spec(shape):
    nd = len(shape)
    return pl.BlockSpec(shape, lambda *_: (0,) * nd)


def _rms(x, g):
    return x * lax.rsqrt(jnp.mean(x * x, axis=-1, keepdims=True) + EPS) * g


def _silu(x):
    return x / (1.0 + jnp.exp(-x))


def _dot(a, b):
    return jnp.dot(a, b, preferred_element_type=F32)


def _dot_nt(a, b):
    return lax.dot_general(a, b, (((1,), (1,)), ((), ())), preferred_element_type=F32)


def _split_bf16(x):
    hi = x.astype(BF16)
    return hi, (x - hi.astype(F32)).astype(BF16)


def _mod_kernel(cs_ref, w_ref, b_ref, o_ref):
    s_hi, s_lo = _split_bf16(_silu(cs_ref[...]))
    w_hi, w_lo = _split_bf16(w_ref[0])
    o_ref[0] = _dot(s_hi, w_hi) + (_dot(s_hi, w_lo) + _dot(s_lo, w_hi)) + b_ref[0]


def _modulation(cs, mod_w, mod_b):
    depth, d, n6 = mod_w.shape
    tn = _tile(n6, 1024)
    return pl.pallas_call(
        _mod_kernel,
        grid=(depth, n6 // tn),
        in_specs=[
            _const_spec((MOD_ROWS, d)),
            pl.BlockSpec((1, d, tn), lambda l, j: (l, 0, j)),
            pl.BlockSpec((1, 1, tn), lambda l, j: (l, 0, j)),
        ],
        out_specs=pl.BlockSpec((1, MOD_ROWS, tn), lambda l, j: (l, 0, j)),
        out_shape=jax.ShapeDtypeStruct((depth, MOD_ROWS, n6), F32),
        compiler_params=_params("arbitrary", "arbitrary"),
        name="modulation",
    )(cs, mod_w, mod_b.reshape(depth, 1, n6))


def _mod_spec(d, row_fn):
    return pl.BlockSpec((None, 1, d), lambda i: (row_fn(i), 0, 0))


def _even_inproj_kernel(*refs, rope, scale):
    if rope:
        x_ref, g_ref, sh_ref, sc_ref, w_ref, qkg_ref, cos_ref, sin_ref, o_ref = refs
        cos, sin = cos_ref[...], sin_ref[...]
    else:
        x_ref, g_ref, sh_ref, sc_ref, w_ref, qkg_ref, o_ref = refs
    a = _rms(x_ref[...], g_ref[...]) * (1.0 + sc_ref[...]) + sh_ref[...]
    ab = a.astype(BF16)
    ones = jnp.ones((x_ref.shape[0], HEAD_DIM), BF16)
    for j in range(EVEN_IN_COLS // (2 * HEAD_DIM)):
        acc = _dot(ab, w_ref[:, j * 2 * HEAD_DIM:(j + 1) * 2 * HEAD_DIM])
        for hh in range(2):
            slot, kind = _even_head(2 * j + hh)
            v = acc[:, hh * HEAD_DIM:(hh + 1) * HEAD_DIM]
            if kind == "bq":
                v = _rms(v, qkg_ref[0:1, :])
            elif kind == "bk":
                v = _rms(v, qkg_ref[1:2, :])
            if rope and kind[1] != "v":
                v = v * cos + pltpu.roll(v, HEAD_DIM // 2, 1) * sin
            if kind[1] == "q":
                v = v * scale
            o_ref[:, slot * HEAD_DIM:(slot + 1) * HEAD_DIM] = v.astype(BF16)
            if kind[1] == "v":
                o_ref[:, (slot + 1) * HEAD_DIM:(slot + 2) * HEAD_DIM] = ones


def _even_inproj(h, g, shift, scale_m, w_bf, qk_gain, rope_tabs, row_fn, seq):
    r, d = h.shape
    tm = _tile(seq, 512)
    rope = rope_tabs is not None
    in_specs = [
        pl.BlockSpec((tm, d), lambda i: (i, 0)),
        _const_spec((1, d)),
        _mod_spec(d, lambda i: row_fn(i * tm)),
        _mod_spec(d, lambda i: row_fn(i * tm)),
        _const_spec((d, EVEN_IN_COLS)),
        _const_spec((2, HEAD_DIM)),
    ]
    args = [h, g, shift, scale_m, w_bf, qk_gain]
    if rope:
        nt = seq // tm
        in_specs += [pl.BlockSpec((tm, HEAD_DIM), lambda i: (i % nt, 0))] * 2
        args += list(rope_tabs)
    return pl.pallas_call(
        functools.partial(_even_inproj_kernel, rope=rope, scale=HEAD_DIM ** -0.5 * LOG2E),
        grid=(r // tm,),
        in_specs=in_specs,
        out_specs=pl.BlockSpec((tm, EVEN_COLS), lambda i: (i, 0)),
        out_shape=jax.ShapeDtypeStruct((r, EVEN_COLS), BF16),
        compiler_params=_params("arbitrary"),
        name="even_inproj",
    )(*args)


def _rope_pad(v, c, s1, s2):
    return v * c + pltpu.roll(v, LANES - MLA_ROPE // 2, 1) * s1 + pltpu.roll(v, MLA_ROPE // 2, 1) * s2


def _mla_proj_kernel(*refs, rope, want_q, scale):
    refs = list(refs)
    x_ref, g_ref, sh_ref, sc_ref, win_ref, wkr_ref, qn_ref, kvn_ref = refs[:8]
    refs = refs[8:]
    if want_q:
        wuq_ref = refs.pop(0)
    wukv_ref = refs.pop(0)
    if rope:
        c, s1, s2 = refs[0][...], refs[1][...], refs[2][...]
        refs = refs[3:]
    if want_q:
        q_ref = refs.pop(0)
    k_ref, v_ref = refs
    a = _rms(x_ref[...], g_ref[...]) * (1.0 + sc_ref[...]) + sh_ref[...]
    ab = a.astype(BF16)
    low = _dot(ab, win_ref[...])
    kr = _dot(ab, wkr_ref[...])
    if rope:
        kr = _rope_pad(kr, c, s1, s2)
    krb = kr.astype(BF16)
    if want_q:
        cq = _rms(low[:, :MLA_Q_RANK], qn_ref[...]).astype(BF16)
        for h in range(MLA_HEADS):
            acc = _dot(cq, wuq_ref[:, h * MLA_QK_PAD:(h + 1) * MLA_QK_PAD])
            qr = acc[:, MLA_NOPE:]
            if rope:
                qr = _rope_pad(qr, c, s1, s2)
            q_ref[:, h * MLA_QK_PAD:h * MLA_QK_PAD + MLA_NOPE] = (acc[:, :MLA_NOPE] * scale).astype(BF16)
            q_ref[:, h * MLA_QK_PAD + MLA_NOPE:(h + 1) * MLA_QK_PAD] = (qr * scale).astype(BF16)
    ckv = _rms(low[:, MLA_Q_RANK:], kvn_ref[...]).astype(BF16)
    hw = MLA_NOPE + MLA_V
    for h in range(MLA_HEADS):
        acc = _dot(ckv, wukv_ref[:, h * hw:(h + 1) * hw])
        k_ref[:, h * MLA_QK_PAD:h * MLA_QK_PAD + MLA_NOPE] = acc[:, :MLA_NOPE].astype(BF16)
        k_ref[:, h * MLA_QK_PAD + MLA_NOPE:(h + 1) * MLA_QK_PAD] = krb
        v_ref[:, 2 * h * MLA_V:(2 * h + 1) * MLA_V] = acc[:, MLA_NOPE:].astype(BF16)
        v_ref[:, (2 * h + 1) * MLA_V:(2 * h + 2) * MLA_V] = jnp.ones((x_ref.shape[0], MLA_V), BF16)


def _mla_proj(h, g, shift, scale_m, w_in_bf, w_kr_bf, q_norm, kv_norm, w_uq_bf, w_ukv_bf, rope_tabs,
              row_fn, seq, want_q):
    r, d = h.shape
    tm = _tile(seq, 512)
    rope = rope_tabs is not None
    in_specs = [
        pl.BlockSpec((tm, d), lambda i: (i, 0)),
        _const_spec((1, d)),
        _mod_spec(d, lambda i: row_fn(i * tm)),
        _mod_spec(d, lambda i: row_fn(i * tm)),
        _const_spec(w_in_bf.shape),
        _const_spec(w_kr_bf.shape),
        _const_spec((1, MLA_Q_RANK)),
        _const_spec((1, MLA_KV_RANK)),
    ]
    args = [h, g, shift, scale_m, w_in_bf, w_kr_bf, q_norm, kv_norm]
    if want_q:
        in_specs.append(_const_spec(w_uq_bf.shape))
        args.append(w_uq_bf)
    in_specs.append(_const_spec(w_ukv_bf.shape))
    args.append(w_ukv_bf)
    if rope:
        nt = seq // tm
        in_specs += [pl.BlockSpec((tm, LANES), lambda i: (i % nt, 0))] * 3
        args += list(rope_tabs)
    kcols = MLA_HEADS * MLA_QK_PAD
    vcols = MLA_HEADS * 2 * MLA_V
    out_specs = [pl.BlockSpec((tm, kcols), lambda i: (i, 0)), pl.BlockSpec((tm, vcols), lambda i: (i, 0))]
    out_shape = [jax.ShapeDtypeStruct((r, kcols), BF16), jax.ShapeDtypeStruct((r, vcols), BF16)]
    if want_q:
        out_specs.insert(0, pl.BlockSpec((tm, kcols), lambda i: (i, 0)))
        out_shape.insert(0, jax.ShapeDtypeStruct((r, kcols), BF16))
    return pl.pallas_call(
        functools.partial(_mla_proj_kernel, rope=rope, want_q=want_q,
                          scale=(MLA_NOPE + MLA_ROPE) ** -0.5 * LOG2E),
        grid=(r // tm,),
        in_specs=in_specs,
        out_specs=out_specs,
        out_shape=out_shape,
        compiler_params=_params("arbitrary"),
        name="mla_proj",
    )(*args)


def _attn_kernel(*refs, n_seg, band, use_sink, tq, seq, hp, group, dq, dv, tk):
    refs = list(refs)
    if use_sink:
        sink_ref = refs.pop(0)
    q_ref = refs.pop(0)
    o_ref = refs.pop()
    dvx = 2 * dv
    if band:
        wk = tq + 2 * WINDOW
        q0 = pl.program_id(2) * tq
        start = pl.multiple_of(jnp.clip(q0 - WINDOW, 0, seq - wk), LANES)
        dist = (lax.broadcasted_iota(I32, (tq, wk), 1) - lax.broadcasted_iota(I32, (tq, wk), 0)) + (start - q0)
        valid = jnp.abs(dist) <= WINDOW
    for j in range(hp):
        kv = j // group
        q = q_ref[:, j * dq:(j + 1) * dq]
        m = None
        acc = None
        for s_i in range(n_seg):
            k_ref, v_ref = refs[2 * s_i], refs[2 * s_i + 1]
            slen = k_ref.shape[0]
            masked = band and s_i == n_seg - 1
            chunks = [(start, wk)] if masked else [(c0, min(tk, slen - c0)) for c0 in range(0, slen, tk)]
            for c0, cl in chunks:
                s = _dot_nt(q, k_ref[pl.ds(c0, cl), kv * dq:(kv + 1) * dq])
                if masked:
                    s = jnp.where(valid, s, -jnp.inf)
                v = v_ref[pl.ds(c0, cl), kv * dvx:(kv + 1) * dvx]
                mc = s.max(axis=-1, keepdims=True)
                if m is None:
                    m = mc
                    acc = _dot(jnp.exp2(s - m).astype(BF16), v)
                else:
                    m_new = jnp.maximum(m, mc)
                    acc = jnp.exp2(m - m_new) * acc + _dot(jnp.exp2(s - m_new).astype(BF16), v)
                    m = m_new
        den = acc[:, dv:]
        if use_sink:
            sink = sink_ref[pl.program_id(1) * hp + j] * LOG2E
            den = den + jnp.exp2(sink - m)
        o_ref[:, j * dv:(j + 1) * dv] = (acc[:, :dv] / den).astype(o_ref.dtype)


def _attention(q_arr, q_col0, segs, sink, *, batch, seq, n_heads, group, dq, dv, band, tq_cap, hp, tk=512):
    tq = _tile(seq, tq_cap)
    nq = seq // tq
    use_sink = sink is not None
    if hp <= group:
        assert group % hp == 0
        kvp, kv_of, step_group = 1, (lambda hg: hg // (group // hp)), hp
    else:
        assert hp % group == 0
        kvp, kv_of, step_group = hp // group, (lambda hg: hg), group
    qw, kw, vw = hp * dq, kvp * dq, kvp * 2 * dv
    assert q_col0 % qw == 0
    in_specs, args = [], []
    if use_sink:
        in_specs.append(pl.BlockSpec(memory_space=pltpu.SMEM))
        args.append(sink)
    in_specs.append(pl.BlockSpec((tq, qw), lambda b, hg, i: (b * nq + i, q_col0 // qw + hg)))
    args.append(q_arr)
    for k_arr, k_col0, v_arr, v_col0, slen in segs:
        assert k_col0 % kw == 0 and v_col0 % vw == 0
        in_specs.append(pl.BlockSpec((slen, kw), lambda b, hg, i, c=k_col0 // kw: (b, c + kv_of(hg))))
        in_specs.append(pl.BlockSpec((slen, vw), lambda b, hg, i, c=v_col0 // vw: (b, c + kv_of(hg))))
        args += [k_arr, v_arr]
    return pl.pallas_call(
        functools.partial(_attn_kernel, n_seg=len(segs), band=band, use_sink=use_sink, tq=tq, seq=seq, hp=hp,
                          group=step_group, dq=dq, dv=dv, tk=tk),
        grid=(batch, n_heads // hp, nq),
        in_specs=in_specs,
        out_specs=pl.BlockSpec((tq, hp * dv), lambda b, hg, i: (b * nq + i, hg)),
        out_shape=jax.ShapeDtypeStruct((batch * seq, n_heads * dv), BF16),
        compiler_params=_params("arbitrary", "arbitrary", "arbitrary"),
        name="attention",
    )(*args)


def _outproj_kernel(*refs, n_in):
    o_refs = refs[:n_in]
    (w_ref, h_ref, g1_ref, gate_ref, g2_ref, sh_ref, sc_ref, wr_ref, hn_ref, xf_ref, aff_ref) = refs[n_in:]
    tm = h_ref.shape[0]
    sub = min(tm, OUTPROJ_SUB)
    for r in range(0, tm, sub):
        rows = slice(r, r + sub)
        y = None
        k0 = 0
        for o_ref in o_refs:
            kk = o_ref.shape[1]
            part = _dot(o_ref[rows, :], w_ref[k0:k0 + kk, :])
            y = part if y is None else y + part
            k0 += kk
        hn = h_ref[rows, :] + gate_ref[...] * _rms(y, g1_ref[...])
        hn_ref[rows, :] = hn
        xf = _rms(hn, g2_ref[...]) * (1.0 + sc_ref[...]) + sh_ref[...]
        xf_ref[rows, :] = xf.astype(BF16)
        logits = lax.dot_general(wr_ref[...], xf, (((1,), (1,)), ((), ())),
                                 precision=lax.Precision.HIGHEST, preferred_element_type=F32)
        e = jnp.exp(logits - logits.max(axis=0, keepdims=True))
        aff_ref[:, rows] = e / e.sum(axis=0, keepdims=True)


def _outproj(o_list, w_bf, h, g1, gate, g2, shift, scale_m, wr_t, row_fn, seq):
    r, d = h.shape
    tm = _tile(seq, 512)
    n_e = wr_t.shape[0]
    in_specs = [pl.BlockSpec((tm, o.shape[1]), lambda i: (i, 0)) for o in o_list]
    in_specs += [
        _const_spec(w_bf.shape),
        pl.BlockSpec((tm, d), lambda i: (i, 0)),
        _const_spec((1, d)),
        _mod_spec(d, lambda i: row_fn(i * tm)),
        _const_spec((1, d)),
        _mod_spec(d, lambda i: row_fn(i * tm)),
        _mod_spec(d, lambda i: row_fn(i * tm)),
        _const_spec((n_e, d)),
    ]
    return pl.pallas_call(
        functools.partial(_outproj_kernel, n_in=len(o_list)),
        grid=(r // tm,),
        in_specs=in_specs,
        out_specs=[
            pl.BlockSpec((tm, d), lambda i: (i, 0)),
            pl.BlockSpec((tm, d), lambda i: (i, 0)),
            pl.BlockSpec((n_e, tm), lambda i: (0, i)),
        ],
        out_shape=[
            jax.ShapeDtypeStruct((r, d), F32),
            jax.ShapeDtypeStruct((r, d), BF16),
            jax.ShapeDtypeStruct((n_e, r), F32),
        ],
        compiler_params=_params("arbitrary"),
        name="outproj",
    )(*o_list, w_bf, h, g1, gate, g2, shift, scale_m, wr_t)


def _route_kernel(aff_ref, posm_ref, pose_ref, gsel_ref, *, cap):
    n_e, n = aff_ref.shape
    bits = pltpu.bitcast(aff_ref[...], I32)
    capf = float(cap)

    def search(_, carry):
        lo, hi = carry
        mid = lo + ((hi - lo + 1) >> 1)
        cnt = jnp.sum(jnp.where(bits >= mid, 1.0, 0.0), axis=1, keepdims=True)
        ok = cnt >= capf
        return jnp.where(ok, mid, lo), jnp.where(ok, hi, mid - 1)

    lo0 = jnp.zeros((n_e, 1), I32)
    hi0 = jnp.full((n_e, 1), 0x7F800000, I32)
    thr, _ = lax.fori_loop(0, 31, search, (lo0, hi0))
    need = capf - jnp.sum(jnp.where(bits > thr, 1.0, 0.0), axis=1, keepdims=True)
    upper = jnp.where(lax.broadcasted_iota(I32, (LANES, LANES), 0) < lax.broadcasted_iota(I32, (LANES, LANES), 1),
                      1.0, 0.0).astype(BF16)
    run_eq = jnp.zeros((n_e, 1), F32)
    run_sel = jnp.zeros((n_e, 1), F32)
    for j in range(n // LANES):
        sl = slice(j * LANES, (j + 1) * LANES)
        a = aff_ref[:, sl]
        bb = pltpu.bitcast(a, I32)
        eq = jnp.where(bb == thr, 1.0, 0.0)
        rank = _dot(eq.astype(BF16), upper) + run_eq
        run_eq = run_eq + eq.sum(axis=1, keepdims=True)
        sel = (bb > thr) | ((bb == thr) & (rank < need))
        self_f = jnp.where(sel, 1.0, 0.0)
        pos = _dot(self_f.astype(BF16), upper) + run_sel
        run_sel = run_sel + self_f.sum(axis=1, keepdims=True)
        pos_i = pos.astype(I32)
        pose_ref[0, :, sl] = pos_i
        posm_ref[0, :, sl] = jnp.where(sel, pos_i, -1)
        gsel_ref[0, :, sl] = jnp.where(sel, a, 0.0)


def _route(aff, batch, n, cap):
    n_e = aff.shape[0]
    spec = pl.BlockSpec((1, n_e, n), lambda b: (b, 0, 0))
    return pl.pallas_call(
        functools.partial(_route_kernel, cap=cap),
        grid=(batch,),
        in_specs=[pl.BlockSpec((n_e, n), lambda b: (0, b))],
        out_specs=[spec, spec, spec],
        out_shape=[
            jax.ShapeDtypeStruct((batch, n_e, n), I32),
            jax.ShapeDtypeStruct((batch, n_e, n), I32),
            jax.ShapeDtypeStruct((batch, n_e, n), F32),
        ],
        compiler_params=_params("arbitrary"),
        name="route",
    )(aff)


def _gather_kernel(cs_ref, posm_ref, gsel_ref, x_ref, xg_ref, gs_ref, acc_ref, gacc_ref, *, win, tc, n_chunks,
                   n_e, cap):
    b, e = pl.program_id(0), pl.program_id(1)
    base = (b * n_e + e) * (n_chunks + 1)
    acc_ref[...] = jnp.zeros_like(acc_ref)
    gacc_ref[...] = jnp.zeros_like(gacc_ref)
    row = lax.broadcasted_iota(I32, (win, tc), 0)

    def fill(c, a0, w):
        first = a0 + w * win
        start = pl.multiple_of(jnp.minimum(first, cap - win), F32_SUBLANES)
        slot = start + row
        hit = (posm_ref[0, 0, c:c + 1, :] == slot) & (slot >= first)
        onehot = jnp.where(hit, 1.0, 0.0).astype(BF16)
        acc_ref[pl.ds(start, win), :] += _dot(onehot, x_ref[c * tc:(c + 1) * tc, :])
        gacc_ref[pl.ds(start, win), :] += jnp.where(hit, gsel_ref[0, 0, c:c + 1, :], 0.0).sum(
            axis=1, keepdims=True)

    starts = [(cs_ref[base + c] // F32_SUBLANES) * F32_SUBLANES for c in range(n_chunks)]
    for c in range(n_chunks):
        fill(c, starts[c], 0)
    for c in range(n_chunks):
        n_win = (cs_ref[base + c + 1] - starts[c] + win - 1) // win

        def more(w, carry, c=c):
            fill(c, starts[c], w)
            return carry

        lax.fori_loop(1, n_win, more, 0)

    xg_ref[0] = acc_ref[...].astype(BF16)
    gs_ref[0] = gacc_ref[...]


def _gather(xf, posm, pose, gsel, batch, n, cap):
    d = xf.shape[1]
    n_e = posm.shape[1]
    tc = _tile(n, 256)
    win = min(cap, LANES)
    n_chunks = n // tc
    cstart = jnp.concatenate([pose[:, :, ::tc], jnp.full((batch, n_e, 1), cap, I32)], axis=-1).reshape(-1)
    posm4 = posm.reshape(batch, n_e, n_chunks, tc)
    gsel4 = gsel.reshape(batch, n_e, n_chunks, tc)
    grid_spec = pltpu.PrefetchScalarGridSpec(
        num_scalar_prefetch=1,
        grid=(batch, n_e),
        in_specs=[
            pl.BlockSpec((1, 1, n_chunks, tc), lambda b, e, cs: (b, e, 0, 0)),
            pl.BlockSpec((1, 1, n_chunks, tc), lambda b, e, cs: (b, e, 0, 0)),
            pl.BlockSpec((n, d), lambda b, e, cs: (b, 0)),
        ],
        out_specs=[
            pl.BlockSpec((1, cap, d), lambda b, e, cs: (e, b, 0)),
            pl.BlockSpec((1, cap, 1), lambda b, e, cs: (e, b, 0)),
        ],
        scratch_shapes=[pltpu.VMEM((cap, d), F32), pltpu.VMEM((cap, 1), F32)],
    )
    return pl.pallas_call(
        functools.partial(_gather_kernel, win=win, tc=tc, n_chunks=n_chunks, n_e=n_e, cap=cap),
        grid_spec=grid_spec,
        out_shape=[
            jax.ShapeDtypeStruct((n_e, batch * cap, d), BF16),
            jax.ShapeDtypeStruct((n_e, batch * cap, 1), F32),
        ],
        compiler_params=_params("arbitrary", "arbitrary"),
        name="moe_gather",
    )(cstart, posm4, gsel4, xf)


def _ffn_kernel(*refs, with_ctx):
    if with_ctx:
        xl_ref, xc_ref, wg_ref, wu_ref, wd_ref, gl_ref, gc_ref, yl_ref, yc_ref, accl_ref, accc_ref = refs
    else:
        xl_ref, wg_ref, wu_ref, wd_ref, gl_ref, yl_ref, accl_ref = refs
    i, f = pl.program_id(1), pl.program_id(2)
    last_f = pl.num_programs(2) - 1
    tf = wg_ref.shape[2]
    fw = min(tf, FFN_SUB)

    def run(x_ref, gs_ref, y_ref, acc_ref):
        @pl.when(f == 0)
        def _():
            acc_ref[...] = jnp.zeros_like(acc_ref)

        x = x_ref[0]
        hid = []
        for c0 in range(0, tf, fw):
            hg = _dot(x, wg_ref[0, :, c0:c0 + fw].astype(BF16))
            hu = _dot(x, wu_ref[0, :, c0:c0 + fw].astype(BF16))
            hid.append((_silu(hg) * hu).astype(BF16))
        acc_ref[...] += _dot(jnp.concatenate(hid, axis=1), wd_ref[0].astype(BF16))

        @pl.when(f == last_f)
        def _():
            y_ref[0] = (acc_ref[...] * gs_ref[0]).astype(BF16)

    run(xl_ref, gl_ref, yl_ref, accl_ref)
    if with_ctx:
        @pl.when(i == pl.num_programs(1) - 1)
        def _():
            run(xc_ref, gc_ref, yc_ref, accc_ref)


def _expert_ffn(layer, w_gate, w_up, w_down, xg, gslot, xg_c=None, gslot_c=None):
    n_e, m, d = xg.shape
    ff = w_gate.shape[3]
    tm = _tile(m, 1024)
    tf = _tile(ff, 512)
    with_ctx = xg_c is not None
    x_spec = pl.BlockSpec((1, tm, d), lambda e, i, f: (e, i, 0))
    g_spec = pl.BlockSpec((1, tm, 1), lambda e, i, f: (e, i, 0))
    w_specs = [
        pl.BlockSpec((None, 1, d, tf), lambda e, i, f: (layer, e, 0, f)),
        pl.BlockSpec((None, 1, d, tf), lambda e, i, f: (layer, e, 0, f)),
        pl.BlockSpec((None, 1, tf, d), lambda e, i, f: (layer, e, f, 0)),
    ]
    out_specs = [x_spec]
    out_shape = [jax.ShapeDtypeStruct((n_e, m, d), BF16)]
    scratch = [pltpu.VMEM((tm, d), F32)]
    if with_ctx:
        mc = xg_c.shape[1]
        xc_spec = pl.BlockSpec((1, mc, d), lambda e, i, f: (e, 0, 0))
        gc_spec = pl.BlockSpec((1, mc, 1), lambda e, i, f: (e, 0, 0))
        in_specs = [x_spec, xc_spec] + w_specs + [g_spec, gc_spec]
        args = [xg, xg_c, w_gate, w_up, w_down, gslot, gslot_c]
        out_specs.append(xc_spec)
        out_shape.append(jax.ShapeDtypeStruct((n_e, mc, d), BF16))
        scratch.append(pltpu.VMEM((mc, d), F32))
    else:
        in_specs = [x_spec] + w_specs + [g_spec]
        args = [xg, w_gate, w_up, w_down, gslot]
    return pl.pallas_call(
        functools.partial(_ffn_kernel, with_ctx=with_ctx),
        grid=(n_e, m // tm, ff // tf),
        in_specs=in_specs,
        out_specs=out_specs,
        out_shape=out_shape,
        scratch_shapes=scratch,
        compiler_params=_params("arbitrary", "arbitrary", "arbitrary"),
        name="expert_ffn",
    )(*args)


def _combine_kernel(cs_ref, posm_ref, y_ref, h_ref, g_ref, gate_ref, o_ref, *, tt, win, cap, n_e, n_tiles):
    b, i = pl.program_id(0), pl.program_id(1)
    acc = None
    for e in range(n_e):
        if cap <= win:
            a0 = 0
            yw = y_ref[e]
        else:
            s0 = cs_ref[(b * n_e + e) * n_tiles + i]
            a0 = jnp.minimum((s0 // BF16_SUBLANES) * BF16_SUBLANES, cap - win)
            a0 = pl.multiple_of(a0, BF16_SUBLANES)
            yw = y_ref[e, pl.ds(a0, win), :]
        wn = yw.shape[0]
        slot = a0 + lax.broadcasted_iota(I32, (wn, tt), 0)
        onehot = jnp.where(posm_ref[0, e:e + 1, :] == slot, 1.0, 0.0).astype(BF16)
        part = lax.dot_general(onehot, yw, (((0,), (0,)), ((), ())), preferred_element_type=F32)
        acc = part if acc is None else acc + part
    o_ref[...] = h_ref[...] + gate_ref[...] * _rms(acc, g_ref[...])


def _combine(y, posm, pose, h, g, gate, row_fn, batch, n, cap):
    n_e = y.shape[0]
    d = y.shape[2]
    tt = LANES
    win = tt + BF16_SUBLANES
    n_tiles = n // tt
    cstart = pose[:, :, ::tt].reshape(-1)
    grid_spec = pltpu.PrefetchScalarGridSpec(
        num_scalar_prefetch=1,
        grid=(batch, n_tiles),
        in_specs=[
            pl.BlockSpec((1, n_e, tt), lambda b, i, cs: (b, 0, i)),
            pl.BlockSpec((n_e, cap, d), lambda b, i, cs: (0, b, 0), pipeline_mode=pl.Buffered(1)),
            pl.BlockSpec((tt, d), lambda b, i, cs: (b * n_tiles + i, 0)),
            pl.BlockSpec((1, d), lambda b, i, cs: (0, 0)),
            pl.BlockSpec((None, 1, d), lambda b, i, cs: (row_fn(b * n), 0, 0)),
        ],
        out_specs=pl.BlockSpec((tt, d), lambda b, i, cs: (b * n_tiles + i, 0)),
    )
    return pl.pallas_call(
        functools.partial(_combine_kernel, tt=tt, win=win, cap=cap, n_e=n_e, n_tiles=n_tiles),
        grid_spec=grid_spec,
        out_shape=jax.ShapeDtypeStruct(h.shape, F32),
        compiler_params=_params("arbitrary", "arbitrary"),
        name="moe_combine",
    )(cstart, posm, y, h, g, gate)


def _moe_dispatch(xf, aff, batch, n):
    cap = CAPACITY_FACTOR * n // N_EXPERTS
    posm, pose, gsel = _route(aff, batch, n, cap)
    xg, gslot = _gather(xf, posm, pose, gsel, batch, n, cap)
    return xg, gslot, (posm, pose, cap)


def _rope_angles(seq, rot_dim):
    n_rows = seq // GRID_W
    row = jnp.repeat(jnp.arange(n_rows), GRID_W)
    col = jnp.tile(jnp.arange(GRID_W), n_rows)
    n_freq = rot_dim // 4
    inv = ROPE_BASE ** (-jnp.arange(n_freq, dtype=F32) / n_freq)
    ang = jnp.concatenate([row[:, None] * inv, col[:, None] * inv], axis=-1)
    return jnp.cos(ang), jnp.sin(ang)


def kernel(x, c, ctx, c_ctx, mod_w, mod_b, norm_g, ev_w_in, ev_w_out, ev_sink, ev_qk_norm, od_w_in, od_q_norm,
           od_kv_norm, od_w_uq, od_w_ukv, od_w_out, router_w, exp_w_gate, exp_w_up, exp_w_down):
    batch, seq, d = x.shape
    n_ctx = ctx.shape[1]
    depth = mod_w.shape[0]
    assert batch < MOD_ROWS and seq % LANES == 0 and n_ctx % LANES == 0 and seq >= 4 * WINDOW

    cs = jnp.concatenate([c, c_ctx[None, :], jnp.zeros((MOD_ROWS - batch - 1, d), F32)], axis=0)
    mods = _modulation(cs, mod_w, mod_b).reshape(depth, MOD_ROWS, N_MOD, 1, d)

    def lat_row(r):
        return r // seq

    def ctx_row(r):
        return batch

    cos_h, sin_h = _rope_angles(seq, HEAD_DIM)
    rope_even = (jnp.concatenate([cos_h, cos_h], axis=-1), jnp.concatenate([-sin_h, sin_h], axis=-1))
    cos_m, sin_m = _rope_angles(seq, MLA_ROPE)
    half = MLA_ROPE // 2
    zeros = lambda w: jnp.zeros((seq, w), F32)
    rope_mla = (
        jnp.concatenate([cos_m, cos_m, zeros(LANES - MLA_ROPE)], axis=-1),
        jnp.concatenate([-sin_m, zeros(LANES - half)], axis=-1),
        jnp.concatenate([zeros(half), sin_m, zeros(LANES - MLA_ROPE)], axis=-1),
    )

    h_lat = x.reshape(batch * seq, d)
    h_ctx = ctx.reshape(batch * n_ctx, d)
    for layer in range(depth):
        with_ctx = layer < depth - 1
        i = layer // 2
        g = norm_g[layer].reshape(4, 1, d)
        m = [mods[layer, :, k] for k in range(N_MOD)]
        wr_t = router_w[layer].T
        if layer % 2 == 0:
            w_in = ev_w_in[i].astype(BF16)
            w_out = ev_w_out[i].astype(BF16)
            qkv_l = _even_inproj(h_lat, g[0], m[0], m[1], w_in, ev_qk_norm[i], rope_even, lat_row, seq)
            qkv_c = _even_inproj(h_ctx, g[0], m[0], m[1], w_in, ev_qk_norm[i], None, ctx_row, n_ctx)
            ga, gb = A_HEADS // A_KV_HEADS, B_HEADS // B_KV_HEADS
            hd = HEAD_DIM
            a_kw = dict(batch=batch, n_heads=A_HEADS, group=ga, dq=hd, dv=hd)
            b_kw = dict(batch=batch, n_heads=B_HEADS, group=gb, dq=hd, dv=hd, band=False)
            a_ctx = (qkv_c, AK0 * hd, qkv_c, AV0 * hd, n_ctx)
            b_ctx = (qkv_c, BK0 * hd, qkv_c, BV0 * hd, n_ctx)
            o_a = _attention(qkv_l, AQ0 * hd, [a_ctx, (qkv_l, AK0 * hd, qkv_l, AV0 * hd, seq)], ev_sink[i],
                             seq=seq, band=True, tq_cap=256, hp=4, **a_kw)
            o_b = _attention(qkv_l, BQ0 * hd, [b_ctx, (qkv_l, BK0 * hd, qkv_l, BV0 * hd, seq)], None,
                             seq=seq, tq_cap=512, hp=2, **b_kw)
            o_lat = [o_a, o_b]
            if with_ctx:
                o_ac = _attention(qkv_c, AQ0 * hd, [a_ctx], ev_sink[i], seq=n_ctx, band=False, tq_cap=512,
                                  hp=4, **a_kw)
                o_bc = _attention(qkv_c, BQ0 * hd, [b_ctx], None, seq=n_ctx, tq_cap=512, hp=4, **b_kw)
                o_ctx = [o_ac, o_bc]
        else:
            w_in = od_w_in[i]
            w_qkv = w_in[:, :MLA_Q_RANK + MLA_KV_RANK].astype(BF16)
            w_kr = jnp.pad(w_in[:, MLA_Q_RANK + MLA_KV_RANK:], ((0, 0), (0, LANES - MLA_ROPE))).astype(BF16)
            w_uq = od_w_uq[i].reshape(MLA_Q_RANK, MLA_HEADS, MLA_NOPE + MLA_ROPE)
            w_uq = jnp.pad(w_uq, ((0, 0), (0, 0), (0, MLA_QK_PAD - MLA_NOPE - MLA_ROPE)))
            w_uq = w_uq.reshape(MLA_Q_RANK, MLA_HEADS * MLA_QK_PAD).astype(BF16)
            w_ukv = od_w_ukv[i].astype(BF16)
            w_out = od_w_out[i].astype(BF16)
            qn, kvn = od_q_norm[i].reshape(1, -1), od_kv_norm[i].reshape(1, -1)
            q_l, k_l, v_l = _mla_proj(h_lat, g[0], m[0], m[1], w_qkv, w_kr, qn, kvn, w_uq, w_ukv, rope_mla,
                                      lat_row, seq, True)
            proj_c = _mla_proj(h_ctx, g[0], m[0], m[1], w_qkv, w_kr, qn, kvn, w_uq, w_ukv, None, ctx_row,
                               n_ctx, with_ctx)
            k_c, v_c = proj_c[-2], proj_c[-1]
            m_kw = dict(batch=batch, n_heads=MLA_HEADS, group=1, dq=MLA_QK_PAD, dv=MLA_V, band=False, tq_cap=512,
                        hp=2)
            o_lat = [_attention(q_l, 0, [(k_c, 0, v_c, 0, n_ctx), (k_l, 0, v_l, 0, seq)], None, seq=seq, **m_kw)]
            if with_ctx:
                o_ctx = [_attention(proj_c[0], 0, [(k_c, 0, v_c, 0, n_ctx)], None, seq=n_ctx, **m_kw)]
        ew = (layer, exp_w_gate, exp_w_up, exp_w_down)
        h_lat, xf, aff = _outproj(o_lat, w_out, h_lat, g[1], m[2], g[2], m[3], m[4], wr_t, lat_row, seq)
        xg, gslot, (posm, pose, cap) = _moe_dispatch(xf, aff, batch, seq)
        if with_ctx:
            h_ctx, xf_c, aff_c = _outproj(o_ctx, w_out, h_ctx, g[1], m[2], g[2], m[3], m[4], wr_t, ctx_row,
                                          n_ctx)
            xg_c, gslot_c, (posm_c, pose_c, cap_c) = _moe_dispatch(xf_c, aff_c, batch, n_ctx)
            y, y_c = _expert_ffn(*ew, xg, gslot, xg_c, gslot_c)
            h_ctx = _combine(y_c, posm_c, pose_c, h_ctx, g[3], m[5], ctx_row, batch, n_ctx, cap_c)
        else:
            (y,) = _expert_ffn(*ew, xg, gslot)
        h_lat = _combine(y, posm, pose, h_lat, g[3], m[5], lat_row, batch, seq, cap)
    return h_lat.reshape(batch, seq, d)
```

```python
import functools

import jax
import jax.numpy as jnp
from jax import lax
from jax.experimental import pallas as pl
from jax.experimental.pallas import tpu as pltpu

F32 = jnp.float32
BF16 = jnp.bfloat16
I32 = jnp.int32

EPS = 1e-6
GRID_W = 64
WINDOW = 128
ROPE_BASE = 10000.0
HEAD_DIM = 128
A_HEADS = 8
A_KV_HEADS = 2
B_HEADS = 8
B_KV_HEADS = 2
MLA_HEADS = 16
MLA_Q_RANK = 512
MLA_KV_RANK = 512
MLA_NOPE = 128
MLA_ROPE = 64
MLA_V = 128
MLA_QK_PAD = 256
N_EXPERTS = 16
CAPACITY_FACTOR = 2
N_MOD = 6
MOD_ROWS = 8

LANES = 128
BF16_SUBLANES = 16
F32_SUBLANES = 8
VMEM_LIMIT_BYTES = 60 * 1024 * 1024
FFN_SUB = 256
OUTPROJ_SUB = 512

LOG2E = 1.4426950408889634

EVEN_IN_COLS = (A_HEADS + 2 * A_KV_HEADS + B_HEADS + 2 * B_KV_HEADS) * HEAD_DIM
AQ0 = 0
BQ0 = AQ0 + A_HEADS
AK0 = BQ0 + B_HEADS
BK0 = AK0 + A_KV_HEADS
AV0 = BK0 + B_KV_HEADS
BV0 = AV0 + 2 * A_KV_HEADS
EVEN_SLOTS = BV0 + 2 * B_KV_HEADS
EVEN_COLS = EVEN_SLOTS * HEAD_DIM
_W_AK0 = A_HEADS
_W_AV0 = _W_AK0 + A_KV_HEADS
_W_BQ0 = _W_AV0 + A_KV_HEADS
_W_BK0 = _W_BQ0 + B_HEADS
_W_BV0 = _W_BK0 + B_KV_HEADS


def _even_head(wh):
    if wh < _W_AK0:
        return AQ0 + wh, "aq"
    if wh < _W_AV0:
        return AK0 + wh - _W_AK0, "ak"
    if wh < _W_BQ0:
        return AV0 + 2 * (wh - _W_AV0), "av"
    if wh < _W_BK0:
        return BQ0 + wh - _W_BQ0, "bq"
    if wh < _W_BV0:
        return BK0 + wh - _W_BK0, "bk"
    return BV0 + 2 * (wh - _W_BV0), "bv"


def _params(*sem):
    return pltpu.CompilerParams(dimension_semantics=sem, vmem_limit_bytes=VMEM_LIMIT_BYTES)


def _tile(n, cap):
    t = min(n, cap)
    while n % t:
        t -= 1
    return t


def _const_spec(shape):
    nd = len(shape)
    return pl.BlockSpec(shape, lambda *_: (0,) * nd)


def _rms(x, g):
    return x * lax.rsqrt(jnp.mean(x * x, axis=-1, keepdims=True) + EPS) * g


def _silu(x):
    return x / (1.0 + jnp.exp(-x))


def _dot(a, b):
    return jnp.dot(a, b, preferred_element_type=F32)


def _dot_nt(a, b):
    return lax.dot_general(a, b, (((1,), (1,)), ((), ())), preferred_element_type=F32)


def _split_bf16(x):
    hi = x.astype(BF16)
    return hi, (x - hi.astype(F32)).astype(BF16)


def _mod_kernel(cs_ref, w_ref, b_ref, o_ref):
    s_hi, s_lo = _split_bf16(_silu(cs_ref[...]))
    w_hi, w_lo = _split_bf16(w_ref[0])
    o_ref[0] = _dot(s_hi, w_hi) + (_dot(s_hi, w_lo) + _dot(s_lo, w_hi)) + b_ref[0]


def _modulation(cs, mod_w, mod_b):
    depth, d, n6 = mod_w.shape
    tn = _tile(n6, 1024)
    return pl.pallas_call(
        _mod_kernel,
        grid=(depth, n6 // tn),
        in_specs=[
            _const_spec((MOD_ROWS, d)),
            pl.BlockSpec((1, d, tn), lambda l, j: (l, 0, j)),
            pl.BlockSpec((1, 1, tn), lambda l, j: (l, 0, j)),
        ],
        out_specs=pl.BlockSpec((1, MOD_ROWS, tn), lambda l, j: (l, 0, j)),
        out_shape=jax.ShapeDtypeStruct((depth, MOD_ROWS, n6), F32),
        compiler_params=_params("arbitrary", "arbitrary"),
        name="modulation",
    )(cs, mod_w, mod_b.reshape(depth, 1, n6))


def _mod_spec(d, row_fn):
    return pl.BlockSpec((None, 1, d), lambda i: (row_fn(i), 0, 0))


def _even_inproj_kernel(*refs, rope, scale):
    if rope:
        x_ref, g_ref, sh_ref, sc_ref, w_ref, qkg_ref, cos_ref, sin_ref, o_ref = refs
        cos, sin = cos_ref[...], sin_ref[...]
    else:
        x_ref, g_ref, sh_ref, sc_ref, w_ref, qkg_ref, o_ref = refs
    a = _rms(x_ref[...], g_ref[...]) * (1.0 + sc_ref[...]) + sh_ref[...]
    ab = a.astype(BF16)
    ones = jnp.ones((x_ref.shape[0], HEAD_DIM), BF16)
    for j in range(EVEN_IN_COLS // (2 * HEAD_DIM)):
        acc = _dot(ab, w_ref[:, j * 2 * HEAD_DIM:(j + 1) * 2 * HEAD_DIM])
        for hh in range(2):
            slot, kind = _even_head(2 * j + hh)
            v = acc[:, hh * HEAD_DIM:(hh + 1) * HEAD_DIM]
            if kind == "bq":
                v = _rms(v, qkg_ref[0:1, :])
            elif kind == "bk":
                v = _rms(v, qkg_ref[1:2, :])
            if rope and kind[1] != "v":
                v = v * cos + pltpu.roll(v, HEAD_DIM // 2, 1) * sin
            if kind[1] == "q":
                v = v * scale
            o_ref[:, slot * HEAD_DIM:(slot + 1) * HEAD_DIM] = v.astype(BF16)
            if kind[1] == "v":
                o_ref[:, (slot + 1) * HEAD_DIM:(slot + 2) * HEAD_DIM] = ones


def _even_inproj(h, g, shift, scale_m, w_bf, qk_gain, rope_tabs, row_fn, seq):
    r, d = h.shape
    tm = _tile(seq, 512)
    rope = rope_tabs is not None
    in_specs = [
        pl.BlockSpec((tm, d), lambda i: (i, 0)),
        _const_spec((1, d)),
        _mod_spec(d, lambda i: row_fn(i * tm)),
        _mod_spec(d, lambda i: row_fn(i * tm)),
        _const_spec((d, EVEN_IN_COLS)),
        _const_spec((2, HEAD_DIM)),
    ]
    args = [h, g, shift, scale_m, w_bf, qk_gain]
    if rope:
        nt = seq // tm
        in_specs += [pl.BlockSpec((tm, HEAD_DIM), lambda i: (i % nt, 0))] * 2
        args += list(rope_tabs)
    return pl.pallas_call(
        functools.partial(_even_inproj_kernel, rope=rope, scale=HEAD_DIM ** -0.5 * LOG2E),
        grid=(r // tm,),
        in_specs=in_specs,
        out_specs=pl.BlockSpec((tm, EVEN_COLS), lambda i: (i, 0)),
        out_shape=jax.ShapeDtypeStruct((r, EVEN_COLS), BF16),
        compiler_params=_params("arbitrary"),
        name="even_inproj",
    )(*args)


def _rope_pad(v, c, s1, s2):
    return v * c + pltpu.roll(v, LANES - MLA_ROPE // 2, 1) * s1 + pltpu.roll(v, MLA_ROPE // 2, 1) * s2


def _mla_proj_kernel(*refs, rope, want_q, scale):
    refs = list(refs)
    x_ref, g_ref, sh_ref, sc_ref, win_ref, wkr_ref, qn_ref, kvn_ref = refs[:8]
    refs = refs[8:]
    if want_q:
        wuq_ref = refs.pop(0)
    wukv_ref = refs.pop(0)
    if rope:
        c, s1, s2 = refs[0][...], refs[1][...], refs[2][...]
        refs = refs[3:]
    if want_q:
        q_ref = refs.pop(0)
    k_ref, v_ref = refs
    a = _rms(x_ref[...], g_ref[...]) * (1.0 + sc_ref[...]) + sh_ref[...]
    ab = a.astype(BF16)
    low = _dot(ab, win_ref[...])
    kr = _dot(ab, wkr_ref[...])
    if rope:
        kr = _rope_pad(kr, c, s1, s2)
    krb = kr.astype(BF16)
    if want_q:
        cq = _rms(low[:, :MLA_Q_RANK], qn_ref[...]).astype(BF16)
        for h in range(MLA_HEADS):
            acc = _dot(cq, wuq_ref[:, h * MLA_QK_PAD:(h + 1) * MLA_QK_PAD])
            qr = acc[:, MLA_NOPE:]
            if rope:
                qr = _rope_pad(qr, c, s1, s2)
            q_ref[:, h * MLA_QK_PAD:h * MLA_QK_PAD + MLA_NOPE] = (acc[:, :MLA_NOPE] * scale).astype(BF16)
            q_ref[:, h * MLA_QK_PAD + MLA_NOPE:(h + 1) * MLA_QK_PAD] = (qr * scale).astype(BF16)
    ckv = _rms(low[:, MLA_Q_RANK:], kvn_ref[...]).astype(BF16)
    hw = MLA_NOPE + MLA_V
    for h in range(MLA_HEADS):
        acc = _dot(ckv, wukv_ref[:, h * hw:(h + 1) * hw])
        k_ref[:, h * MLA_QK_PAD:h * MLA_QK_PAD + MLA_NOPE] = acc[:, :MLA_NOPE].astype(BF16)
        k_ref[:, h * MLA_QK_PAD + MLA_NOPE:(h + 1) * MLA_QK_PAD] = krb
        v_ref[:, 2 * h * MLA_V:(2 * h + 1) * MLA_V] = acc[:, MLA_NOPE:].astype(BF16)
        v_ref[:, (2 * h + 1) * MLA_V:(2 * h + 2) * MLA_V] = jnp.ones((x_ref.shape[0], MLA_V), BF16)


def _mla_proj(h, g, shift, scale_m, w_in_bf, w_kr_bf, q_norm, kv_norm, w_uq_bf, w_ukv_bf, rope_tabs,
              row_fn, seq, want_q):
    r, d = h.shape
    tm = _tile(seq, 512)
    rope = rope_tabs is not None
    in_specs = [
        pl.BlockSpec((tm, d), lambda i: (i, 0)),
        _const_spec((1, d)),
        _mod_spec(d, lambda i: row_fn(i * tm)),
        _mod_spec(d, lambda i: row_fn(i * tm)),
        _const_spec(w_in_bf.shape),
        _const_spec(w_kr_bf.shape),
        _const_spec((1, MLA_Q_RANK)),
        _const_spec((1, MLA_KV_RANK)),
    ]
    args = [h, g, shift, scale_m, w_in_bf, w_kr_bf, q_norm, kv_norm]
    if want_q:
        in_specs.append(_const_spec(w_uq_bf.shape))
        args.append(w_uq_bf)
    in_specs.append(_const_spec(w_ukv_bf.shape))
    args.append(w_ukv_bf)
    if rope:
        nt = seq // tm
        in_specs += [pl.BlockSpec((tm, LANES), lambda i: (i % nt, 0))] * 3
        args += list(rope_tabs)
    kcols = MLA_HEADS * MLA_QK_PAD
    vcols = MLA_HEADS * 2 * MLA_V
    out_specs = [pl.BlockSpec((tm, kcols), lambda i: (i, 0)), pl.BlockSpec((tm, vcols), lambda i: (i, 0))]
    out_shape = [jax.ShapeDtypeStruct((r, kcols), BF16), jax.ShapeDtypeStruct((r, vcols), BF16)]
    if want_q:
        out_specs.insert(0, pl.BlockSpec((tm, kcols), lambda i: (i, 0)))
        out_shape.insert(0, jax.ShapeDtypeStruct((r, kcols), BF16))
    return pl.pallas_call(
        functools.partial(_mla_proj_kernel, rope=rope, want_q=want_q,
                          scale=(MLA_NOPE + MLA_ROPE) ** -0.5 * LOG2E),
        grid=(r // tm,),
        in_specs=in_specs,
        out_specs=out_specs,
        out_shape=out_shape,
        compiler_params=_params("arbitrary"),
        name="mla_proj",
    )(*args)


def _attn_kernel(*refs, n_seg, band, use_sink, tq, seq, hp, group, dq, dv, tk):
    refs = list(refs)
    if use_sink:
        sink_ref = refs.pop(0)
    q_ref = refs.pop(0)
    o_ref = refs.pop()
    dvx = 2 * dv
    if band:
        wk = tq + 2 * WINDOW
        q0 = pl.program_id(2) * tq
        start = pl.multiple_of(jnp.clip(q0 - WINDOW, 0, seq - wk), LANES)
        dist = (lax.broadcasted_iota(I32, (tq, wk), 1) - lax.broadcasted_iota(I32, (tq, wk), 0)) + (start - q0)
        valid = jnp.abs(dist) <= WINDOW
    for j in range(hp):
        kv = j // group
        q = q_ref[:, j * dq:(j + 1) * dq]
        m = None
        acc = None
        for s_i in range(n_seg):
            k_ref, v_ref = refs[2 * s_i], refs[2 * s_i + 1]
            slen = k_ref.shape[0]
            masked = band and s_i == n_seg - 1
            chunks = [(start, wk)] if masked else [(c0, min(tk, slen - c0)) for c0 in range(0, slen, tk)]
            for c0, cl in chunks:
                s = _dot_nt(q, k_ref[pl.ds(c0, cl), kv * dq:(kv + 1) * dq])
                if masked:
                    s = jnp.where(valid, s, -jnp.inf)
                v = v_ref[pl.ds(c0, cl), kv * dvx:(kv + 1) * dvx]
                mc = s.max(axis=-1, keepdims=True)
                if m is None:
                    m = mc
                    acc = _dot(jnp.exp2(s - m).astype(BF16), v)
                else:
                    m_new = jnp.maximum(m, mc)
                    acc = jnp.exp2(m - m_new) * acc + _dot(jnp.exp2(s - m_new).astype(BF16), v)
                    m = m_new
        den = acc[:, dv:]
        if use_sink:
            sink = sink_ref[pl.program_id(1) * hp + j] * LOG2E
            den = den + jnp.exp2(sink - m)
        o_ref[:, j * dv:(j + 1) * dv] = (acc[:, :dv] / den).astype(o_ref.dtype)


def _attention(q_arr, q_col0, segs, sink, *, batch, seq, n_heads, group, dq, dv, band, tq_cap, hp, tk=512):
    tq = _tile(seq, tq_cap)
    nq = seq // tq
    use_sink = sink is not None
    if hp <= group:
        assert group % hp == 0
        kvp, kv_of, step_group = 1, (lambda hg: hg // (group // hp)), hp
    else:
        assert hp % group == 0
        kvp, kv_of, step_group = hp // group, (lambda hg: hg), group
    qw, kw, vw = hp * dq, kvp * dq, kvp * 2 * dv
    assert q_col0 % qw == 0
    in_specs, args = [], []
    if use_sink:
        in_specs.append(pl.BlockSpec(memory_space=pltpu.SMEM))
        args.append(sink)
    in_specs.append(pl.BlockSpec((tq, qw), lambda b, hg, i: (b * nq + i, q_col0 // qw + hg)))
    args.append(q_arr)
    for k_arr, k_col0, v_arr, v_col0, slen in segs:
        assert k_col0 % kw == 0 and v_col0 % vw == 0
        in_specs.append(pl.BlockSpec((slen, kw), lambda b, hg, i, c=k_col0 // kw: (b, c + kv_of(hg))))
        in_specs.append(pl.BlockSpec((slen, vw), lambda b, hg, i, c=v_col0 // vw: (b, c + kv_of(hg))))
        args += [k_arr, v_arr]
    return pl.pallas_call(
        functools.partial(_attn_kernel, n_seg=len(segs), band=band, use_sink=use_sink, tq=tq, seq=seq, hp=hp,
                          group=step_group, dq=dq, dv=dv, tk=tk),
        grid=(batch, n_heads // hp, nq),
        in_specs=in_specs,
        out_specs=pl.BlockSpec((tq, hp * dv), lambda b, hg, i: (b * nq + i, hg)),
        out_shape=jax.ShapeDtypeStruct((batch * seq, n_heads * dv), BF16),
        compiler_params=_params("arbitrary", "arbitrary", "arbitrary"),
        name="attention",
    )(*args)


def _outproj_kernel(*refs, n_in):
    o_refs = refs[:n_in]
    (w_ref, h_ref, g1_ref, gate_ref, g2_ref, sh_ref, sc_ref, wr_ref, hn_ref, xf_ref, aff_ref) = refs[n_in:]
    tm = h_ref.shape[0]
    sub = min(tm, OUTPROJ_SUB)
    for r in range(0, tm, sub):
        rows = slice(r, r + sub)
        y = None
        k0 = 0
        for o_ref in o_refs:
            kk = o_ref.shape[1]
            part = _dot(o_ref[rows, :], w_ref[k0:k0 + kk, :])
            y = part if y is None else y + part
            k0 += kk
        hn = h_ref[rows, :] + gate_ref[...] * _rms(y, g1_ref[...])
        hn_ref[rows, :] = hn
        xf = _rms(hn, g2_ref[...]) * (1.0 + sc_ref[...]) + sh_ref[...]
        xf_ref[rows, :] = xf.astype(BF16)
        logits = lax.dot_general(wr_ref[...], xf, (((1,), (1,)), ((), ())),
                                 precision=lax.Precision.HIGHEST, preferred_element_type=F32)
        e = jnp.exp(logits - logits.max(axis=0, keepdims=True))
        aff_ref[:, rows] = e / e.sum(axis=0, keepdims=True)


def _outproj(o_list, w_bf, h, g1, gate, g2, shift, scale_m, wr_t, row_fn, seq):
    r, d = h.shape
    tm = _tile(seq, 512)
    n_e = wr_t.shape[0]
    in_specs = [pl.BlockSpec((tm, o.shape[1]), lambda i: (i, 0)) for o in o_list]
    in_specs += [
        _const_spec(w_bf.shape),
        pl.BlockSpec((tm, d), lambda i: (i, 0)),
        _const_spec((1, d)),
        _mod_spec(d, lambda i: row_fn(i * tm)),
        _const_spec((1, d)),
        _mod_spec(d, lambda i: row_fn(i * tm)),
        _mod_spec(d, lambda i: row_fn(i * tm)),
        _const_spec((n_e, d)),
    ]
    return pl.pallas_call(
        functools.partial(_outproj_kernel, n_in=len(o_list)),
        grid=(r // tm,),
        in_specs=in_specs,
        out_specs=[
            pl.BlockSpec((tm, d), lambda i: (i, 0)),
            pl.BlockSpec((tm, d), lambda i: (i, 0)),
            pl.BlockSpec((n_e, tm), lambda i: (0, i)),
        ],
        out_shape=[
            jax.ShapeDtypeStruct((r, d), F32),
            jax.ShapeDtypeStruct((r, d), BF16),
            jax.ShapeDtypeStruct((n_e, r), F32),
        ],
        compiler_params=_params("arbitrary"),
        name="outproj",
    )(*o_list, w_bf, h, g1, gate, g2, shift, scale_m, wr_t)


def _route_kernel(aff_ref, posm_ref, pose_ref, gsel_ref, *, cap):
    n_e, n = aff_ref.shape
    bits = pltpu.bitcast(aff_ref[...], I32)
    capf = float(cap)

    def search(_, carry):
        lo, hi = carry
        mid = lo + ((hi - lo + 1) >> 1)
        cnt = jnp.sum(jnp.where(bits >= mid, 1.0, 0.0), axis=1, keepdims=True)
        ok = cnt >= capf
        return jnp.where(ok, mid, lo), jnp.where(ok, hi, mid - 1)

    lo0 = jnp.zeros((n_e, 1), I32)
    hi0 = jnp.full((n_e, 1), 0x7F800000, I32)
    thr, _ = lax.fori_loop(0, 31, search, (lo0, hi0))
    need = capf - jnp.sum(jnp.where(bits > thr, 1.0, 0.0), axis=1, keepdims=True)
    upper = jnp.where(lax.broadcasted_iota(I32, (LANES, LANES), 0) < lax.broadcasted_iota(I32, (LANES, LANES), 1),
                      1.0, 0.0).astype(BF16)
    run_eq = jnp.zeros((n_e, 1), F32)
    run_sel = jnp.zeros((n_e, 1), F32)
    for j in range(n // LANES):
        sl = slice(j * LANES, (j + 1) * LANES)
        a = aff_ref[:, sl]
        bb = pltpu.bitcast(a, I32)
        eq = jnp.where(bb == thr, 1.0, 0.0)
        rank = _dot(eq.astype(BF16), upper) + run_eq
        run_eq = run_eq + eq.sum(axis=1, keepdims=True)
        sel = (bb > thr) | ((bb == thr) & (rank < need))
        self_f = jnp.where(sel, 1.0, 0.0)
        pos = _dot(self_f.astype(BF16), upper) + run_sel
        run_sel = run_sel + self_f.sum(axis=1, keepdims=True)
        pos_i = pos.astype(I32)
        pose_ref[0, :, sl] = pos_i
        posm_ref[0, :, sl] = jnp.where(sel, pos_i, -1)
        gsel_ref[0, :, sl] = jnp.where(sel, a, 0.0)


def _route(aff, batch, n, cap):
    n_e = aff.shape[0]
    spec = pl.BlockSpec((1, n_e, n), lambda b: (b, 0, 0))
    return pl.pallas_call(
        functools.partial(_route_kernel, cap=cap),
        grid=(batch,),
        in_specs=[pl.BlockSpec((n_e, n), lambda b: (0, b))],
        out_specs=[spec, spec, spec],
        out_shape=[
            jax.ShapeDtypeStruct((batch, n_e, n), I32),
            jax.ShapeDtypeStruct((batch, n_e, n), I32),
            jax.ShapeDtypeStruct((batch, n_e, n), F32),
        ],
        compiler_params=_params("arbitrary"),
        name="route",
    )(aff)


def _gather_kernel(cs_ref, posm_ref, gsel_ref, x_ref, xg_ref, gs_ref, acc_ref, gacc_ref, *, win, tc, n_chunks,
                   n_e, cap):
    b, e = pl.program_id(0), pl.program_id(1)
    base = (b * n_e + e) * (n_chunks + 1)
    acc_ref[...] = jnp.zeros_like(acc_ref)
    gacc_ref[...] = jnp.zeros_like(gacc_ref)
    row = lax.broadcasted_iota(I32, (win, tc), 0)

    def fill(c, a0, w):
        first = a0 + w * win
        start = pl.multiple_of(jnp.minimum(first, cap - win), F32_SUBLANES)
        slot = start + row
        hit = (posm_ref[0, 0, c:c + 1, :] == slot) & (slot >= first)
        onehot = jnp.where(hit, 1.0, 0.0).astype(BF16)
        acc_ref[pl.ds(start, win), :] += _dot(onehot, x_ref[c * tc:(c + 1) * tc, :])
        gacc_ref[pl.ds(start, win), :] += jnp.where(hit, gsel_ref[0, 0, c:c + 1, :], 0.0).sum(
            axis=1, keepdims=True)

    starts = [(cs_ref[base + c] // F32_SUBLANES) * F32_SUBLANES for c in range(n_chunks)]
    for c in range(n_chunks):
        fill(c, starts[c], 0)
    for c in range(n_chunks):
        n_win = (cs_ref[base + c + 1] - starts[c] + win - 1) // win

        def more(w, carry, c=c):
            fill(c, starts[c], w)
            return carry

        lax.fori_loop(1, n_win, more, 0)

    xg_ref[0] = acc_ref[...].astype(BF16)
    gs_ref[0] = gacc_ref[...]


def _gather(xf, posm, pose, gsel, batch, n, cap):
    d = xf.shape[1]
    n_e = posm.shape[1]
    tc = _tile(n, 256)
    win = min(cap, LANES)
    n_chunks = n // tc
    cstart = jnp.concatenate([pose[:, :, ::tc], jnp.full((batch, n_e, 1), cap, I32)], axis=-1).reshape(-1)
    posm4 = posm.reshape(batch, n_e, n_chunks, tc)
    gsel4 = gsel.reshape(batch, n_e, n_chunks, tc)
    grid_spec = pltpu.PrefetchScalarGridSpec(
        num_scalar_prefetch=1,
        grid=(batch, n_e),
        in_specs=[
            pl.BlockSpec((1, 1, n_chunks, tc), lambda b, e, cs: (b, e, 0, 0)),
            pl.BlockSpec((1, 1, n_chunks, tc), lambda b, e, cs: (b, e, 0, 0)),
            pl.BlockSpec((n, d), lambda b, e, cs: (b, 0)),
        ],
        out_specs=[
            pl.BlockSpec((1, cap, d), lambda b, e, cs: (e, b, 0)),
            pl.BlockSpec((1, cap, 1), lambda b, e, cs: (e, b, 0)),
        ],
        scratch_shapes=[pltpu.VMEM((cap, d), F32), pltpu.VMEM((cap, 1), F32)],
    )
    return pl.pallas_call(
        functools.partial(_gather_kernel, win=win, tc=tc, n_chunks=n_chunks, n_e=n_e, cap=cap),
        grid_spec=grid_spec,
        out_shape=[
            jax.ShapeDtypeStruct((n_e, batch * cap, d), BF16),
            jax.ShapeDtypeStruct((n_e, batch * cap, 1), F32),
        ],
        compiler_params=_params("arbitrary", "arbitrary"),
        name="moe_gather",
    )(cstart, posm4, gsel4, xf)


def _ffn_kernel(*refs, with_ctx):
    if with_ctx:
        xl_ref, xc_ref, wg_ref, wu_ref, wd_ref, gl_ref, gc_ref, yl_ref, yc_ref, accl_ref, accc_ref = refs
    else:
        xl_ref, wg_ref, wu_ref, wd_ref, gl_ref, yl_ref, accl_ref = refs
    i, f = pl.program_id(1), pl.program_id(2)
    last_f = pl.num_programs(2) - 1
    tf = wg_ref.shape[2]
    fw = min(tf, FFN_SUB)

    def run(x_ref, gs_ref, y_ref, acc_ref):
        @pl.when(f == 0)
        def _():
            acc_ref[...] = jnp.zeros_like(acc_ref)

        x = x_ref[0]
        hid = []
        for c0 in range(0, tf, fw):
            hg = _dot(x, wg_ref[0, :, c0:c0 + fw].astype(BF16))
            hu = _dot(x, wu_ref[0, :, c0:c0 + fw].astype(BF16))
            hid.append((_silu(hg) * hu).astype(BF16))
        acc_ref[...] += _dot(jnp.concatenate(hid, axis=1), wd_ref[0].astype(BF16))

        @pl.when(f == last_f)
        def _():
            y_ref[0] = (acc_ref[...] * gs_ref[0]).astype(BF16)

    run(xl_ref, gl_ref, yl_ref, accl_ref)
    if with_ctx:
        @pl.when(i == pl.num_programs(1) - 1)
        def _():
            run(xc_ref, gc_ref, yc_ref, accc_ref)


def _expert_ffn(layer, w_gate, w_up, w_down, xg, gslot, xg_c=None, gslot_c=None):
    n_e, m, d = xg.shape
    ff = w_gate.shape[3]
    tm = _tile(m, 1024)
    tf = _tile(ff, 512)
    with_ctx = xg_c is not None
    x_spec = pl.BlockSpec((1, tm, d), lambda e, i, f: (e, i, 0))
    g_spec = pl.BlockSpec((1, tm, 1), lambda e, i, f: (e, i, 0))
    w_specs = [
        pl.BlockSpec((None, 1, d, tf), lambda e, i, f: (layer, e, 0, f)),
        pl.BlockSpec((None, 1, d, tf), lambda e, i, f: (layer, e, 0, f)),
        pl.BlockSpec((None, 1, tf, d), lambda e, i, f: (layer, e, f, 0)),
    ]
    out_specs = [x_spec]
    out_shape = [jax.ShapeDtypeStruct((n_e, m, d), BF16)]
    scratch = [pltpu.VMEM((tm, d), F32)]
    if with_ctx:
        mc = xg_c.shape[1]
        xc_spec = pl.BlockSpec((1, mc, d), lambda e, i, f: (e, 0, 0))
        gc_spec = pl.BlockSpec((1, mc, 1), lambda e, i, f: (e, 0, 0))
        in_specs = [x_spec, xc_spec] + w_specs + [g_spec, gc_spec]
        args = [xg, xg_c, w_gate, w_up, w_down, gslot, gslot_c]
        out_specs.append(xc_spec)
        out_shape.append(jax.ShapeDtypeStruct((n_e, mc, d), BF16))
        scratch.append(pltpu.VMEM((mc, d), F32))
    else:
        in_specs = [x_spec] + w_specs + [g_spec]
        args = [xg, w_gate, w_up, w_down, gslot]
    return pl.pallas_call(
        functools.partial(_ffn_kernel, with_ctx=with_ctx),
        grid=(n_e, m // tm, ff // tf),
        in_specs=in_specs,
        out_specs=out_specs,
        out_shape=out_shape,
        scratch_shapes=scratch,
        compiler_params=_params("arbitrary", "arbitrary", "arbitrary"),
        name="expert_ffn",
    )(*args)


def _combine_kernel(cs_ref, posm_ref, y_ref, h_ref, g_ref, gate_ref, o_ref, *, tt, win, cap, n_e, n_tiles):
    b, i = pl.program_id(0), pl.program_id(1)
    acc = None
    for e in range(n_e):
        if cap <= win:
            a0 = 0
            yw = y_ref[e]
        else:
            s0 = cs_ref[(b * n_e + e) * n_tiles + i]
            a0 = jnp.minimum((s0 // BF16_SUBLANES) * BF16_SUBLANES, cap - win)
            a0 = pl.multiple_of(a0, BF16_SUBLANES)
            yw = y_ref[e, pl.ds(a0, win), :]
        wn = yw.shape[0]
        slot = a0 + lax.broadcasted_iota(I32, (wn, tt), 0)
        onehot = jnp.where(posm_ref[0, e:e + 1, :] == slot, 1.0, 0.0).astype(BF16)
        part = lax.dot_general(onehot, yw, (((0,), (0,)), ((), ())), preferred_element_type=F32)
        acc = part if acc is None else acc + part
    o_ref[...] = h_ref[...] + gate_ref[...] * _rms(acc, g_ref[...])


def _combine(y, posm, pose, h, g, gate, row_fn, batch, n, cap):
    n_e = y.shape[0]
    d = y.shape[2]
    tt = LANES
    win = tt + BF16_SUBLANES
    n_tiles = n // tt
    cstart = pose[:, :, ::tt].reshape(-1)
    grid_spec = pltpu.PrefetchScalarGridSpec(
        num_scalar_prefetch=1,
        grid=(batch, n_tiles),
        in_specs=[
            pl.BlockSpec((1, n_e, tt), lambda b, i, cs: (b, 0, i)),
            pl.BlockSpec((n_e, cap, d), lambda b, i, cs: (0, b, 0), pipeline_mode=pl.Buffered(1)),
            pl.BlockSpec((tt, d), lambda b, i, cs: (b * n_tiles + i, 0)),
            pl.BlockSpec((1, d), lambda b, i, cs: (0, 0)),
            pl.BlockSpec((None, 1, d), lambda b, i, cs: (row_fn(b * n), 0, 0)),
        ],
        out_specs=pl.BlockSpec((tt, d), lambda b, i, cs: (b * n_tiles + i, 0)),
    )
    return pl.pallas_call(
        functools.partial(_combine_kernel, tt=tt, win=win, cap=cap, n_e=n_e, n_tiles=n_tiles),
        grid_spec=grid_spec,
        out_shape=jax.ShapeDtypeStruct(h.shape, F32),
        compiler_params=_params("arbitrary", "arbitrary"),
        name="moe_combine",
    )(cstart, posm, y, h, g, gate)


def _moe_dispatch(xf, aff, batch, n):
    cap = CAPACITY_FACTOR * n // N_EXPERTS
    posm, pose, gsel = _route(aff, batch, n, cap)
    xg, gslot = _gather(xf, posm, pose, gsel, batch, n, cap)
    return xg, gslot, (posm, pose, cap)


def _rope_angles(seq, rot_dim):
    n_rows = seq // GRID_W
    row = jnp.repeat(jnp.arange(n_rows), GRID_W)
    col = jnp.tile(jnp.arange(GRID_W), n_rows)
    n_freq = rot_dim // 4
    inv = ROPE_BASE ** (-jnp.arange(n_freq, dtype=F32) / n_freq)
    ang = jnp.concatenate([row[:, None] * inv, col[:, None] * inv], axis=-1)
    return jnp.cos(ang), jnp.sin(ang)


def kernel(x, c, ctx, c_ctx, mod_w, mod_b, norm_g, ev_w_in, ev_w_out, ev_sink, ev_qk_norm, od_w_in, od_q_norm,
           od_kv_norm, od_w_uq, od_w_ukv, od_w_out, router_w, exp_w_gate, exp_w_up, exp_w_down):
    batch, seq, d = x.shape
    n_ctx = ctx.shape[1]
    depth = mod_w.shape[0]
    assert batch < MOD_ROWS and seq % LANES == 0 and n_ctx % LANES == 0 and seq >= 4 * WINDOW

    cs = jnp.concatenate([c, c_ctx[None, :], jnp.zeros((MOD_ROWS - batch - 1, d), F32)], axis=0)
    mods = _modulation(cs, mod_w, mod_b).reshape(depth, MOD_ROWS, N_MOD, 1, d)

    def lat_row(r):
        return r // seq

    def ctx_row(r):
        return batch

    cos_h, sin_h = _rope_angles(seq, HEAD_DIM)
    rope_even = (jnp.concatenate([cos_h, cos_h], axis=-1), jnp.concatenate([-sin_h, sin_h], axis=-1))
    cos_m, sin_m = _rope_angles(seq, MLA_ROPE)
    half = MLA_ROPE // 2
    zeros = lambda w: jnp.zeros((seq, w), F32)
    rope_mla = (
        jnp.concatenate([cos_m, cos_m, zeros(LANES - MLA_ROPE)], axis=-1),
        jnp.concatenate([-sin_m, zeros(LANES - half)], axis=-1),
        jnp.concatenate([zeros(half), sin_m, zeros(LANES - MLA_ROPE)], axis=-1),
    )

    h_lat = x.reshape(batch * seq, d)
    h_ctx = ctx.reshape(batch * n_ctx, d)
    for layer in range(depth):
        with_ctx = layer < depth - 1
        i = layer // 2
        g = norm_g[layer].reshape(4, 1, d)
        m = [mods[layer, :, k] for k in range(N_MOD)]
        wr_t = router_w[layer].T
        if layer % 2 == 0:
            w_in = ev_w_in[i].astype(BF16)
            w_out = ev_w_out[i].astype(BF16)
            qkv_l = _even_inproj(h_lat, g[0], m[0], m[1], w_in, ev_qk_norm[i], rope_even, lat_row, seq)
            qkv_c = _even_inproj(h_ctx, g[0], m[0], m[1], w_in, ev_qk_norm[i], None, ctx_row, n_ctx)
            ga, gb = A_HEADS // A_KV_HEADS, B_HEADS // B_KV_HEADS
            hd = HEAD_DIM
            a_kw = dict(batch=batch, n_heads=A_HEADS, group=ga, dq=hd, dv=hd)
            b_kw = dict(batch=batch, n_heads=B_HEADS, group=gb, dq=hd, dv=hd, band=False)
            a_ctx = (qkv_c, AK0 * hd, qkv_c, AV0 * hd, n_ctx)
            b_ctx = (qkv_c, BK0 * hd, qkv_c, BV0 * hd, n_ctx)
            o_a = _attention(qkv_l, AQ0 * hd, [a_ctx, (qkv_l, AK0 * hd, qkv_l, AV0 * hd, seq)], ev_sink[i],
                             seq=seq, band=True, tq_cap=256, hp=4, **a_kw)
            o_b = _attention(qkv_l, BQ0 * hd, [b_ctx, (qkv_l, BK0 * hd, qkv_l, BV0 * hd, seq)], None,
                             seq=seq, tq_cap=512, hp=4, **b_kw)
            o_lat = [o_a, o_b]
            if with_ctx:
                o_ac = _attention(qkv_c, AQ0 * hd, [a_ctx], ev_sink[i], seq=n_ctx, band=False, tq_cap=512,
                                  hp=4, **a_kw)
                o_bc = _attention(qkv_c, BQ0 * hd, [b_ctx], None, seq=n_ctx, tq_cap=512, hp=4, **b_kw)
                o_ctx = [o_ac, o_bc]
        else:
            w_in = od_w_in[i]
            w_qkv = w_in[:, :MLA_Q_RANK + MLA_KV_RANK].astype(BF16)
            w_kr = jnp.pad(w_in[:, MLA_Q_RANK + MLA_KV_RANK:], ((0, 0), (0, LANES - MLA_ROPE))).astype(BF16)
            w_uq = od_w_uq[i].reshape(MLA_Q_RANK, MLA_HEADS, MLA_NOPE + MLA_ROPE)
            w_uq = jnp.pad(w_uq, ((0, 0), (0, 0), (0, MLA_QK_PAD - MLA_NOPE - MLA_ROPE)))
            w_uq = w_uq.reshape(MLA_Q_RANK, MLA_HEADS * MLA_QK_PAD).astype(BF16)
            w_ukv = od_w_ukv[i].astype(BF16)
            w_out = od_w_out[i].astype(BF16)
            qn, kvn = od_q_norm[i].reshape(1, -1), od_kv_norm[i].reshape(1, -1)
            q_l, k_l, v_l = _mla_proj(h_lat, g[0], m[0], m[1], w_qkv, w_kr, qn, kvn, w_uq, w_ukv, rope_mla,
                                      lat_row, seq, True)
            proj_c = _mla_proj(h_ctx, g[0], m[0], m[1], w_qkv, w_kr, qn, kvn, w_uq, w_ukv, None, ctx_row,
                               n_ctx, with_ctx)
            k_c, v_c = proj_c[-2], proj_c[-1]
            m_kw = dict(batch=batch, n_heads=MLA_HEADS, group=1, dq=MLA_QK_PAD, dv=MLA_V, band=False, tq_cap=512,
                        hp=4, tk=1024)
            o_lat = [_attention(q_l, 0, [(k_c, 0, v_c, 0, n_ctx), (k_l, 0, v_l, 0, seq)], None, seq=seq, **m_kw)]
            if with_ctx:
                o_ctx = [_attention(proj_c[0], 0, [(k_c, 0, v_c, 0, n_ctx)], None, seq=n_ctx, **m_kw)]
        ew = (layer, exp_w_gate, exp_w_up, exp_w_down)
        h_lat, xf, aff = _outproj(o_lat, w_out, h_lat, g[1], m[2], g[2], m[3], m[4], wr_t, lat_row, seq)
        xg, gslot, (posm, pose, cap) = _moe_dispatch(xf, aff, batch, seq)
        if with_ctx:
            h_ctx, xf_c, aff_c = _outproj(o_ctx, w_out, h_ctx, g[1], m[2], g[2], m[3], m[4], wr_t, ctx_row,
                                          n_ctx)
            xg_c, gslot_c, (posm_c, pose_c, cap_c) = _moe_dispatch(xf_c, aff_c, batch, n_ctx)
            y, y_c = _expert_ffn(*ew, xg, gslot, xg_c, gslot_c)
            h_ctx = _combine(y_c, posm_c, pose_c, h_ctx, g[3], m[5], ctx_row, batch, n_ctx, cap_c)
        else:
            (y,) = _expert_ffn(*ew, xg, gslot)
        h_lat = _combine(y, posm, pose, h_lat, g[3], m[5], lat_row, batch, seq, cap)
    return h_lat.reshape(batch, seq, d)
```

```python
import functools

import jax
import jax.numpy as jnp
from jax import lax
from jax.experimental import pallas as pl
from jax.experimental.pallas import tpu as pltpu

F32 = jnp.float32
BF16 = jnp.bfloat16
I32 = jnp.int32

EPS = 1e-6
GRID_W = 64
WINDOW = 128
ROPE_BASE = 10000.0
HEAD_DIM = 128
A_HEADS = 8
A_KV_HEADS = 2
B_HEADS = 8
B_KV_HEADS = 2
MLA_HEADS = 16
MLA_Q_RANK = 512
MLA_KV_RANK = 512
MLA_NOPE = 128
MLA_ROPE = 64
MLA_V = 128
MLA_QK_PAD = 256
N_EXPERTS = 16
CAPACITY_FACTOR = 2
N_MOD = 6
MOD_ROWS = 8

LANES = 128
BF16_SUBLANES = 16
F32_SUBLANES = 8
VMEM_LIMIT_BYTES = 60 * 1024 * 1024
FFN_SUB = 256
ROUTE_SEARCH_STEPS = 64

LOG2E = 1.4426950408889634

EVEN_IN_COLS = (A_HEADS + 2 * A_KV_HEADS + B_HEADS + 2 * B_KV_HEADS) * HEAD_DIM
AQ0 = 0
BQ0 = AQ0 + A_HEADS
AK0 = BQ0 + B_HEADS
BK0 = AK0 + A_KV_HEADS
AV0 = BK0 + B_KV_HEADS
BV0 = AV0 + 2 * A_KV_HEADS
EVEN_SLOTS = BV0 + 2 * B_KV_HEADS
EVEN_COLS = EVEN_SLOTS * HEAD_DIM
_W_AK0 = A_HEADS
_W_AV0 = _W_AK0 + A_KV_HEADS
_W_BQ0 = _W_AV0 + A_KV_HEADS
_W_BK0 = _W_BQ0 + B_HEADS
_W_BV0 = _W_BK0 + B_KV_HEADS


def _even_head(wh):
    if wh < _W_AK0:
        return AQ0 + wh, "aq"
    if wh < _W_AV0:
        return AK0 + wh - _W_AK0, "ak"
    if wh < _W_BQ0:
        return AV0 + 2 * (wh - _W_AV0), "av"
    if wh < _W_BK0:
        return BQ0 + wh - _W_BQ0, "bq"
    if wh < _W_BV0:
        return BK0 + wh - _W_BK0, "bk"
    return BV0 + 2 * (wh - _W_BV0), "bv"


def _params(*sem):
    return pltpu.CompilerParams(dimension_semantics=sem, vmem_limit_bytes=VMEM_LIMIT_BYTES)


def _tile(n, cap):
    t = min(n, cap)
    while n % t:
        t -= 1
    return t


def _const_spec(shape):
    nd = len(shape)
    return pl.BlockSpec(shape, lambda *_: (0,) * nd)


def _rms(x, g):
    return x * lax.rsqrt(jnp.mean(x * x, axis=-1, keepdims=True) + EPS) * g


def _silu(x):
    return x / (1.0 + jnp.exp(-x))


def _dot(a, b):
    return jnp.dot(a, b, preferred_element_type=F32)


def _dot_nt(a, b):
    return lax.dot_general(a, b, (((1,), (1,)), ((), ())), preferred_element_type=F32)


def _split_bf16(x):
    hi = x.astype(BF16)
    return hi, (x - hi.astype(F32)).astype(BF16)


def _mod_kernel(cs_ref, w_ref, b_ref, o_ref):
    s_hi, s_lo = _split_bf16(_silu(cs_ref[...]))
    w_hi, w_lo = _split_bf16(w_ref[0])
    o_ref[0] = _dot(s_hi, w_hi) + (_dot(s_hi, w_lo) + _dot(s_lo, w_hi)) + b_ref[0]


def _modulation(cs, mod_w, mod_b):
    depth, d, n6 = mod_w.shape
    tn = _tile(n6, 1024)
    return pl.pallas_call(
        _mod_kernel,
        grid=(depth, n6 // tn),
        in_specs=[
            _const_spec((MOD_ROWS, d)),
            pl.BlockSpec((1, d, tn), lambda l, j: (l, 0, j)),
            pl.BlockSpec((1, 1, tn), lambda l, j: (l, 0, j)),
        ],
        out_specs=pl.BlockSpec((1, MOD_ROWS, tn), lambda l, j: (l, 0, j)),
        out_shape=jax.ShapeDtypeStruct((depth, MOD_ROWS, n6), F32),
        compiler_params=_params("arbitrary", "arbitrary"),
        name="modulation",
    )(cs, mod_w, mod_b.reshape(depth, 1, n6))


def _mod_spec(d, row_fn):
    return pl.BlockSpec((None, 1, d), lambda i: (row_fn(i), 0, 0))


def _even_inproj_kernel(*refs, rope, scale):
    if rope:
        x_ref, g_ref, sh_ref, sc_ref, w_ref, qkg_ref, cos_ref, sin_ref, o_ref = refs
        cos, sin = cos_ref[...], sin_ref[...]
    else:
        x_ref, g_ref, sh_ref, sc_ref, w_ref, qkg_ref, o_ref = refs
    a = _rms(x_ref[...], g_ref[...]) * (1.0 + sc_ref[...]) + sh_ref[...]
    ab = a.astype(BF16)
    ones = jnp.ones((x_ref.shape[0], HEAD_DIM), BF16)
    for j in range(EVEN_IN_COLS // (2 * HEAD_DIM)):
        acc = _dot(ab, w_ref[:, j * 2 * HEAD_DIM:(j + 1) * 2 * HEAD_DIM])
        for hh in range(2):
            slot, kind = _even_head(2 * j + hh)
            v = acc[:, hh * HEAD_DIM:(hh + 1) * HEAD_DIM]
            if kind == "bq":
                v = _rms(v, qkg_ref[0:1, :])
            elif kind == "bk":
                v = _rms(v, qkg_ref[1:2, :])
            if rope and kind[1] != "v":
                v = v * cos + pltpu.roll(v, HEAD_DIM // 2, 1) * sin
            if kind[1] == "q":
                v = v * scale
            o_ref[:, slot * HEAD_DIM:(slot + 1) * HEAD_DIM] = v.astype(BF16)
            if kind[1] == "v":
                o_ref[:, (slot + 1) * HEAD_DIM:(slot + 2) * HEAD_DIM] = ones


def _even_inproj(h, g, shift, scale_m, w_bf, qk_gain, rope_tabs, row_fn, seq):
    r, d = h.shape
    tm = _tile(seq, 512)
    rope = rope_tabs is not None
    in_specs = [
        pl.BlockSpec((tm, d), lambda i: (i, 0)),
        _const_spec((1, d)),
        _mod_spec(d, lambda i: row_fn(i * tm)),
        _mod_spec(d, lambda i: row_fn(i * tm)),
        _const_spec((d, EVEN_IN_COLS)),
        _const_spec((2, HEAD_DIM)),
    ]
    args = [h, g, shift, scale_m, w_bf, qk_gain]
    if rope:
        nt = seq // tm
        in_specs += [pl.BlockSpec((tm, HEAD_DIM), lambda i: (i % nt, 0))] * 2
        args += list(rope_tabs)
    return pl.pallas_call(
        functools.partial(_even_inproj_kernel, rope=rope, scale=HEAD_DIM ** -0.5 * LOG2E),
        grid=(r // tm,),
        in_specs=in_specs,
        out_specs=pl.BlockSpec((tm, EVEN_COLS), lambda i: (i, 0)),
        out_shape=jax.ShapeDtypeStruct((r, EVEN_COLS), BF16),
        compiler_params=_params("arbitrary"),
        name="even_inproj",
    )(*args)


def _rope_pad(v, c, s1, s2):
    return v * c + pltpu.roll(v, LANES - MLA_ROPE // 2, 1) * s1 + pltpu.roll(v, MLA_ROPE // 2, 1) * s2


def _mla_proj_kernel(*refs, rope, want_q, scale):
    refs = list(refs)
    x_ref, g_ref, sh_ref, sc_ref, win_ref, wkr_ref, qn_ref, kvn_ref = refs[:8]
    refs = refs[8:]
    if want_q:
        wuq_ref = refs.pop(0)
    wukv_ref = refs.pop(0)
    if rope:
        c, s1, s2 = refs[0][...], refs[1][...], refs[2][...]
        refs = refs[3:]
    if want_q:
        q_ref = refs.pop(0)
    k_ref, v_ref = refs
    a = _rms(x_ref[...], g_ref[...]) * (1.0 + sc_ref[...]) + sh_ref[...]
    ab = a.astype(BF16)
    low = _dot(ab, win_ref[...])
    kr = _dot(ab, wkr_ref[...])
    if rope:
        kr = _rope_pad(kr, c, s1, s2)
    krb = kr.astype(BF16)
    if want_q:
        cq = _rms(low[:, :MLA_Q_RANK], qn_ref[...]).astype(BF16)
        for h in range(MLA_HEADS):
            acc = _dot(cq, wuq_ref[:, h * MLA_QK_PAD:(h + 1) * MLA_QK_PAD])
            qr = acc[:, MLA_NOPE:]
            if rope:
                qr = _rope_pad(qr, c, s1, s2)
            q_ref[:, h * MLA_QK_PAD:h * MLA_QK_PAD + MLA_NOPE] = (acc[:, :MLA_NOPE] * scale).astype(BF16)
            q_ref[:, h * MLA_QK_PAD + MLA_NOPE:(h + 1) * MLA_QK_PAD] = (qr * scale).astype(BF16)
    ckv = _rms(low[:, MLA_Q_RANK:], kvn_ref[...]).astype(BF16)
    hw = MLA_NOPE + MLA_V
    for h in range(MLA_HEADS):
        acc = _dot(ckv, wukv_ref[:, h * hw:(h + 1) * hw])
        k_ref[:, h * MLA_QK_PAD:h * MLA_QK_PAD + MLA_NOPE] = acc[:, :MLA_NOPE].astype(BF16)
        k_ref[:, h * MLA_QK_PAD + MLA_NOPE:(h + 1) * MLA_QK_PAD] = krb
        v_ref[:, 2 * h * MLA_V:(2 * h + 1) * MLA_V] = acc[:, MLA_NOPE:].astype(BF16)
        v_ref[:, (2 * h + 1) * MLA_V:(2 * h + 2) * MLA_V] = jnp.ones((x_ref.shape[0], MLA_V), BF16)


def _mla_proj(h, g, shift, scale_m, w_in_bf, w_kr_bf, q_norm, kv_norm, w_uq_bf, w_ukv_bf, rope_tabs,
              row_fn, seq, want_q):
    r, d = h.shape
    tm = _tile(seq, 512)
    rope = rope_tabs is not None
    in_specs = [
        pl.BlockSpec((tm, d), lambda i: (i, 0)),
        _const_spec((1, d)),
        _mod_spec(d, lambda i: row_fn(i * tm)),
        _mod_spec(d, lambda i: row_fn(i * tm)),
        _const_spec(w_in_bf.shape),
        _const_spec(w_kr_bf.shape),
        _const_spec((1, MLA_Q_RANK)),
        _const_spec((1, MLA_KV_RANK)),
    ]
    args = [h, g, shift, scale_m, w_in_bf, w_kr_bf, q_norm, kv_norm]
    if want_q:
        in_specs.append(_const_spec(w_uq_bf.shape))
        args.append(w_uq_bf)
    in_specs.append(_const_spec(w_ukv_bf.shape))
    args.append(w_ukv_bf)
    if rope:
        nt = seq // tm
        in_specs += [pl.BlockSpec((tm, LANES), lambda i: (i % nt, 0))] * 3
        args += list(rope_tabs)
    kcols = MLA_HEADS * MLA_QK_PAD
    vcols = MLA_HEADS * 2 * MLA_V
    out_specs = [pl.BlockSpec((tm, kcols), lambda i: (i, 0)), pl.BlockSpec((tm, vcols), lambda i: (i, 0))]
    out_shape = [jax.ShapeDtypeStruct((r, kcols), BF16), jax.ShapeDtypeStruct((r, vcols), BF16)]
    if want_q:
        out_specs.insert(0, pl.BlockSpec((tm, kcols), lambda i: (i, 0)))
        out_shape.insert(0, jax.ShapeDtypeStruct((r, kcols), BF16))
    return pl.pallas_call(
        functools.partial(_mla_proj_kernel, rope=rope, want_q=want_q,
                          scale=(MLA_NOPE + MLA_ROPE) ** -0.5 * LOG2E),
        grid=(r // tm,),
        in_specs=in_specs,
        out_specs=out_specs,
        out_shape=out_shape,
        compiler_params=_params("arbitrary"),
        name="mla_proj",
    )(*args)


def _attn_kernel(*refs, n_seg, band, use_sink, tq, seq, hp, group, dq, dv, tk):
    refs = list(refs)
    if use_sink:
        sink_ref = refs.pop(0)
    q_ref = refs.pop(0)
    o_ref = refs.pop()
    dvx = 2 * dv
    if band:
        wk = tq + 2 * WINDOW
        q0 = pl.program_id(2) * tq
        start = pl.multiple_of(jnp.clip(q0 - WINDOW, 0, seq - wk), LANES)
        dist = (lax.broadcasted_iota(I32, (tq, wk), 1) - lax.broadcasted_iota(I32, (tq, wk), 0)) + (start - q0)
        valid = jnp.abs(dist) <= WINDOW
    for j in range(hp):
        kv = j // group
        q = q_ref[:, j * dq:(j + 1) * dq]
        m = None
        acc = None
        for s_i in range(n_seg):
            k_ref, v_ref = refs[2 * s_i], refs[2 * s_i + 1]
            slen = k_ref.shape[0]
            masked = band and s_i == n_seg - 1
            chunks = [(start, wk)] if masked else [(c0, min(tk, slen - c0)) for c0 in range(0, slen, tk)]
            for c0, cl in chunks:
                s = _dot_nt(q, k_ref[pl.ds(c0, cl), kv * dq:(kv + 1) * dq])
                if masked:
                    s = jnp.where(valid, s, -jnp.inf)
                v = v_ref[pl.ds(c0, cl), kv * dvx:(kv + 1) * dvx]
                mc = s.max(axis=-1, keepdims=True)
                if m is None:
                    m = mc
                    acc = _dot(jnp.exp2(s - m).astype(BF16), v)
                else:
                    m_new = jnp.maximum(m, mc)
                    acc = jnp.exp2(m - m_new) * acc + _dot(jnp.exp2(s - m_new).astype(BF16), v)
                    m = m_new
        den = acc[:, dv:]
        if use_sink:
            sink = sink_ref[pl.program_id(1) * hp + j] * LOG2E
            den = den + jnp.exp2(sink - m)
        o_ref[:, j * dv:(j + 1) * dv] = (acc[:, :dv] / den).astype(o_ref.dtype)


def _attention(q_arr, q_col0, segs, sink, *, batch, seq, n_heads, group, dq, dv, band, tq_cap, hp, tk=512):
    tq = _tile(seq, tq_cap)
    nq = seq // tq
    use_sink = sink is not None
    if hp <= group:
        assert group % hp == 0
        kvp, kv_of, step_group = 1, (lambda hg: hg // (group // hp)), hp
    else:
        assert hp % group == 0
        kvp, kv_of, step_group = hp // group, (lambda hg: hg), group
    qw, kw, vw = hp * dq, kvp * dq, kvp * 2 * dv
    assert q_col0 % qw == 0
    in_specs, args = [], []
    if use_sink:
        in_specs.append(pl.BlockSpec(memory_space=pltpu.SMEM))
        args.append(sink)
    in_specs.append(pl.BlockSpec((tq, qw), lambda b, hg, i: (b * nq + i, q_col0 // qw + hg)))
    args.append(q_arr)
    for k_arr, k_col0, v_arr, v_col0, slen in segs:
        assert k_col0 % kw == 0 and v_col0 % vw == 0
        in_specs.append(pl.BlockSpec((slen, kw), lambda b, hg, i, c=k_col0 // kw: (b, c + kv_of(hg))))
        in_specs.append(pl.BlockSpec((slen, vw), lambda b, hg, i, c=v_col0 // vw: (b, c + kv_of(hg))))
        args += [k_arr, v_arr]
    return pl.pallas_call(
        functools.partial(_attn_kernel, n_seg=len(segs), band=band, use_sink=use_sink, tq=tq, seq=seq, hp=hp,
                          group=step_group, dq=dq, dv=dv, tk=tk),
        grid=(batch, n_heads // hp, nq),
        in_specs=in_specs,
        out_specs=pl.BlockSpec((tq, hp * dv), lambda b, hg, i: (b * nq + i, hg)),
        out_shape=jax.ShapeDtypeStruct((batch * seq, n_heads * dv), BF16),
        compiler_params=_params("arbitrary", "arbitrary", "arbitrary"),
        name="attention",
    )(*args)


def _outproj_kernel(*refs, n_in):
    o_refs = refs[:n_in]
    (w_ref, h_ref, g1_ref, gate_ref, g2_ref, sh_ref, sc_ref, wr_ref, hn_ref, xf_ref, aff_ref) = refs[n_in:]
    y = None
    k0 = 0
    for o_ref in o_refs:
        kk = o_ref.shape[1]
        part = _dot(o_ref[...], w_ref[k0:k0 + kk, :])
        y = part if y is None else y + part
        k0 += kk
    hn = h_ref[...] + gate_ref[...] * _rms(y, g1_ref[...])
    hn_ref[...] = hn
    xf = _rms(hn, g2_ref[...]) * (1.0 + sc_ref[...]) + sh_ref[...]
    xf_ref[...] = xf.astype(BF16)
    logits = lax.dot_general(wr_ref[...], xf, (((1,), (1,)), ((), ())),
                             precision=lax.Precision.HIGHEST, preferred_element_type=F32)
    e = jnp.exp(logits - logits.max(axis=0, keepdims=True))
    aff_ref[...] = e / e.sum(axis=0, keepdims=True)


def _outproj(o_list, w_bf, h, g1, gate, g2, shift, scale_m, wr_t, row_fn, seq):
    r, d = h.shape
    tm = _tile(seq, 512)
    n_e = wr_t.shape[0]
    in_specs = [pl.BlockSpec((tm, o.shape[1]), lambda i: (i, 0)) for o in o_list]
    in_specs += [
        _const_spec(w_bf.shape),
        pl.BlockSpec((tm, d), lambda i: (i, 0)),
        _const_spec((1, d)),
        _mod_spec(d, lambda i: row_fn(i * tm)),
        _const_spec((1, d)),
        _mod_spec(d, lambda i: row_fn(i * tm)),
        _mod_spec(d, lambda i: row_fn(i * tm)),
        _const_spec((n_e, d)),
    ]
    return pl.pallas_call(
        functools.partial(_outproj_kernel, n_in=len(o_list)),
        grid=(r // tm,),
        in_specs=in_specs,
        out_specs=[
            pl.BlockSpec((tm, d), lambda i: (i, 0)),
            pl.BlockSpec((tm, d), lambda i: (i, 0)),
            pl.BlockSpec((n_e, tm), lambda i: (0, i)),
        ],
        out_shape=[
            jax.ShapeDtypeStruct((r, d), F32),
            jax.ShapeDtypeStruct((r, d), BF16),
            jax.ShapeDtypeStruct((n_e, r), F32),
        ],
        compiler_params=_params("arbitrary"),
        name="outproj",
    )(*o_list, w_bf, h, g1, gate, g2, shift, scale_m, wr_t)


def _route_kernel(aff_ref, posm_ref, pose_ref, gsel_ref, *, cap):
    n_e, n = aff_ref.shape
    aff = aff_ref[...]
    capf = float(cap)
    floor = float(jnp.finfo(F32).tiny)

    def search(_, carry):
        lo, hi = carry
        mid = jnp.sqrt(jnp.maximum(lo, floor) * hi)
        cnt = jnp.sum(jnp.where(aff >= mid, 1.0, 0.0), axis=1, keepdims=True)
        ok = cnt >= capf
        return jnp.where(ok, mid, lo), jnp.where(ok, hi, mid)

    lo0 = jnp.zeros((n_e, 1), F32)
    hi0 = jnp.full((n_e, 1), 2.0, F32)
    lo, hi = lax.fori_loop(0, ROUTE_SEARCH_STEPS, search, (lo0, hi0))
    need = capf - jnp.sum(jnp.where(aff >= hi, 1.0, 0.0), axis=1, keepdims=True)
    upper = jnp.where(lax.broadcasted_iota(I32, (LANES, LANES), 0) < lax.broadcasted_iota(I32, (LANES, LANES), 1),
                      1.0, 0.0).astype(BF16)
    run_eq = jnp.zeros((n_e, 1), F32)
    run_sel = jnp.zeros((n_e, 1), F32)
    for j in range(n // LANES):
        sl = slice(j * LANES, (j + 1) * LANES)
        a = aff_ref[:, sl]
        above = a >= hi
        tie = (a >= lo) & (a < hi)
        eq = jnp.where(tie, 1.0, 0.0)
        rank = _dot(eq.astype(BF16), upper) + run_eq
        run_eq = run_eq + eq.sum(axis=1, keepdims=True)
        sel = above | (tie & (rank < need))
        self_f = jnp.where(sel, 1.0, 0.0)
        pos = _dot(self_f.astype(BF16), upper) + run_sel
        run_sel = run_sel + self_f.sum(axis=1, keepdims=True)
        pos_i = pos.astype(I32)
        pose_ref[0, :, sl] = pos_i
        posm_ref[0, :, sl] = jnp.where(sel, pos_i, -1)
        gsel_ref[0, :, sl] = jnp.where(sel, a, 0.0)


def _route(aff, batch, n, cap):
    n_e = aff.shape[0]
    spec = pl.BlockSpec((1, n_e, n), lambda b: (b, 0, 0))
    return pl.pallas_call(
        functools.partial(_route_kernel, cap=cap),
        grid=(batch,),
        in_specs=[pl.BlockSpec((n_e, n), lambda b: (0, b))],
        out_specs=[spec, spec, spec],
        out_shape=[
            jax.ShapeDtypeStruct((batch, n_e, n), I32),
            jax.ShapeDtypeStruct((batch, n_e, n), I32),
            jax.ShapeDtypeStruct((batch, n_e, n), F32),
        ],
        compiler_params=_params("arbitrary"),
        name="route",
    )(aff)


def _gather_kernel(cs_ref, posm_ref, gsel_ref, x_ref, xg_ref, gs_ref, acc_ref, gacc_ref, *, win, tc, n_chunks,
                   n_e, cap):
    b, e = pl.program_id(0), pl.program_id(1)
    base = (b * n_e + e) * (n_chunks + 1)
    acc_ref[...] = jnp.zeros_like(acc_ref)
    gacc_ref[...] = jnp.zeros_like(gacc_ref)
    row = lax.broadcasted_iota(I32, (win, tc), 0)

    def fill(c, a0, w):
        first = a0 + w * win
        start = pl.multiple_of(jnp.minimum(first, cap - win), F32_SUBLANES)
        slot = start + row
        hit = (posm_ref[0, 0, c:c + 1, :] == slot) & (slot >= first)
        onehot = jnp.where(hit, 1.0, 0.0).astype(BF16)
        acc_ref[pl.ds(start, win), :] += _dot(onehot, x_ref[c * tc:(c + 1) * tc, :])
        gacc_ref[pl.ds(start, win), :] += jnp.where(hit, gsel_ref[0, 0, c:c + 1, :], 0.0).sum(
            axis=1, keepdims=True)

    starts = [(cs_ref[base + c] // F32_SUBLANES) * F32_SUBLANES for c in range(n_chunks)]
    for c in range(n_chunks):
        fill(c, starts[c], 0)
    for c in range(n_chunks):
        n_win = (cs_ref[base + c + 1] - starts[c] + win - 1) // win

        def more(w, carry, c=c):
            fill(c, starts[c], w)
            return carry

        lax.fori_loop(1, n_win, more, 0)

    xg_ref[0] = acc_ref[...].astype(BF16)
    gs_ref[0] = gacc_ref[...]


def _gather(xf, posm, pose, gsel, batch, n, cap):
    d = xf.shape[1]
    n_e = posm.shape[1]
    tc = _tile(n, 256)
    win = min(cap, LANES)
    n_chunks = n // tc
    cstart = jnp.concatenate([pose[:, :, ::tc], jnp.full((batch, n_e, 1), cap, I32)], axis=-1).reshape(-1)
    posm4 = posm.reshape(batch, n_e, n_chunks, tc)
    gsel4 = gsel.reshape(batch, n_e, n_chunks, tc)
    grid_spec = pltpu.PrefetchScalarGridSpec(
        num_scalar_prefetch=1,
        grid=(batch, n_e),
        in_specs=[
            pl.BlockSpec((1, 1, n_chunks, tc), lambda b, e, cs: (b, e, 0, 0)),
            pl.BlockSpec((1, 1, n_chunks, tc), lambda b, e, cs: (b, e, 0, 0)),
            pl.BlockSpec((n, d), lambda b, e, cs: (b, 0)),
        ],
        out_specs=[
            pl.BlockSpec((1, cap, d), lambda b, e, cs: (e, b, 0)),
            pl.BlockSpec((1, cap, 1), lambda b, e, cs: (e, b, 0)),
        ],
        scratch_shapes=[pltpu.VMEM((cap, d), F32), pltpu.VMEM((cap, 1), F32)],
    )
    return pl.pallas_call(
        functools.partial(_gather_kernel, win=win, tc=tc, n_chunks=n_chunks, n_e=n_e, cap=cap),
        grid_spec=grid_spec,
        out_shape=[
            jax.ShapeDtypeStruct((n_e, batch * cap, d), BF16),
            jax.ShapeDtypeStruct((n_e, batch * cap, 1), F32),
        ],
        compiler_params=_params("arbitrary", "arbitrary"),
        name="moe_gather",
    )(cstart, posm4, gsel4, xf)


def _ffn_kernel(*refs, with_ctx):
    if with_ctx:
        xl_ref, xc_ref, wg_ref, wu_ref, wd_ref, gl_ref, gc_ref, yl_ref, yc_ref, accl_ref, accc_ref = refs
    else:
        xl_ref, wg_ref, wu_ref, wd_ref, gl_ref, yl_ref, accl_ref = refs
    i, f = pl.program_id(1), pl.program_id(2)
    last_f = pl.num_programs(2) - 1
    tf = wg_ref.shape[2]
    fw = min(tf, FFN_SUB)

    def run(x_ref, gs_ref, y_ref, acc_ref):
        @pl.when(f == 0)
        def _():
            acc_ref[...] = jnp.zeros_like(acc_ref)

        x = x_ref[0]
        hid = []
        for c0 in range(0, tf, fw):
            hg = _dot(x, wg_ref[0, :, c0:c0 + fw].astype(BF16))
            hu = _dot(x, wu_ref[0, :, c0:c0 + fw].astype(BF16))
            hid.append((_silu(hg) * hu).astype(BF16))
        acc_ref[...] += _dot(jnp.concatenate(hid, axis=1), wd_ref[0].astype(BF16))

        @pl.when(f == last_f)
        def _():
            y_ref[0] = (acc_ref[...] * gs_ref[0]).astype(BF16)

    run(xl_ref, gl_ref, yl_ref, accl_ref)
    if with_ctx:
        @pl.when(i == pl.num_programs(1) - 1)
        def _():
            run(xc_ref, gc_ref, yc_ref, accc_ref)


def _expert_ffn(layer, w_gate, w_up, w_down, xg, gslot, xg_c=None, gslot_c=None):
    n_e, m, d = xg.shape
    ff = w_gate.shape[3]
    tm = _tile(m, 1024)
    tf = _tile(ff, 512)
    with_ctx = xg_c is not None
    x_spec = pl.BlockSpec((1, tm, d), lambda e, i, f: (e, i, 0))
    g_spec = pl.BlockSpec((1, tm, 1), lambda e, i, f: (e, i, 0))
    w_specs = [
        pl.BlockSpec((None, 1, d, tf), lambda e, i, f: (layer, e, 0, f)),
        pl.BlockSpec((None, 1, d, tf), lambda e, i, f: (layer, e, 0, f)),
        pl.BlockSpec((None, 1, tf, d), lambda e, i, f: (layer, e, f, 0)),
    ]
    out_specs = [x_spec]
    out_shape = [jax.ShapeDtypeStruct((n_e, m, d), BF16)]
    scratch = [pltpu.VMEM((tm, d), F32)]
    if with_ctx:
        mc = xg_c.shape[1]
        xc_spec = pl.BlockSpec((1, mc, d), lambda e, i, f: (e, 0, 0))
        gc_spec = pl.BlockSpec((1, mc, 1), lambda e, i, f: (e, 0, 0))
        in_specs = [x_spec, xc_spec] + w_specs + [g_spec, gc_spec]
        args = [xg, xg_c, w_gate, w_up, w_down, gslot, gslot_c]
        out_specs.append(xc_spec)
        out_shape.append(jax.ShapeDtypeStruct((n_e, mc, d), BF16))
        scratch.append(pltpu.VMEM((mc, d), F32))
    else:
        in_specs = [x_spec] + w_specs + [g_spec]
        args = [xg, w_gate, w_up, w_down, gslot]
    return pl.pallas_call(
        functools.partial(_ffn_kernel, with_ctx=with_ctx),
        grid=(n_e, m // tm, ff // tf),
        in_specs=in_specs,
        out_specs=out_specs,
        out_shape=out_shape,
        scratch_shapes=scratch,
        compiler_params=_params("arbitrary", "arbitrary", "arbitrary"),
        name="expert_ffn",
    )(*args)


def _combine_kernel(cs_ref, posm_ref, y_ref, h_ref, g_ref, gate_ref, o_ref, *, tt, n_sub, win, cap, n_e,
                    n_tiles):
    b, i = pl.program_id(0), pl.program_id(1)
    for sub in range(n_sub):
        cols = slice(sub * tt, (sub + 1) * tt)
        acc = None
        for e in range(n_e):
            if cap <= win:
                a0 = 0
                yw = y_ref[e]
            else:
                s0 = cs_ref[(b * n_e + e) * n_tiles + i * n_sub + sub]
                a0 = jnp.minimum((s0 // BF16_SUBLANES) * BF16_SUBLANES, cap - win)
                a0 = pl.multiple_of(a0, BF16_SUBLANES)
                yw = y_ref[e, pl.ds(a0, win), :]
            wn = yw.shape[0]
            slot = a0 + lax.broadcasted_iota(I32, (wn, tt), 0)
            onehot = jnp.where(posm_ref[0, e:e + 1, cols] == slot, 1.0, 0.0).astype(BF16)
            part = lax.dot_general(onehot, yw, (((0,), (0,)), ((), ())), preferred_element_type=F32)
            acc = part if acc is None else acc + part
        o_ref[cols, :] = h_ref[cols, :] + gate_ref[...] * _rms(acc, g_ref[...])


def _combine(y, posm, pose, h, g, gate, row_fn, batch, n, cap):
    n_e = y.shape[0]
    d = y.shape[2]
    tt = LANES
    win = tt + BF16_SUBLANES
    n_tiles = n // tt
    n_sub = 2 if n_tiles % 2 == 0 else 1
    n_steps = n_tiles // n_sub
    cstart = pose[:, :, ::tt].reshape(-1)
    grid_spec = pltpu.PrefetchScalarGridSpec(
        num_scalar_prefetch=1,
        grid=(batch, n_steps),
        in_specs=[
            pl.BlockSpec((1, n_e, n_sub * tt), lambda b, i, cs: (b, 0, i)),
            pl.BlockSpec((n_e, cap, d), lambda b, i, cs: (0, b, 0), pipeline_mode=pl.Buffered(1)),
            pl.BlockSpec((n_sub * tt, d), lambda b, i, cs: (b * n_steps + i, 0)),
            pl.BlockSpec((1, d), lambda b, i, cs: (0, 0)),
            pl.BlockSpec((None, 1, d), lambda b, i, cs: (row_fn(b * n), 0, 0)),
        ],
        out_specs=pl.BlockSpec((n_sub * tt, d), lambda b, i, cs: (b * n_steps + i, 0)),
    )
    return pl.pallas_call(
        functools.partial(_combine_kernel, tt=tt, n_sub=n_sub, win=win, cap=cap, n_e=n_e, n_tiles=n_tiles),
        grid_spec=grid_spec,
        out_shape=jax.ShapeDtypeStruct(h.shape, F32),
        compiler_params=_params("arbitrary", "arbitrary"),
        name="moe_combine",
    )(cstart, posm, y, h, g, gate)


def _moe_dispatch(xf, aff, batch, n):
    cap = CAPACITY_FACTOR * n // N_EXPERTS
    posm, pose, gsel = _route(aff, batch, n, cap)
    xg, gslot = _gather(xf, posm, pose, gsel, batch, n, cap)
    return xg, gslot, (posm, pose, cap)


def _rope_angles(seq, rot_dim):
    n_rows = seq // GRID_W
    row = jnp.repeat(jnp.arange(n_rows), GRID_W)
    col = jnp.tile(jnp.arange(GRID_W), n_rows)
    n_freq = rot_dim // 4
    inv = ROPE_BASE ** (-jnp.arange(n_freq, dtype=F32) / n_freq)
    ang = jnp.concatenate([row[:, None] * inv, col[:, None] * inv], axis=-1)
    return jnp.cos(ang), jnp.sin(ang)


def kernel(x, c, ctx, c_ctx, mod_w, mod_b, norm_g, ev_w_in, ev_w_out, ev_sink, ev_qk_norm, od_w_in, od_q_norm,
           od_kv_norm, od_w_uq, od_w_ukv, od_w_out, router_w, exp_w_gate, exp_w_up, exp_w_down):
    batch, seq, d = x.shape
    n_ctx = ctx.shape[1]
    depth = mod_w.shape[0]
    assert batch < MOD_ROWS and seq % LANES == 0 and n_ctx % LANES == 0 and seq >= 4 * WINDOW

    cs = jnp.concatenate([c, c_ctx[None, :], jnp.zeros((MOD_ROWS - batch - 1, d), F32)], axis=0)
    mods = _modulation(cs, mod_w, mod_b).reshape(depth, MOD_ROWS, N_MOD, 1, d)

    def lat_row(r):
        return r // seq

    def ctx_row(r):
        return batch

    cos_h, sin_h = _rope_angles(seq, HEAD_DIM)
    rope_even = (jnp.concatenate([cos_h, cos_h], axis=-1), jnp.concatenate([-sin_h, sin_h], axis=-1))
    cos_m, sin_m = _rope_angles(seq, MLA_ROPE)
    half = MLA_ROPE // 2
    zeros = lambda w: jnp.zeros((seq, w), F32)
    rope_mla = (
        jnp.concatenate([cos_m, cos_m, zeros(LANES - MLA_ROPE)], axis=-1),
        jnp.concatenate([-sin_m, zeros(LANES - half)], axis=-1),
        jnp.concatenate([zeros(half), sin_m, zeros(LANES - MLA_ROPE)], axis=-1),
    )

    h_lat = x.reshape(batch * seq, d)
    h_ctx = ctx.reshape(batch * n_ctx, d)
    for layer in range(depth):
        with_ctx = layer < depth - 1
        i = layer // 2
        g = norm_g[layer].reshape(4, 1, d)
        m = [mods[layer, :, k] for k in range(N_MOD)]
        wr_t = router_w[layer].T
        if layer % 2 == 0:
            w_in = ev_w_in[i].astype(BF16)
            w_out = ev_w_out[i].astype(BF16)
            qkv_l = _even_inproj(h_lat, g[0], m[0], m[1], w_in, ev_qk_norm[i], rope_even, lat_row, seq)
            qkv_c = _even_inproj(h_ctx, g[0], m[0], m[1], w_in, ev_qk_norm[i], None, ctx_row, n_ctx)
            ga, gb = A_HEADS // A_KV_HEADS, B_HEADS // B_KV_HEADS
            hd = HEAD_DIM
            a_kw = dict(batch=batch, n_heads=A_HEADS, group=ga, dq=hd, dv=hd)
            b_kw = dict(batch=batch, n_heads=B_HEADS, group=gb, dq=hd, dv=hd, band=False)
            a_ctx = (qkv_c, AK0 * hd, qkv_c, AV0 * hd, n_ctx)
            b_ctx = (qkv_c, BK0 * hd, qkv_c, BV0 * hd, n_ctx)
            o_a = _attention(qkv_l, AQ0 * hd, [a_ctx, (qkv_l, AK0 * hd, qkv_l, AV0 * hd, seq)], ev_sink[i],
                             seq=seq, band=True, tq_cap=512, hp=4, **a_kw)
            o_b = _attention(qkv_l, BQ0 * hd, [b_ctx, (qkv_l, BK0 * hd, qkv_l, BV0 * hd, seq)], None,
                             seq=seq, tq_cap=512, hp=4, **b_kw)
            o_lat = [o_a, o_b]
            if with_ctx:
                o_ac = _attention(qkv_c, AQ0 * hd, [a_ctx], ev_sink[i], seq=n_ctx, band=False, tq_cap=512,
                                  hp=4, **a_kw)
                o_bc = _attention(qkv_c, BQ0 * hd, [b_ctx], None, seq=n_ctx, tq_cap=512, hp=4, **b_kw)
                o_ctx = [o_ac, o_bc]
        else:
            w_in = od_w_in[i]
            w_qkv = w_in[:, :MLA_Q_RANK + MLA_KV_RANK].astype(BF16)
            w_kr = jnp.pad(w_in[:, MLA_Q_RANK + MLA_KV_RANK:], ((0, 0), (0, LANES - MLA_ROPE))).astype(BF16)
            w_uq = od_w_uq[i].reshape(MLA_Q_RANK, MLA_HEADS, MLA_NOPE + MLA_ROPE)
            w_uq = jnp.pad(w_uq, ((0, 0), (0, 0), (0, MLA_QK_PAD - MLA_NOPE - MLA_ROPE)))
            w_uq = w_uq.reshape(MLA_Q_RANK, MLA_HEADS * MLA_QK_PAD).astype(BF16)
            w_ukv = od_w_ukv[i].astype(BF16)
            w_out = od_w_out[i].astype(BF16)
            qn, kvn = od_q_norm[i].reshape(1, -1), od_kv_norm[i].reshape(1, -1)
            q_l, k_l, v_l = _mla_proj(h_lat, g[0], m[0], m[1], w_qkv, w_kr, qn, kvn, w_uq, w_ukv, rope_mla,
                                      lat_row, seq, True)
            proj_c = _mla_proj(h_ctx, g[0], m[0], m[1], w_qkv, w_kr, qn, kvn, w_uq, w_ukv, None, ctx_row,
                               n_ctx, with_ctx)
            k_c, v_c = proj_c[-2], proj_c[-1]
            m_kw = dict(batch=batch, n_heads=MLA_HEADS, group=1, dq=MLA_QK_PAD, dv=MLA_V, band=False, tq_cap=512,
                        hp=4, tk=1024)
            o_lat = [_attention(q_l, 0, [(k_c, 0, v_c, 0, n_ctx), (k_l, 0, v_l, 0, seq)], None, seq=seq, **m_kw)]
            if with_ctx:
                o_ctx = [_attention(proj_c[0], 0, [(k_c, 0, v_c, 0, n_ctx)], None, seq=n_ctx, **m_kw)]
        ew = (layer, exp_w_gate, exp_w_up, exp_w_down)
        h_lat, xf, aff = _outproj(o_lat, w_out, h_lat, g[1], m[2], g[2], m[3], m[4], wr_t, lat_row, seq)
        xg, gslot, (posm, pose, cap) = _moe_dispatch(xf, aff, batch, seq)
        if with_ctx:
            h_ctx, xf_c, aff_c = _outproj(o_ctx, w_out, h_ctx, g[1], m[2], g[2], m[3], m[4], wr_t, ctx_row,
                                          n_ctx)
            xg_c, gslot_c, (posm_c, pose_c, cap_c) = _moe_dispatch(xf_c, aff_c, batch, n_ctx)
            y, y_c = _expert_ffn(*ew, xg, gslot, xg_c, gslot_c)
            h_ctx = _combine(y_c, posm_c, pose_c, h_ctx, g[3], m[5], ctx_row, batch, n_ctx, cap_c)
        else:
            (y,) = _expert_ffn(*ew, xg, gslot)
        h_lat = _combine(y, posm, pose, h_lat, g[3], m[5], lat_row, batch, seq, cap)
    return h_lat.reshape(batch, seq, d)
```

```python
import functools

import jax
import jax.numpy as jnp
from jax import lax
from jax.experimental import pallas as pl
from jax.experimental.pallas import tpu as pltpu

F32 = jnp.float32
BF16 = jnp.bfloat16
I32 = jnp.int32

EPS = 1e-6
GRID_W = 64
WINDOW = 128
ROPE_BASE = 10000.0
HEAD_DIM = 128
A_HEADS = 8
A_KV_HEADS = 2
B_HEADS = 8
B_KV_HEADS = 2
MLA_HEADS = 16
MLA_Q_RANK = 512
MLA_KV_RANK = 512
MLA_NOPE = 128
MLA_ROPE = 64
MLA_V = 128
MLA_QK_PAD = 256
N_EXPERTS = 16
CAPACITY_FACTOR = 2
N_MOD = 6
MOD_ROWS = 8

LANES = 128
BF16_SUBLANES = 16
F32_SUBLANES = 8
VMEM_LIMIT_BYTES = 60 * 1024 * 1024
FFN_SUB = 256
ROUTE_SEARCH_STEPS = 64

LOG2E = 1.4426950408889634

EVEN_IN_COLS = (A_HEADS + 2 * A_KV_HEADS + B_HEADS + 2 * B_KV_HEADS) * HEAD_DIM
AQ0 = 0
BQ0 = AQ0 + A_HEADS
AK0 = BQ0 + B_HEADS
BK0 = AK0 + A_KV_HEADS
AV0 = BK0 + B_KV_HEADS
BV0 = AV0 + 2 * A_KV_HEADS
EVEN_SLOTS = BV0 + 2 * B_KV_HEADS
EVEN_COLS = EVEN_SLOTS * HEAD_DIM
_W_AK0 = A_HEADS
_W_AV0 = _W_AK0 + A_KV_HEADS
_W_BQ0 = _W_AV0 + A_KV_HEADS
_W_BK0 = _W_BQ0 + B_HEADS
_W_BV0 = _W_BK0 + B_KV_HEADS


def _even_head(wh):
    if wh < _W_AK0:
        return AQ0 + wh, "aq"
    if wh < _W_AV0:
        return AK0 + wh - _W_AK0, "ak"
    if wh < _W_BQ0:
        return AV0 + 2 * (wh - _W_AV0), "av"
    if wh < _W_BK0:
        return BQ0 + wh - _W_BQ0, "bq"
    if wh < _W_BV0:
        return BK0 + wh - _W_BK0, "bk"
    return BV0 + 2 * (wh - _W_BV0), "bv"


def _params(*sem):
    return pltpu.CompilerParams(dimension_semantics=sem, vmem_limit_bytes=VMEM_LIMIT_BYTES)


def _tile(n, cap):
    t = min(n, cap)
    while n % t:
        t -= 1
    return t


def _const_spec(shape):
    nd = len(shape)
    return pl.BlockSpec(shape, lambda *_: (0,) * nd)


def _rms(x, g):
    return x * lax.rsqrt(jnp.mean(x * x, axis=-1, keepdims=True) + EPS) * g


def _silu(x):
    return x / (1.0 + jnp.exp(-x))


def _dot(a, b):
    return jnp.dot(a, b, preferred_element_type=F32)


def _dot_nt(a, b):
    return lax.dot_general(a, b, (((1,), (1,)), ((), ())), preferred_element_type=F32)


def _split_bf16(x):
    hi = x.astype(BF16)
    return hi, (x - hi.astype(F32)).astype(BF16)


def _mod_kernel(cs_ref, w_ref, b_ref, o_ref):
    s_hi, s_lo = _split_bf16(_silu(cs_ref[...]))
    w_hi, w_lo = _split_bf16(w_ref[0])
    o_ref[0] = _dot(s_hi, w_hi) + (_dot(s_hi, w_lo) + _dot(s_lo, w_hi)) + b_ref[0]


def _modulation(cs, mod_w, mod_b):
    depth, d, n6 = mod_w.shape
    tn = _tile(n6, 1024)
    return pl.pallas_call(
        _mod_kernel,
        grid=(depth, n6 // tn),
        in_specs=[
            _const_spec((MOD_ROWS, d)),
            pl.BlockSpec((1, d, tn), lambda l, j: (l, 0, j)),
            pl.BlockSpec((1, 1, tn), lambda l, j: (l, 0, j)),
        ],
        out_specs=pl.BlockSpec((1, MOD_ROWS, tn), lambda l, j: (l, 0, j)),
        out_shape=jax.ShapeDtypeStruct((depth, MOD_ROWS, n6), F32),
        compiler_params=_params("arbitrary", "arbitrary"),
        name="modulation",
    )(cs, mod_w, mod_b.reshape(depth, 1, n6))


def _mod_spec(d, row_fn):
    return pl.BlockSpec((None, 1, d), lambda i: (row_fn(i), 0, 0))


def _even_inproj_kernel(*refs, rope, scale):
    if rope:
        x_ref, g_ref, sh_ref, sc_ref, w_ref, qkg_ref, cos_ref, sin_ref, o_ref = refs
        cos, sin = cos_ref[...], sin_ref[...]
    else:
        x_ref, g_ref, sh_ref, sc_ref, w_ref, qkg_ref, o_ref = refs
    a = _rms(x_ref[...], g_ref[...]) * (1.0 + sc_ref[...]) + sh_ref[...]
    ab = a.astype(BF16)
    ones = jnp.ones((x_ref.shape[0], HEAD_DIM), BF16)
    for j in range(EVEN_IN_COLS // (2 * HEAD_DIM)):
        acc = _dot(ab, w_ref[:, j * 2 * HEAD_DIM:(j + 1) * 2 * HEAD_DIM])
        for hh in range(2):
            slot, kind = _even_head(2 * j + hh)
            v = acc[:, hh * HEAD_DIM:(hh + 1) * HEAD_DIM]
            if kind == "bq":
                v = _rms(v, qkg_ref[0:1, :])
            elif kind == "bk":
                v = _rms(v, qkg_ref[1:2, :])
            if rope and kind[1] != "v":
                v = v * cos + pltpu.roll(v, HEAD_DIM // 2, 1) * sin
            if kind[1] == "q":
                v = v * scale
            o_ref[:, slot * HEAD_DIM:(slot + 1) * HEAD_DIM] = v.astype(BF16)
            if kind[1] == "v":
                o_ref[:, (slot + 1) * HEAD_DIM:(slot + 2) * HEAD_DIM] = ones


def _even_inproj(h, g, shift, scale_m, w_bf, qk_gain, rope_tabs, row_fn, seq):
    r, d = h.shape
    tm = _tile(seq, 512)
    rope = rope_tabs is not None
    in_specs = [
        pl.BlockSpec((tm, d), lambda i: (i, 0)),
        _const_spec((1, d)),
        _mod_spec(d, lambda i: row_fn(i * tm)),
        _mod_spec(d, lambda i: row_fn(i * tm)),
        _const_spec((d, EVEN_IN_COLS)),
        _const_spec((2, HEAD_DIM)),
    ]
    args = [h, g, shift, scale_m, w_bf, qk_gain]
    if rope:
        nt = seq // tm
        in_specs += [pl.BlockSpec((tm, HEAD_DIM), lambda i: (i % nt, 0))] * 2
        args += list(rope_tabs)
    return pl.pallas_call(
        functools.partial(_even_inproj_kernel, rope=rope, scale=HEAD_DIM ** -0.5 * LOG2E),
        grid=(r // tm,),
        in_specs=in_specs,
        out_specs=pl.BlockSpec((tm, EVEN_COLS), lambda i: (i, 0)),
        out_shape=jax.ShapeDtypeStruct((r, EVEN_COLS), BF16),
        compiler_params=_params("arbitrary"),
        name="even_inproj",
    )(*args)


def _rope_pad(v, c, s1, s2):
    return v * c + pltpu.roll(v, LANES - MLA_ROPE // 2, 1) * s1 + pltpu.roll(v, MLA_ROPE // 2, 1) * s2


def _mla_proj_kernel(*refs, rope, want_q, scale):
    refs = list(refs)
    x_ref, g_ref, sh_ref, sc_ref, win_ref, wkr_ref, qn_ref, kvn_ref = refs[:8]
    refs = refs[8:]
    if want_q:
        wuq_ref = refs.pop(0)
    wukv_ref = refs.pop(0)
    if rope:
        c, s1, s2 = refs[0][...], refs[1][...], refs[2][...]
        refs = refs[3:]
    if want_q:
        q_ref = refs.pop(0)
    k_ref, v_ref = refs
    a = _rms(x_ref[...], g_ref[...]) * (1.0 + sc_ref[...]) + sh_ref[...]
    ab = a.astype(BF16)
    low = _dot(ab, win_ref[...])
    kr = _dot(ab, wkr_ref[...])
    if rope:
        kr = _rope_pad(kr, c, s1, s2)
    krb = kr.astype(BF16)
    if want_q:
        cq = _rms(low[:, :MLA_Q_RANK], qn_ref[...]).astype(BF16)
        for h in range(MLA_HEADS):
            acc = _dot(cq, wuq_ref[:, h * MLA_QK_PAD:(h + 1) * MLA_QK_PAD])
            qr = acc[:, MLA_NOPE:]
            if rope:
                qr = _rope_pad(qr, c, s1, s2)
            q_ref[:, h * MLA_QK_PAD:h * MLA_QK_PAD + MLA_NOPE] = (acc[:, :MLA_NOPE] * scale).astype(BF16)
            q_ref[:, h * MLA_QK_PAD + MLA_NOPE:(h + 1) * MLA_QK_PAD] = (qr * scale).astype(BF16)
    ckv = _rms(low[:, MLA_Q_RANK:], kvn_ref[...]).astype(BF16)
    hw = MLA_NOPE + MLA_V
    for h in range(MLA_HEADS):
        acc = _dot(ckv, wukv_ref[:, h * hw:(h + 1) * hw])
        k_ref[:, h * MLA_QK_PAD:h * MLA_QK_PAD + MLA_NOPE] = acc[:, :MLA_NOPE].astype(BF16)
        k_ref[:, h * MLA_QK_PAD + MLA_NOPE:(h + 1) * MLA_QK_PAD] = krb
        v_ref[:, 2 * h * MLA_V:(2 * h + 1) * MLA_V] = acc[:, MLA_NOPE:].astype(BF16)
        v_ref[:, (2 * h + 1) * MLA_V:(2 * h + 2) * MLA_V] = jnp.ones((x_ref.shape[0], MLA_V), BF16)


def _mla_proj(h, g, shift, scale_m, w_in_bf, w_kr_bf, q_norm, kv_norm, w_uq_bf, w_ukv_bf, rope_tabs,
              row_fn, seq, want_q):
    r, d = h.shape
    tm = _tile(seq, 512)
    rope = rope_tabs is not None
    in_specs = [
        pl.BlockSpec((tm, d), lambda i: (i, 0)),
        _const_spec((1, d)),
        _mod_spec(d, lambda i: row_fn(i * tm)),
        _mod_spec(d, lambda i: row_fn(i * tm)),
        _const_spec(w_in_bf.shape),
        _const_spec(w_kr_bf.shape),
        _const_spec((1, MLA_Q_RANK)),
        _const_spec((1, MLA_KV_RANK)),
    ]
    args = [h, g, shift, scale_m, w_in_bf, w_kr_bf, q_norm, kv_norm]
    if want_q:
        in_specs.append(_const_spec(w_uq_bf.shape))
        args.append(w_uq_bf)
    in_specs.append(_const_spec(w_ukv_bf.shape))
    args.append(w_ukv_bf)
    if rope:
        nt = seq // tm
        in_specs += [pl.BlockSpec((tm, LANES), lambda i: (i % nt, 0))] * 3
        args += list(rope_tabs)
    kcols = MLA_HEADS * MLA_QK_PAD
    vcols = MLA_HEADS * 2 * MLA_V
    out_specs = [pl.BlockSpec((tm, kcols), lambda i: (i, 0)), pl.BlockSpec((tm, vcols), lambda i: (i, 0))]
    out_shape = [jax.ShapeDtypeStruct((r, kcols), BF16), jax.ShapeDtypeStruct((r, vcols), BF16)]
    if want_q:
        out_specs.insert(0, pl.BlockSpec((tm, kcols), lambda i: (i, 0)))
        out_shape.insert(0, jax.ShapeDtypeStruct((r, kcols), BF16))
    return pl.pallas_call(
        functools.partial(_mla_proj_kernel, rope=rope, want_q=want_q,
                          scale=(MLA_NOPE + MLA_ROPE) ** -0.5 * LOG2E),
        grid=(r // tm,),
        in_specs=in_specs,
        out_specs=out_specs,
        out_shape=out_shape,
        compiler_params=_params("arbitrary"),
        name="mla_proj",
    )(*args)


def _attn_kernel(*refs, n_seg, band, use_sink, tq, seq, hp, group, dq, dv, tk):
    refs = list(refs)
    if use_sink:
        sink_ref = refs.pop(0)
    q_ref = refs.pop(0)
    o_ref = refs.pop()
    dvx = 2 * dv
    if band:
        wk = tq + 2 * WINDOW
        q0 = pl.program_id(2) * tq
        start = pl.multiple_of(jnp.clip(q0 - WINDOW, 0, seq - wk), LANES)
        dist = (lax.broadcasted_iota(I32, (tq, wk), 1) - lax.broadcasted_iota(I32, (tq, wk), 0)) + (start - q0)
        valid = jnp.abs(dist) <= WINDOW
    for j in range(hp):
        kv = j // group
        q = q_ref[:, j * dq:(j + 1) * dq]
        m = None
        acc = None
        for s_i in range(n_seg):
            k_ref, v_ref = refs[2 * s_i], refs[2 * s_i + 1]
            slen = k_ref.shape[0]
            masked = band and s_i == n_seg - 1
            chunks = [(start, wk)] if masked else [(c0, min(tk, slen - c0)) for c0 in range(0, slen, tk)]
            for c0, cl in chunks:
                s = _dot_nt(q, k_ref[pl.ds(c0, cl), kv * dq:(kv + 1) * dq])
                if masked:
                    s = jnp.where(valid, s, -jnp.inf)
                v = v_ref[pl.ds(c0, cl), kv * dvx:(kv + 1) * dvx]
                mc = s.max(axis=-1, keepdims=True)
                if m is None:
                    m = mc
                    acc = _dot(jnp.exp2(s - m).astype(BF16), v)
                else:
                    m_new = jnp.maximum(m, mc)
                    acc = jnp.exp2(m - m_new) * acc + _dot(jnp.exp2(s - m_new).astype(BF16), v)
                    m = m_new
        den = acc[:, dv:]
        if use_sink:
            sink = sink_ref[pl.program_id(1) * hp + j] * LOG2E
            den = den + jnp.exp2(sink - m)
        o_ref[:, j * dv:(j + 1) * dv] = (acc[:, :dv] / den).astype(o_ref.dtype)


def _attention(q_arr, q_col0, segs, sink, *, batch, seq, n_heads, group, dq, dv, band, tq_cap, hp, tk=512):
    tq = _tile(seq, tq_cap)
    nq = seq // tq
    use_sink = sink is not None
    if hp <= group:
        assert group % hp == 0
        kvp, kv_of, step_group = 1, (lambda hg: hg // (group // hp)), hp
    else:
        assert hp % group == 0
        kvp, kv_of, step_group = hp // group, (lambda hg: hg), group
    qw, kw, vw = hp * dq, kvp * dq, kvp * 2 * dv
    assert q_col0 % qw == 0
    in_specs, args = [], []
    if use_sink:
        in_specs.append(pl.BlockSpec(memory_space=pltpu.SMEM))
        args.append(sink)
    in_specs.append(pl.BlockSpec((tq, qw), lambda b, hg, i: (b * nq + i, q_col0 // qw + hg)))
    args.append(q_arr)
    for k_arr, k_col0, v_arr, v_col0, slen in segs:
        assert k_col0 % kw == 0 and v_col0 % vw == 0
        in_specs.append(pl.BlockSpec((slen, kw), lambda b, hg, i, c=k_col0 // kw: (b, c + kv_of(hg))))
        in_specs.append(pl.BlockSpec((slen, vw), lambda b, hg, i, c=v_col0 // vw: (b, c + kv_of(hg))))
        args += [k_arr, v_arr]
    return pl.pallas_call(
        functools.partial(_attn_kernel, n_seg=len(segs), band=band, use_sink=use_sink, tq=tq, seq=seq, hp=hp,
                          group=step_group, dq=dq, dv=dv, tk=tk),
        grid=(batch, n_heads // hp, nq),
        in_specs=in_specs,
        out_specs=pl.BlockSpec((tq, hp * dv), lambda b, hg, i: (b * nq + i, hg)),
        out_shape=jax.ShapeDtypeStruct((batch * seq, n_heads * dv), BF16),
        compiler_params=_params("arbitrary", "arbitrary", "arbitrary"),
        name="attention",
    )(*args)


def _outproj_kernel(*refs, n_in):
    o_refs = refs[:n_in]
    (w_ref, h_ref, g1_ref, gate_ref, g2_ref, sh_ref, sc_ref, wr_ref, hn_ref, xf_ref, aff_ref) = refs[n_in:]
    y = None
    k0 = 0
    for o_ref in o_refs:
        kk = o_ref.shape[1]
        part = _dot(o_ref[...], w_ref[k0:k0 + kk, :])
        y = part if y is None else y + part
        k0 += kk
    hn = h_ref[...] + gate_ref[...] * _rms(y, g1_ref[...])
    hn_ref[...] = hn
    xf = _rms(hn, g2_ref[...]) * (1.0 + sc_ref[...]) + sh_ref[...]
    xf_ref[...] = xf.astype(BF16)
    logits = lax.dot_general(wr_ref[...], xf, (((1,), (1,)), ((), ())),
                             precision=lax.Precision.HIGHEST, preferred_element_type=F32)
    e = jnp.exp(logits - logits.max(axis=0, keepdims=True))
    aff_ref[...] = e / e.sum(axis=0, keepdims=True)


def _outproj(o_list, w_bf, h, g1, gate, g2, shift, scale_m, wr_t, row_fn, seq):
    r, d = h.shape
    tm = _tile(seq, 512)
    n_e = wr_t.shape[0]
    in_specs = [pl.BlockSpec((tm, o.shape[1]), lambda i: (i, 0)) for o in o_list]
    in_specs += [
        _const_spec(w_bf.shape),
        pl.BlockSpec((tm, d), lambda i: (i, 0)),
        _const_spec((1, d)),
        _mod_spec(d, lambda i: row_fn(i * tm)),
        _const_spec((1, d)),
        _mod_spec(d, lambda i: row_fn(i * tm)),
        _mod_spec(d, lambda i: row_fn(i * tm)),
        _const_spec((n_e, d)),
    ]
    return pl.pallas_call(
        functools.partial(_outproj_kernel, n_in=len(o_list)),
        grid=(r // tm,),
        in_specs=in_specs,
        out_specs=[
            pl.BlockSpec((tm, d), lambda i: (i, 0)),
            pl.BlockSpec((tm, d), lambda i: (i, 0)),
            pl.BlockSpec((n_e, tm), lambda i: (0, i)),
        ],
        out_shape=[
            jax.ShapeDtypeStruct((r, d), F32),
            jax.ShapeDtypeStruct((r, d), BF16),
            jax.ShapeDtypeStruct((n_e, r), F32),
        ],
        compiler_params=_params("arbitrary"),
        name="outproj",
    )(*o_list, w_bf, h, g1, gate, g2, shift, scale_m, wr_t)


def _route_kernel(aff_ref, posm_ref, pose_ref, gsel_ref, *, cap):
    n_e, n = aff_ref.shape
    aff = aff_ref[...]
    capf = float(cap)
    floor = float(jnp.finfo(F32).tiny)

    def search(_, carry):
        lo, hi = carry
        mid = jnp.sqrt(jnp.maximum(lo, floor) * hi)
        cnt = jnp.sum(jnp.where(aff >= mid, 1.0, 0.0), axis=1, keepdims=True)
        ok = cnt >= capf
        return jnp.where(ok, mid, lo), jnp.where(ok, hi, mid)

    lo0 = jnp.zeros((n_e, 1), F32)
    hi0 = jnp.full((n_e, 1), 2.0, F32)
    lo, hi = lax.fori_loop(0, ROUTE_SEARCH_STEPS, search, (lo0, hi0))
    need = capf - jnp.sum(jnp.where(aff >= hi, 1.0, 0.0), axis=1, keepdims=True)
    upper = jnp.where(lax.broadcasted_iota(I32, (LANES, LANES), 0) < lax.broadcasted_iota(I32, (LANES, LANES), 1),
                      1.0, 0.0).astype(BF16)
    run_eq = jnp.zeros((n_e, 1), F32)
    run_sel = jnp.zeros((n_e, 1), F32)
    for j in range(n // LANES):
        sl = slice(j * LANES, (j + 1) * LANES)
        a = aff_ref[:, sl]
        above = a >= hi
        tie = (a >= lo) & (a < hi)
        eq = jnp.where(tie, 1.0, 0.0)
        rank = _dot(eq.astype(BF16), upper) + run_eq
        run_eq = run_eq + eq.sum(axis=1, keepdims=True)
        sel = above | (tie & (rank < need))
        self_f = jnp.where(sel, 1.0, 0.0)
        pos = _dot(self_f.astype(BF16), upper) + run_sel
        run_sel = run_sel + self_f.sum(axis=1, keepdims=True)
        pos_i = pos.astype(I32)
        pose_ref[0, :, sl] = pos_i
        posm_ref[0, :, sl] = jnp.where(sel, pos_i, -1)
        gsel_ref[0, :, sl] = jnp.where(sel, a, 0.0)


def _route(aff, batch, n, cap):
    n_e = aff.shape[0]
    spec = pl.BlockSpec((1, n_e, n), lambda b: (b, 0, 0))
    return pl.pallas_call(
        functools.partial(_route_kernel, cap=cap),
        grid=(batch,),
        in_specs=[pl.BlockSpec((n_e, n), lambda b: (0, b))],
        out_specs=[spec, spec, spec],
        out_shape=[
            jax.ShapeDtypeStruct((batch, n_e, n), I32),
            jax.ShapeDtypeStruct((batch, n_e, n), I32),
            jax.ShapeDtypeStruct((batch, n_e, n), F32),
        ],
        compiler_params=_params("arbitrary"),
        name="route",
    )(aff)


def _gather_kernel(cs_ref, posm_ref, gsel_ref, x_ref, xg_ref, gs_ref, acc_ref, gacc_ref, *, win, tc, n_chunks,
                   n_e, cap):
    b, e = pl.program_id(0), pl.program_id(1)
    base = (b * n_e + e) * (n_chunks + 1)
    acc_ref[...] = jnp.zeros_like(acc_ref)
    gacc_ref[...] = jnp.zeros_like(gacc_ref)
    row = lax.broadcasted_iota(I32, (win, tc), 0)

    def fill(c, a0, w):
        first = a0 + w * win
        start = pl.multiple_of(jnp.minimum(first, cap - win), F32_SUBLANES)
        slot = start + row
        hit = (posm_ref[0, 0, c:c + 1, :] == slot) & (slot >= first)
        onehot = jnp.where(hit, 1.0, 0.0).astype(BF16)
        acc_ref[pl.ds(start, win), :] += _dot(onehot, x_ref[c * tc:(c + 1) * tc, :])
        gacc_ref[pl.ds(start, win), :] += jnp.where(hit, gsel_ref[0, 0, c:c + 1, :], 0.0).sum(
            axis=1, keepdims=True)

    starts = [(cs_ref[base + c] // F32_SUBLANES) * F32_SUBLANES for c in range(n_chunks)]
    for c in range(n_chunks):
        fill(c, starts[c], 0)
    for c in range(n_chunks):
        n_win = (cs_ref[base + c + 1] - starts[c] + win - 1) // win

        def more(w, carry, c=c):
            fill(c, starts[c], w)
            return carry

        lax.fori_loop(1, n_win, more, 0)

    xg_ref[0] = acc_ref[...].astype(BF16)
    gs_ref[0] = gacc_ref[...]


def _gather(xf, posm, pose, gsel, batch, n, cap):
    d = xf.shape[1]
    n_e = posm.shape[1]
    tc = _tile(n, 512)
    win = min(cap, LANES)
    n_chunks = n // tc
    cstart = jnp.concatenate([pose[:, :, ::tc], jnp.full((batch, n_e, 1), cap, I32)], axis=-1).reshape(-1)
    posm4 = posm.reshape(batch, n_e, n_chunks, tc)
    gsel4 = gsel.reshape(batch, n_e, n_chunks, tc)
    grid_spec = pltpu.PrefetchScalarGridSpec(
        num_scalar_prefetch=1,
        grid=(batch, n_e),
        in_specs=[
            pl.BlockSpec((1, 1, n_chunks, tc), lambda b, e, cs: (b, e, 0, 0)),
            pl.BlockSpec((1, 1, n_chunks, tc), lambda b, e, cs: (b, e, 0, 0)),
            pl.BlockSpec((n, d), lambda b, e, cs: (b, 0)),
        ],
        out_specs=[
            pl.BlockSpec((1, cap, d), lambda b, e, cs: (e, b, 0)),
            pl.BlockSpec((1, cap, 1), lambda b, e, cs: (e, b, 0)),
        ],
        scratch_shapes=[pltpu.VMEM((cap, d), F32), pltpu.VMEM((cap, 1), F32)],
    )
    return pl.pallas_call(
        functools.partial(_gather_kernel, win=win, tc=tc, n_chunks=n_chunks, n_e=n_e, cap=cap),
        grid_spec=grid_spec,
        out_shape=[
            jax.ShapeDtypeStruct((n_e, batch * cap, d), BF16),
            jax.ShapeDtypeStruct((n_e, batch * cap, 1), F32),
        ],
        compiler_params=_params("arbitrary", "arbitrary"),
        name="moe_gather",
    )(cstart, posm4, gsel4, xf)


def _ffn_kernel(*refs, with_ctx):
    if with_ctx:
        xl_ref, xc_ref, wg_ref, wu_ref, wd_ref, gl_ref, gc_ref, yl_ref, yc_ref, accl_ref, accc_ref = refs
    else:
        xl_ref, wg_ref, wu_ref, wd_ref, gl_ref, yl_ref, accl_ref = refs
    i, f = pl.program_id(1), pl.program_id(2)
    last_f = pl.num_programs(2) - 1
    tf = wg_ref.shape[2]
    fw = min(tf, FFN_SUB)

    def run(x_ref, gs_ref, y_ref, acc_ref):
        @pl.when(f == 0)
        def _():
            acc_ref[...] = jnp.zeros_like(acc_ref)

        x = x_ref[0]
        hid = []
        for c0 in range(0, tf, fw):
            hg = _dot(x, wg_ref[0, :, c0:c0 + fw].astype(BF16))
            hu = _dot(x, wu_ref[0, :, c0:c0 + fw].astype(BF16))
            hid.append((_silu(hg) * hu).astype(BF16))
        acc_ref[...] += _dot(jnp.concatenate(hid, axis=1), wd_ref[0].astype(BF16))

        @pl.when(f == last_f)
        def _():
            y_ref[0] = (acc_ref[...] * gs_ref[0]).astype(BF16)

    run(xl_ref, gl_ref, yl_ref, accl_ref)
    if with_ctx:
        @pl.when(i == pl.num_programs(1) - 1)
        def _():
            run(xc_ref, gc_ref, yc_ref, accc_ref)


def _expert_ffn(layer, w_gate, w_up, w_down, xg, gslot, xg_c=None, gslot_c=None):
    n_e, m, d = xg.shape
    ff = w_gate.shape[3]
    tm = _tile(m, 1024)
    tf = _tile(ff, 512)
    with_ctx = xg_c is not None
    x_spec = pl.BlockSpec((1, tm, d), lambda e, i, f: (e, i, 0))
    g_spec = pl.BlockSpec((1, tm, 1), lambda e, i, f: (e, i, 0))
    w_specs = [
        pl.BlockSpec((None, 1, d, tf), lambda e, i, f: (layer, e, 0, f)),
        pl.BlockSpec((None, 1, d, tf), lambda e, i, f: (layer, e, 0, f)),
        pl.BlockSpec((None, 1, tf, d), lambda e, i, f: (layer, e, f, 0)),
    ]
    out_specs = [x_spec]
    out_shape = [jax.ShapeDtypeStruct((n_e, m, d), BF16)]
    scratch = [pltpu.VMEM((tm, d), F32)]
    if with_ctx:
        mc = xg_c.shape[1]
        xc_spec = pl.BlockSpec((1, mc, d), lambda e, i, f: (e, 0, 0))
        gc_spec = pl.BlockSpec((1, mc, 1), lambda e, i, f: (e, 0, 0))
        in_specs = [x_spec, xc_spec] + w_specs + [g_spec, gc_spec]
        args = [xg, xg_c, w_gate, w_up, w_down, gslot, gslot_c]
        out_specs.append(xc_spec)
        out_shape.append(jax.ShapeDtypeStruct((n_e, mc, d), BF16))
        scratch.append(pltpu.VMEM((mc, d), F32))
    else:
        in_specs = [x_spec] + w_specs + [g_spec]
        args = [xg, w_gate, w_up, w_down, gslot]
    return pl.pallas_call(
        functools.partial(_ffn_kernel, with_ctx=with_ctx),
        grid=(n_e, m // tm, ff // tf),
        in_specs=in_specs,
        out_specs=out_specs,
        out_shape=out_shape,
        scratch_shapes=scratch,
        compiler_params=_params("arbitrary", "arbitrary", "arbitrary"),
        name="expert_ffn",
    )(*args)


def _combine_kernel(cs_ref, posm_ref, y_ref, h_ref, g_ref, gate_ref, o_ref, *, tt, n_sub, win, cap, n_e,
                    n_tiles):
    b, i = pl.program_id(0), pl.program_id(1)
    for sub in range(n_sub):
        cols = slice(sub * tt, (sub + 1) * tt)
        acc = None
        for e in range(n_e):
            if cap <= win:
                a0 = 0
                yw = y_ref[e]
            else:
                s0 = cs_ref[(b * n_e + e) * n_tiles + i * n_sub + sub]
                a0 = jnp.minimum((s0 // BF16_SUBLANES) * BF16_SUBLANES, cap - win)
                a0 = pl.multiple_of(a0, BF16_SUBLANES)
                yw = y_ref[e, pl.ds(a0, win), :]
            wn = yw.shape[0]
            slot = a0 + lax.broadcasted_iota(I32, (wn, tt), 0)
            onehot = jnp.where(posm_ref[0, e:e + 1, cols] == slot, 1.0, 0.0).astype(BF16)
            part = lax.dot_general(onehot, yw, (((0,), (0,)), ((), ())), preferred_element_type=F32)
            acc = part if acc is None else acc + part
        o_ref[cols, :] = h_ref[cols, :] + gate_ref[...] * _rms(acc, g_ref[...])


def _combine(y, posm, pose, h, g, gate, row_fn, batch, n, cap):
    n_e = y.shape[0]
    d = y.shape[2]
    tt = LANES
    win = tt + BF16_SUBLANES
    n_tiles = n // tt
    n_sub = 2 if n_tiles % 2 == 0 else 1
    n_steps = n_tiles // n_sub
    cstart = pose[:, :, ::tt].reshape(-1)
    grid_spec = pltpu.PrefetchScalarGridSpec(
        num_scalar_prefetch=1,
        grid=(batch, n_steps),
        in_specs=[
            pl.BlockSpec((1, n_e, n_sub * tt), lambda b, i, cs: (b, 0, i)),
            pl.BlockSpec((n_e, cap, d), lambda b, i, cs: (0, b, 0), pipeline_mode=pl.Buffered(1)),
            pl.BlockSpec((n_sub * tt, d), lambda b, i, cs: (b * n_steps + i, 0)),
            pl.BlockSpec((1, d), lambda b, i, cs: (0, 0)),
            pl.BlockSpec((None, 1, d), lambda b, i, cs: (row_fn(b * n), 0, 0)),
        ],
        out_specs=pl.BlockSpec((n_sub * tt, d), lambda b, i, cs: (b * n_steps + i, 0)),
    )
    return pl.pallas_call(
        functools.partial(_combine_kernel, tt=tt, n_sub=n_sub, win=win, cap=cap, n_e=n_e, n_tiles=n_tiles),
        grid_spec=grid_spec,
        out_shape=jax.ShapeDtypeStruct(h.shape, F32),
        compiler_params=_params("arbitrary", "arbitrary"),
        name="moe_combine",
    )(cstart, posm, y, h, g, gate)


def _moe_dispatch(xf, aff, batch, n):
    cap = CAPACITY_FACTOR * n // N_EXPERTS
    posm, pose, gsel = _route(aff, batch, n, cap)
    xg, gslot = _gather(xf, posm, pose, gsel, batch, n, cap)
    return xg, gslot, (posm, pose, cap)


def _rope_angles(seq, rot_dim):
    n_rows = seq // GRID_W
    row = jnp.repeat(jnp.arange(n_rows), GRID_W)
    col = jnp.tile(jnp.arange(GRID_W), n_rows)
    n_freq = rot_dim // 4
    inv = ROPE_BASE ** (-jnp.arange(n_freq, dtype=F32) / n_freq)
    ang = jnp.concatenate([row[:, None] * inv, col[:, None] * inv], axis=-1)
    return jnp.cos(ang), jnp.sin(ang)


def kernel(x, c, ctx, c_ctx, mod_w, mod_b, norm_g, ev_w_in, ev_w_out, ev_sink, ev_qk_norm, od_w_in, od_q_norm,
           od_kv_norm, od_w_uq, od_w_ukv, od_w_out, router_w, exp_w_gate, exp_w_up, exp_w_down):
    batch, seq, d = x.shape
    n_ctx = ctx.shape[1]
    depth = mod_w.shape[0]
    assert batch < MOD_ROWS and seq % LANES == 0 and n_ctx % LANES == 0 and seq >= 4 * WINDOW

    cs = jnp.concatenate([c, c_ctx[None, :], jnp.zeros((MOD_ROWS - batch - 1, d), F32)], axis=0)
    mods = _modulation(cs, mod_w, mod_b).reshape(depth, MOD_ROWS, N_MOD, 1, d)

    def lat_row(r):
        return r // seq

    def ctx_row(r):
        return batch

    cos_h, sin_h = _rope_angles(seq, HEAD_DIM)
    rope_even = (jnp.concatenate([cos_h, cos_h], axis=-1), jnp.concatenate([-sin_h, sin_h], axis=-1))
    cos_m, sin_m = _rope_angles(seq, MLA_ROPE)
    half = MLA_ROPE // 2
    zeros = lambda w: jnp.zeros((seq, w), F32)
    rope_mla = (
        jnp.concatenate([cos_m, cos_m, zeros(LANES - MLA_ROPE)], axis=-1),
        jnp.concatenate([-sin_m, zeros(LANES - half)], axis=-1),
        jnp.concatenate([zeros(half), sin_m, zeros(LANES - MLA_ROPE)], axis=-1),
    )

    h_lat = x.reshape(batch * seq, d)
    h_ctx = ctx.reshape(batch * n_ctx, d)
    for layer in range(depth):
        with_ctx = layer < depth - 1
        i = layer // 2
        g = norm_g[layer].reshape(4, 1, d)
        m = [mods[layer, :, k] for k in range(N_MOD)]
        wr_t = router_w[layer].T
        if layer % 2 == 0:
            w_in = ev_w_in[i].astype(BF16)
            w_out = ev_w_out[i].astype(BF16)
            qkv_l = _even_inproj(h_lat, g[0], m[0], m[1], w_in, ev_qk_norm[i], rope_even, lat_row, seq)
            qkv_c = _even_inproj(h_ctx, g[0], m[0], m[1], w_in, ev_qk_norm[i], None, ctx_row, n_ctx)
            ga, gb = A_HEADS // A_KV_HEADS, B_HEADS // B_KV_HEADS
            hd = HEAD_DIM
            a_kw = dict(batch=batch, n_heads=A_HEADS, group=ga, dq=hd, dv=hd)
            b_kw = dict(batch=batch, n_heads=B_HEADS, group=gb, dq=hd, dv=hd, band=False)
            a_ctx = (qkv_c, AK0 * hd, qkv_c, AV0 * hd, n_ctx)
            b_ctx = (qkv_c, BK0 * hd, qkv_c, BV0 * hd, n_ctx)
            o_a = _attention(qkv_l, AQ0 * hd, [a_ctx, (qkv_l, AK0 * hd, qkv_l, AV0 * hd, seq)], ev_sink[i],
                             seq=seq, band=True, tq_cap=512, hp=4, **a_kw)
            o_b = _attention(qkv_l, BQ0 * hd, [b_ctx, (qkv_l, BK0 * hd, qkv_l, BV0 * hd, seq)], None,
                             seq=seq, tq_cap=512, hp=4, **b_kw)
            o_lat = [o_a, o_b]
            if with_ctx:
                o_ac = _attention(qkv_c, AQ0 * hd, [a_ctx], ev_sink[i], seq=n_ctx, band=False, tq_cap=512,
                                  hp=4, **a_kw)
                o_bc = _attention(qkv_c, BQ0 * hd, [b_ctx], None, seq=n_ctx, tq_cap=512, hp=4, **b_kw)
                o_ctx = [o_ac, o_bc]
        else:
            w_in = od_w_in[i]
            w_qkv = w_in[:, :MLA_Q_RANK + MLA_KV_RANK].astype(BF16)
            w_kr = jnp.pad(w_in[:, MLA_Q_RANK + MLA_KV_RANK:], ((0, 0), (0, LANES - MLA_ROPE))).astype(BF16)
            w_uq = od_w_uq[i].reshape(MLA_Q_RANK, MLA_HEADS, MLA_NOPE + MLA_ROPE)
            w_uq = jnp.pad(w_uq, ((0, 0), (0, 0), (0, MLA_QK_PAD - MLA_NOPE - MLA_ROPE)))
            w_uq = w_uq.reshape(MLA_Q_RANK, MLA_HEADS * MLA_QK_PAD).astype(BF16)
            w_ukv = od_w_ukv[i].astype(BF16)
            w_out = od_w_out[i].astype(BF16)
            qn, kvn = od_q_norm[i].reshape(1, -1), od_kv_norm[i].reshape(1, -1)
            q_l, k_l, v_l = _mla_proj(h_lat, g[0], m[0], m[1], w_qkv, w_kr, qn, kvn, w_uq, w_ukv, rope_mla,
                                      lat_row, seq, True)
            proj_c = _mla_proj(h_ctx, g[0], m[0], m[1], w_qkv, w_kr, qn, kvn, w_uq, w_ukv, None, ctx_row,
                               n_ctx, with_ctx)
            k_c, v_c = proj_c[-2], proj_c[-1]
            m_kw = dict(batch=batch, n_heads=MLA_HEADS, group=1, dq=MLA_QK_PAD, dv=MLA_V, band=False, tq_cap=512,
                        hp=4, tk=1024)
            o_lat = [_attention(q_l, 0, [(k_c, 0, v_c, 0, n_ctx), (k_l, 0, v_l, 0, seq)], None, seq=seq, **m_kw)]
            if with_ctx:
                o_ctx = [_attention(proj_c[0], 0, [(k_c, 0, v_c, 0, n_ctx)], None, seq=n_ctx, **m_kw)]
        ew = (layer, exp_w_gate, exp_w_up, exp_w_down)
        h_lat, xf, aff = _outproj(o_lat, w_out, h_lat, g[1], m[2], g[2], m[3], m[4], wr_t, lat_row, seq)
        xg, gslot, (posm, pose, cap) = _moe_dispatch(xf, aff, batch, seq)
        if with_ctx:
            h_ctx, xf_c, aff_c = _outproj(o_ctx, w_out, h_ctx, g[1], m[2], g[2], m[3], m[4], wr_t, ctx_row,
                                          n_ctx)
            xg_c, gslot_c, (posm_c, pose_c, cap_c) = _moe_dispatch(xf_c, aff_c, batch, n_ctx)
            y, y_c = _expert_ffn(*ew, xg, gslot, xg_c, gslot_c)
            h_ctx = _combine(y_c, posm_c, pose_c, h_ctx, g[3], m[5], ctx_row, batch, n_ctx, cap_c)
        else:
            (y,) = _expert_ffn(*ew, xg, gslot)
        h_lat = _combine(y, posm, pose, h_lat, g[3], m[5], lat_row, batch, seq, cap)
    return h_lat.reshape(batch, seq, d)
```

```python
import functools

import jax
import jax.numpy as jnp
from jax import lax
from jax.experimental import pallas as pl
from jax.experimental.pallas import tpu as pltpu

F32 = jnp.float32
BF16 = jnp.bfloat16
I32 = jnp.int32

EPS = 1e-6
GRID_W = 64
WINDOW = 128
ROPE_BASE = 10000.0
HEAD_DIM = 128
A_HEADS = 8
A_KV_HEADS = 2
B_HEADS = 8
B_KV_HEADS = 2
MLA_HEADS = 16
MLA_Q_RANK = 512
MLA_KV_RANK = 512
MLA_NOPE = 128
MLA_ROPE = 64
MLA_V = 128
MLA_QK_PAD = 256
N_EXPERTS = 16
CAPACITY_FACTOR = 2
N_MOD = 6
MOD_ROWS = 8

LANES = 128
BF16_SUBLANES = 16
F32_SUBLANES = 8
VMEM_LIMIT_BYTES = 60 * 1024 * 1024
FFN_SUB = 256
ROUTE_SEARCH_STEPS = 64

LOG2E = 1.4426950408889634

EVEN_IN_COLS = (A_HEADS + 2 * A_KV_HEADS + B_HEADS + 2 * B_KV_HEADS) * HEAD_DIM
AQ0 = 0
BQ0 = AQ0 + A_HEADS
AK0 = BQ0 + B_HEADS
BK0 = AK0 + A_KV_HEADS
AV0 = BK0 + B_KV_HEADS
BV0 = AV0 + 2 * A_KV_HEADS
EVEN_SLOTS = BV0 + 2 * B_KV_HEADS
EVEN_COLS = EVEN_SLOTS * HEAD_DIM
_W_AK0 = A_HEADS
_W_AV0 = _W_AK0 + A_KV_HEADS
_W_BQ0 = _W_AV0 + A_KV_HEADS
_W_BK0 = _W_BQ0 + B_HEADS
_W_BV0 = _W_BK0 + B_KV_HEADS


def _even_head(wh):
    if wh < _W_AK0:
        return AQ0 + wh, "aq"
    if wh < _W_AV0:
        return AK0 + wh - _W_AK0, "ak"
    if wh < _W_BQ0:
        return AV0 + 2 * (wh - _W_AV0), "av"
    if wh < _W_BK0:
        return BQ0 + wh - _W_BQ0, "bq"
    if wh < _W_BV0:
        return BK0 + wh - _W_BK0, "bk"
    return BV0 + 2 * (wh - _W_BV0), "bv"


def _params(*sem):
    return pltpu.CompilerParams(dimension_semantics=sem, vmem_limit_bytes=VMEM_LIMIT_BYTES)


def _tile(n, cap):
    t = min(n, cap)
    while n % t:
        t -= 1
    return t


def _const_spec(shape):
    nd = len(shape)
    return pl.BlockSpec(shape, lambda *_: (0,) * nd)


def _rms(x, g):
    return x * lax.rsqrt(jnp.mean(x * x, axis=-1, keepdims=True) + EPS) * g


def _silu(x):
    return x / (1.0 + jnp.exp(-x))


def _dot(a, b):
    return jnp.dot(a, b, preferred_element_type=F32)


def _dot_nt(a, b):
    return lax.dot_general(a, b, (((1,), (1,)), ((), ())), preferred_element_type=F32)


def _split_bf16(x):
    hi = x.astype(BF16)
    return hi, (x - hi.astype(F32)).astype(BF16)


def _mod_kernel(cs_ref, w_ref, b_ref, o_ref):
    s_hi, s_lo = _split_bf16(_silu(cs_ref[...]))
    w_hi, w_lo = _split_bf16(w_ref[0])
    o_ref[0] = _dot(s_hi, w_hi) + (_dot(s_hi, w_lo) + _dot(s_lo, w_hi)) + b_ref[0]


def _modulation(cs, mod_w, mod_b):
    depth, d, n6 = mod_w.shape
    tn = _tile(n6, 1024)
    return pl.pallas_call(
        _mod_kernel,
        grid=(depth, n6 // tn),
        in_specs=[
            _const_spec((MOD_ROWS, d)),
            pl.BlockSpec((1, d, tn), lambda l, j: (l, 0, j)),
            pl.BlockSpec((1, 1, tn), lambda l, j: (l, 0, j)),
        ],
        out_specs=pl.BlockSpec((1, MOD_ROWS, tn), lambda l, j: (l, 0, j)),
        out_shape=jax.ShapeDtypeStruct((depth, MOD_ROWS, n6), F32),
        compiler_params=_params("arbitrary", "arbitrary"),
        name="modulation",
    )(cs, mod_w, mod_b.reshape(depth, 1, n6))


def _mod_spec(d, row_fn):
    return pl.BlockSpec((None, 1, d), lambda i: (row_fn(i), 0, 0))


def _even_inproj_kernel(*refs, rope, scale):
    if rope:
        x_ref, g_ref, sh_ref, sc_ref, w_ref, qkg_ref, cos_ref, sin_ref, o_ref = refs
        cos, sin = cos_ref[...], sin_ref[...]
    else:
        x_ref, g_ref, sh_ref, sc_ref, w_ref, qkg_ref, o_ref = refs
    a = _rms(x_ref[...], g_ref[...]) * (1.0 + sc_ref[...]) + sh_ref[...]
    ab = a.astype(BF16)
    ones = jnp.ones((x_ref.shape[0], HEAD_DIM), BF16)
    for j in range(EVEN_IN_COLS // (2 * HEAD_DIM)):
        acc = _dot(ab, w_ref[:, j * 2 * HEAD_DIM:(j + 1) * 2 * HEAD_DIM])
        for hh in range(2):
            slot, kind = _even_head(2 * j + hh)
            v = acc[:, hh * HEAD_DIM:(hh + 1) * HEAD_DIM]
            if kind == "bq":
                v = _rms(v, qkg_ref[0:1, :])
            elif kind == "bk":
                v = _rms(v, qkg_ref[1:2, :])
            if rope and kind[1] != "v":
                v = v * cos + pltpu.roll(v, HEAD_DIM // 2, 1) * sin
            if kind[1] == "q":
                v = v * scale
            o_ref[:, slot * HEAD_DIM:(slot + 1) * HEAD_DIM] = v.astype(BF16)
            if kind[1] == "v":
                o_ref[:, (slot + 1) * HEAD_DIM:(slot + 2) * HEAD_DIM] = ones


def _even_inproj(h, g, shift, scale_m, w_bf, qk_gain, rope_tabs, row_fn, seq):
    r, d = h.shape
    tm = _tile(seq, 512)
    rope = rope_tabs is not None
    in_specs = [
        pl.BlockSpec((tm, d), lambda i: (i, 0)),
        _const_spec((1, d)),
        _mod_spec(d, lambda i: row_fn(i * tm)),
        _mod_spec(d, lambda i: row_fn(i * tm)),
        _const_spec((d, EVEN_IN_COLS)),
        _const_spec((2, HEAD_DIM)),
    ]
    args = [h, g, shift, scale_m, w_bf, qk_gain]
    if rope:
        nt = seq // tm
        in_specs += [pl.BlockSpec((tm, HEAD_DIM), lambda i: (i % nt, 0))] * 2
        args += list(rope_tabs)
    return pl.pallas_call(
        functools.partial(_even_inproj_kernel, rope=rope, scale=HEAD_DIM ** -0.5 * LOG2E),
        grid=(r // tm,),
        in_specs=in_specs,
        out_specs=pl.BlockSpec((tm, EVEN_COLS), lambda i: (i, 0)),
        out_shape=jax.ShapeDtypeStruct((r, EVEN_COLS), BF16),
        compiler_params=_params("arbitrary"),
        name="even_inproj",
    )(*args)


def _rope_pad(v, c, s1, s2):
    return v * c + pltpu.roll(v, LANES - MLA_ROPE // 2, 1) * s1 + pltpu.roll(v, MLA_ROPE // 2, 1) * s2


def _mla_proj_kernel(*refs, rope, want_q, scale):
    refs = list(refs)
    x_ref, g_ref, sh_ref, sc_ref, win_ref, wkr_ref, qn_ref, kvn_ref = refs[:8]
    refs = refs[8:]
    if want_q:
        wuq_ref = refs.pop(0)
    wukv_ref = refs.pop(0)
    if rope:
        c, s1, s2 = refs[0][...], refs[1][...], refs[2][...]
        refs = refs[3:]
    if want_q:
        q_ref = refs.pop(0)
    k_ref, v_ref = refs
    a = _rms(x_ref[...], g_ref[...]) * (1.0 + sc_ref[...]) + sh_ref[...]
    ab = a.astype(BF16)
    low = _dot(ab, win_ref[...])
    kr = _dot(ab, wkr_ref[...])
    if rope:
        kr = _rope_pad(kr, c, s1, s2)
    krb = kr.astype(BF16)
    if want_q:
        cq = _rms(low[:, :MLA_Q_RANK], qn_ref[...]).astype(BF16)
        for h in range(MLA_HEADS):
            acc = _dot(cq, wuq_ref[:, h * MLA_QK_PAD:(h + 1) * MLA_QK_PAD])
            qr = acc[:, MLA_NOPE:]
            if rope:
                qr = _rope_pad(qr, c, s1, s2)
            q_ref[:, h * MLA_QK_PAD:h * MLA_QK_PAD + MLA_NOPE] = (acc[:, :MLA_NOPE] * scale).astype(BF16)
            q_ref[:, h * MLA_QK_PAD + MLA_NOPE:(h + 1) * MLA_QK_PAD] = (qr * scale).astype(BF16)
    ckv = _rms(low[:, MLA_Q_RANK:], kvn_ref[...]).astype(BF16)
    hw = MLA_NOPE + MLA_V
    for h in range(MLA_HEADS):
        acc = _dot(ckv, wukv_ref[:, h * hw:(h + 1) * hw])
        k_ref[:, h * MLA_QK_PAD:h * MLA_QK_PAD + MLA_NOPE] = acc[:, :MLA_NOPE].astype(BF16)
        k_ref[:, h * MLA_QK_PAD + MLA_NOPE:(h + 1) * MLA_QK_PAD] = krb
        v_ref[:, 2 * h * MLA_V:(2 * h + 1) * MLA_V] = acc[:, MLA_NOPE:].astype(BF16)
        v_ref[:, (2 * h + 1) * MLA_V:(2 * h + 2) * MLA_V] = jnp.ones((x_ref.shape[0], MLA_V), BF16)


def _mla_proj(h, g, shift, scale_m, w_in_bf, w_kr_bf, q_norm, kv_norm, w_uq_bf, w_ukv_bf, rope_tabs,
              row_fn, seq, want_q):
    r, d = h.shape
    tm = _tile(seq, 512)
    rope = rope_tabs is not None
    in_specs = [
        pl.BlockSpec((tm, d), lambda i: (i, 0)),
        _const_spec((1, d)),
        _mod_spec(d, lambda i: row_fn(i * tm)),
        _mod_spec(d, lambda i: row_fn(i * tm)),
        _const_spec(w_in_bf.shape),
        _const_spec(w_kr_bf.shape),
        _const_spec((1, MLA_Q_RANK)),
        _const_spec((1, MLA_KV_RANK)),
    ]
    args = [h, g, shift, scale_m, w_in_bf, w_kr_bf, q_norm, kv_norm]
    if want_q:
        in_specs.append(_const_spec(w_uq_bf.shape))
        args.append(w_uq_bf)
    in_specs.append(_const_spec(w_ukv_bf.shape))
    args.append(w_ukv_bf)
    if rope:
        nt = seq // tm
        in_specs += [pl.BlockSpec((tm, LANES), lambda i: (i % nt, 0))] * 3
        args += list(rope_tabs)
    kcols = MLA_HEADS * MLA_QK_PAD
    vcols = MLA_HEADS * 2 * MLA_V
    out_specs = [pl.BlockSpec((tm, kcols), lambda i: (i, 0)), pl.BlockSpec((tm, vcols), lambda i: (i, 0))]
    out_shape = [jax.ShapeDtypeStruct((r, kcols), BF16), jax.ShapeDtypeStruct((r, vcols), BF16)]
    if want_q:
        out_specs.insert(0, pl.BlockSpec((tm, kcols), lambda i: (i, 0)))
        out_shape.insert(0, jax.ShapeDtypeStruct((r, kcols), BF16))
    return pl.pallas_call(
        functools.partial(_mla_proj_kernel, rope=rope, want_q=want_q,
                          scale=(MLA_NOPE + MLA_ROPE) ** -0.5 * LOG2E),
        grid=(r // tm,),
        in_specs=in_specs,
        out_specs=out_specs,
        out_shape=out_shape,
        compiler_params=_params("arbitrary"),
        name="mla_proj",
    )(*args)


def _attn_kernel(*refs, n_seg, band, use_sink, tq, seq, hp, group, dq, dv, tk):
    refs = list(refs)
    if use_sink:
        sink_ref = refs.pop(0)
    q_ref = refs.pop(0)
    o_ref = refs.pop()
    dvx = 2 * dv
    if band:
        wk = tq + 2 * WINDOW
        q0 = pl.program_id(2) * tq
        start = pl.multiple_of(jnp.clip(q0 - WINDOW, 0, seq - wk), LANES)
        dist = (lax.broadcasted_iota(I32, (tq, wk), 1) - lax.broadcasted_iota(I32, (tq, wk), 0)) + (start - q0)
        valid = jnp.abs(dist) <= WINDOW
    for j in range(hp):
        kv = j // group
        q = q_ref[:, j * dq:(j + 1) * dq]
        m = None
        acc = None
        for s_i in range(n_seg):
            k_ref, v_ref = refs[2 * s_i], refs[2 * s_i + 1]
            slen = k_ref.shape[0]
            masked = band and s_i == n_seg - 1
            chunks = [(start, wk)] if masked else [(c0, min(tk, slen - c0)) for c0 in range(0, slen, tk)]
            for c0, cl in chunks:
                s = _dot_nt(q, k_ref[pl.ds(c0, cl), kv * dq:(kv + 1) * dq])
                if masked:
                    s = jnp.where(valid, s, -jnp.inf)
                v = v_ref[pl.ds(c0, cl), kv * dvx:(kv + 1) * dvx]
                mc = s.max(axis=-1, keepdims=True)
                if m is None:
                    m = mc
                    acc = _dot(jnp.exp2(s - m).astype(BF16), v)
                else:
                    m_new = jnp.maximum(m, mc)
                    acc = jnp.exp2(m - m_new) * acc + _dot(jnp.exp2(s - m_new).astype(BF16), v)
                    m = m_new
        den = acc[:, dv:]
        if use_sink:
            sink = sink_ref[pl.program_id(1) * hp + j] * LOG2E
            den = den + jnp.exp2(sink - m)
        o_ref[:, j * dv:(j + 1) * dv] = (acc[:, :dv] / den).astype(o_ref.dtype)


def _attention(q_arr, q_col0, segs, sink, *, batch, seq, n_heads, group, dq, dv, band, tq_cap, hp, tk=512):
    tq = _tile(seq, tq_cap)
    nq = seq // tq
    use_sink = sink is not None
    if hp <= group:
        assert group % hp == 0
        kvp, kv_of, step_group = 1, (lambda hg: hg // (group // hp)), hp
    else:
        assert hp % group == 0
        kvp, kv_of, step_group = hp // group, (lambda hg: hg), group
    qw, kw, vw = hp * dq, kvp * dq, kvp * 2 * dv
    assert q_col0 % qw == 0
    in_specs, args = [], []
    if use_sink:
        in_specs.append(pl.BlockSpec(memory_space=pltpu.SMEM))
        args.append(sink)
    in_specs.append(pl.BlockSpec((tq, qw), lambda b, hg, i: (b * nq + i, q_col0 // qw + hg)))
    args.append(q_arr)
    for k_arr, k_col0, v_arr, v_col0, slen in segs:
        assert k_col0 % kw == 0 and v_col0 % vw == 0
        in_specs.append(pl.BlockSpec((slen, kw), lambda b, hg, i, c=k_col0 // kw: (b, c + kv_of(hg))))
        in_specs.append(pl.BlockSpec((slen, vw), lambda b, hg, i, c=v_col0 // vw: (b, c + kv_of(hg))))
        args += [k_arr, v_arr]
    return pl.pallas_call(
        functools.partial(_attn_kernel, n_seg=len(segs), band=band, use_sink=use_sink, tq=tq, seq=seq, hp=hp,
                          group=step_group, dq=dq, dv=dv, tk=tk),
        grid=(batch, n_heads // hp, nq),
        in_specs=in_specs,
        out_specs=pl.BlockSpec((tq, hp * dv), lambda b, hg, i: (b * nq + i, hg)),
        out_shape=jax.ShapeDtypeStruct((batch * seq, n_heads * dv), BF16),
        compiler_params=_params("arbitrary", "arbitrary", "arbitrary"),
        name="attention",
    )(*args)


def _outproj_kernel(*refs, n_in):
    o_refs = refs[:n_in]
    (w_ref, h_ref, g1_ref, gate_ref, g2_ref, sh_ref, sc_ref, wr_ref, hn_ref, xf_ref, aff_ref) = refs[n_in:]
    y = None
    k0 = 0
    for o_ref in o_refs:
        kk = o_ref.shape[1]
        part = _dot(o_ref[...], w_ref[k0:k0 + kk, :])
        y = part if y is None else y + part
        k0 += kk
    hn = h_ref[...] + gate_ref[...] * _rms(y, g1_ref[...])
    hn_ref[...] = hn
    xf = _rms(hn, g2_ref[...]) * (1.0 + sc_ref[...]) + sh_ref[...]
    xf_hi, xf_lo = _split_bf16(xf)
    xf_ref[...] = xf_hi
    wr_hi, wr_lo = _split_bf16(wr_ref[...])
    logits = _dot_nt(wr_hi, xf_hi) + (_dot_nt(wr_hi, xf_lo) + _dot_nt(wr_lo, xf_hi))
    e = jnp.exp(logits - logits.max(axis=0, keepdims=True))
    aff_ref[...] = e / e.sum(axis=0, keepdims=True)


def _outproj(o_list, w_bf, h, g1, gate, g2, shift, scale_m, wr_t, row_fn, seq):
    r, d = h.shape
    tm = _tile(seq, 512)
    n_e = wr_t.shape[0]
    in_specs = [pl.BlockSpec((tm, o.shape[1]), lambda i: (i, 0)) for o in o_list]
    in_specs += [
        _const_spec(w_bf.shape),
        pl.BlockSpec((tm, d), lambda i: (i, 0)),
        _const_spec((1, d)),
        _mod_spec(d, lambda i: row_fn(i * tm)),
        _const_spec((1, d)),
        _mod_spec(d, lambda i: row_fn(i * tm)),
        _mod_spec(d, lambda i: row_fn(i * tm)),
        _const_spec((n_e, d)),
    ]
    return pl.pallas_call(
        functools.partial(_outproj_kernel, n_in=len(o_list)),
        grid=(r // tm,),
        in_specs=in_specs,
        out_specs=[
            pl.BlockSpec((tm, d), lambda i: (i, 0)),
            pl.BlockSpec((tm, d), lambda i: (i, 0)),
            pl.BlockSpec((n_e, tm), lambda i: (0, i)),
        ],
        out_shape=[
            jax.ShapeDtypeStruct((r, d), F32),
            jax.ShapeDtypeStruct((r, d), BF16),
            jax.ShapeDtypeStruct((n_e, r), F32),
        ],
        compiler_params=_params("arbitrary"),
        name="outproj",
    )(*o_list, w_bf, h, g1, gate, g2, shift, scale_m, wr_t)


def _route_kernel(aff_ref, posm_ref, pose_ref, gsel_ref, *, cap):
    n_e, n = aff_ref.shape
    aff = aff_ref[...]
    capf = float(cap)
    floor = float(jnp.finfo(F32).tiny)

    def search(_, carry):
        lo, hi = carry
        mid = jnp.sqrt(jnp.maximum(lo, floor) * hi)
        cnt = jnp.sum(jnp.where(aff >= mid, 1.0, 0.0), axis=1, keepdims=True)
        ok = cnt >= capf
        return jnp.where(ok, mid, lo), jnp.where(ok, hi, mid)

    lo0 = jnp.zeros((n_e, 1), F32)
    hi0 = jnp.full((n_e, 1), 2.0, F32)
    lo, hi = lax.fori_loop(0, ROUTE_SEARCH_STEPS, search, (lo0, hi0))
    need = capf - jnp.sum(jnp.where(aff >= hi, 1.0, 0.0), axis=1, keepdims=True)
    upper = jnp.where(lax.broadcasted_iota(I32, (LANES, LANES), 0) < lax.broadcasted_iota(I32, (LANES, LANES), 1),
                      1.0, 0.0).astype(BF16)
    run_eq = jnp.zeros((n_e, 1), F32)
    run_sel = jnp.zeros((n_e, 1), F32)
    for j in range(n // LANES):
        sl = slice(j * LANES, (j + 1) * LANES)
        a = aff_ref[:, sl]
        above = a >= hi
        tie = (a >= lo) & (a < hi)
        eq = jnp.where(tie, 1.0, 0.0)
        rank = _dot(eq.astype(BF16), upper) + run_eq
        run_eq = run_eq + eq.sum(axis=1, keepdims=True)
        sel = above | (tie & (rank < need))
        self_f = jnp.where(sel, 1.0, 0.0)
        pos = _dot(self_f.astype(BF16), upper) + run_sel
        run_sel = run_sel + self_f.sum(axis=1, keepdims=True)
        pos_i = pos.astype(I32)
        pose_ref[0, :, sl] = pos_i
        posm_ref[0, :, sl] = jnp.where(sel, pos_i, -1)
        gsel_ref[0, :, sl] = jnp.where(sel, a, 0.0)


def _route(aff, batch, n, cap):
    n_e = aff.shape[0]
    spec = pl.BlockSpec((1, n_e, n), lambda b: (b, 0, 0))
    return pl.pallas_call(
        functools.partial(_route_kernel, cap=cap),
        grid=(batch,),
        in_specs=[pl.BlockSpec((n_e, n), lambda b: (0, b))],
        out_specs=[spec, spec, spec],
        out_shape=[
            jax.ShapeDtypeStruct((batch, n_e, n), I32),
            jax.ShapeDtypeStruct((batch, n_e, n), I32),
            jax.ShapeDtypeStruct((batch, n_e, n), F32),
        ],
        compiler_params=_params("arbitrary"),
        name="route",
    )(aff)


def _gather_kernel(cs_ref, posm_ref, gsel_ref, x_ref, xg_ref, gs_ref, acc_ref, gacc_ref, *, win, tc, n_chunks,
                   n_e, cap):
    b, e = pl.program_id(0), pl.program_id(1)
    base = (b * n_e + e) * (n_chunks + 1)
    acc_ref[...] = jnp.zeros_like(acc_ref)
    gacc_ref[...] = jnp.zeros_like(gacc_ref)
    row = lax.broadcasted_iota(I32, (win, tc), 0)

    def fill(c, a0, w):
        first = a0 + w * win
        start = pl.multiple_of(jnp.minimum(first, cap - win), F32_SUBLANES)
        slot = start + row
        hit = (posm_ref[0, 0, c:c + 1, :] == slot) & (slot >= first)
        onehot = jnp.where(hit, 1.0, 0.0).astype(BF16)
        acc_ref[pl.ds(start, win), :] += _dot(onehot, x_ref[c * tc:(c + 1) * tc, :])
        gacc_ref[pl.ds(start, win), :] += jnp.where(hit, gsel_ref[0, 0, c:c + 1, :], 0.0).sum(
            axis=1, keepdims=True)

    starts = [(cs_ref[base + c] // F32_SUBLANES) * F32_SUBLANES for c in range(n_chunks)]
    for c in range(n_chunks):
        fill(c, starts[c], 0)
    for c in range(n_chunks):
        n_win = (cs_ref[base + c + 1] - starts[c] + win - 1) // win

        def more(w, carry, c=c):
            fill(c, starts[c], w)
            return carry

        lax.fori_loop(1, n_win, more, 0)

    xg_ref[0] = acc_ref[...].astype(BF16)
    gs_ref[0] = gacc_ref[...]


def _gather(xf, posm, pose, gsel, batch, n, cap):
    d = xf.shape[1]
    n_e = posm.shape[1]
    tc = _tile(n, 512)
    win = min(cap, LANES)
    n_chunks = n // tc
    cstart = jnp.concatenate([pose[:, :, ::tc], jnp.full((batch, n_e, 1), cap, I32)], axis=-1).reshape(-1)
    posm4 = posm.reshape(batch, n_e, n_chunks, tc)
    gsel4 = gsel.reshape(batch, n_e, n_chunks, tc)
    grid_spec = pltpu.PrefetchScalarGridSpec(
        num_scalar_prefetch=1,
        grid=(batch, n_e),
        in_specs=[
            pl.BlockSpec((1, 1, n_chunks, tc), lambda b, e, cs: (b, e, 0, 0)),
            pl.BlockSpec((1, 1, n_chunks, tc), lambda b, e, cs: (b, e, 0, 0)),
            pl.BlockSpec((n, d), lambda b, e, cs: (b, 0)),
        ],
        out_specs=[
            pl.BlockSpec((1, cap, d), lambda b, e, cs: (e, b, 0)),
            pl.BlockSpec((1, cap, 1), lambda b, e, cs: (e, b, 0)),
        ],
        scratch_shapes=[pltpu.VMEM((cap, d), F32), pltpu.VMEM((cap, 1), F32)],
    )
    return pl.pallas_call(
        functools.partial(_gather_kernel, win=win, tc=tc, n_chunks=n_chunks, n_e=n_e, cap=cap),
        grid_spec=grid_spec,
        out_shape=[
            jax.ShapeDtypeStruct((n_e, batch * cap, d), BF16),
            jax.ShapeDtypeStruct((n_e, batch * cap, 1), F32),
        ],
        compiler_params=_params("arbitrary", "arbitrary"),
        name="moe_gather",
    )(cstart, posm4, gsel4, xf)


def _ffn_kernel(*refs, with_ctx):
    if with_ctx:
        xl_ref, xc_ref, wg_ref, wu_ref, wd_ref, gl_ref, gc_ref, yl_ref, yc_ref, accl_ref, accc_ref = refs
    else:
        xl_ref, wg_ref, wu_ref, wd_ref, gl_ref, yl_ref, accl_ref = refs
    i, f = pl.program_id(1), pl.program_id(2)
    last_f = pl.num_programs(2) - 1
    tf = wg_ref.shape[2]
    fw = min(tf, FFN_SUB)

    def run(x_ref, gs_ref, y_ref, acc_ref):
        @pl.when(f == 0)
        def _():
            acc_ref[...] = jnp.zeros_like(acc_ref)

        x = x_ref[0]
        hid = []
        for c0 in range(0, tf, fw):
            hg = _dot(x, wg_ref[0, :, c0:c0 + fw].astype(BF16))
            hu = _dot(x, wu_ref[0, :, c0:c0 + fw].astype(BF16))
            hid.append((_silu(hg) * hu).astype(BF16))
        acc_ref[...] += _dot(jnp.concatenate(hid, axis=1), wd_ref[0].astype(BF16))

        @pl.when(f == last_f)
        def _():
            y_ref[0] = (acc_ref[...] * gs_ref[0]).astype(BF16)

    run(xl_ref, gl_ref, yl_ref, accl_ref)
    if with_ctx:
        @pl.when(i == pl.num_programs(1) - 1)
        def _():
            run(xc_ref, gc_ref, yc_ref, accc_ref)


def _expert_ffn(layer, w_gate, w_up, w_down, xg, gslot, xg_c=None, gslot_c=None):
    n_e, m, d = xg.shape
    ff = w_gate.shape[3]
    tm = _tile(m, 1024)
    tf = _tile(ff, 512)
    with_ctx = xg_c is not None
    x_spec = pl.BlockSpec((1, tm, d), lambda e, i, f: (e, i, 0))
    g_spec = pl.BlockSpec((1, tm, 1), lambda e, i, f: (e, i, 0))
    w_specs = [
        pl.BlockSpec((None, 1, d, tf), lambda e, i, f: (layer, e, 0, f)),
        pl.BlockSpec((None, 1, d, tf), lambda e, i, f: (layer, e, 0, f)),
        pl.BlockSpec((None, 1, tf, d), lambda e, i, f: (layer, e, f, 0)),
    ]
    out_specs = [x_spec]
    out_shape = [jax.ShapeDtypeStruct((n_e, m, d), BF16)]
    scratch = [pltpu.VMEM((tm, d), F32)]
    if with_ctx:
        mc = xg_c.shape[1]
        xc_spec = pl.BlockSpec((1, mc, d), lambda e, i, f: (e, 0, 0))
        gc_spec = pl.BlockSpec((1, mc, 1), lambda e, i, f: (e, 0, 0))
        in_specs = [x_spec, xc_spec] + w_specs + [g_spec, gc_spec]
        args = [xg, xg_c, w_gate, w_up, w_down, gslot, gslot_c]
        out_specs.append(xc_spec)
        out_shape.append(jax.ShapeDtypeStruct((n_e, mc, d), BF16))
        scratch.append(pltpu.VMEM((mc, d), F32))
    else:
        in_specs = [x_spec] + w_specs + [g_spec]
        args = [xg, w_gate, w_up, w_down, gslot]
    return pl.pallas_call(
        functools.partial(_ffn_kernel, with_ctx=with_ctx),
        grid=(n_e, m // tm, ff // tf),
        in_specs=in_specs,
        out_specs=out_specs,
        out_shape=out_shape,
        scratch_shapes=scratch,
        compiler_params=_params("arbitrary", "arbitrary", "arbitrary"),
        name="expert_ffn",
    )(*args)


def _combine_kernel(cs_ref, posm_ref, y_ref, h_ref, g_ref, gate_ref, o_ref, *, tt, n_sub, win, cap, n_e,
                    n_tiles):
    b, i = pl.program_id(0), pl.program_id(1)
    for sub in range(n_sub):
        cols = slice(sub * tt, (sub + 1) * tt)
        acc = None
        for e in range(n_e):
            if cap <= win:
                a0 = 0
                yw = y_ref[e]
            else:
                s0 = cs_ref[(b * n_e + e) * n_tiles + i * n_sub + sub]
                a0 = jnp.minimum((s0 // BF16_SUBLANES) * BF16_SUBLANES, cap - win)
                a0 = pl.multiple_of(a0, BF16_SUBLANES)
                yw = y_ref[e, pl.ds(a0, win), :]
            wn = yw.shape[0]
            slot = a0 + lax.broadcasted_iota(I32, (wn, tt), 0)
            onehot = jnp.where(posm_ref[0, e:e + 1, cols] == slot, 1.0, 0.0).astype(BF16)
            part = lax.dot_general(onehot, yw, (((0,), (0,)), ((), ())), preferred_element_type=F32)
            acc = part if acc is None else acc + part
        o_ref[cols, :] = h_ref[cols, :] + gate_ref[...] * _rms(acc, g_ref[...])


def _combine(y, posm, pose, h, g, gate, row_fn, batch, n, cap):
    n_e = y.shape[0]
    d = y.shape[2]
    tt = LANES
    win = tt + BF16_SUBLANES
    n_tiles = n // tt
    n_sub = 2 if n_tiles % 2 == 0 else 1
    n_steps = n_tiles // n_sub
    cstart = pose[:, :, ::tt].reshape(-1)
    grid_spec = pltpu.PrefetchScalarGridSpec(
        num_scalar_prefetch=1,
        grid=(batch, n_steps),
        in_specs=[
            pl.BlockSpec((1, n_e, n_sub * tt), lambda b, i, cs: (b, 0, i)),
            pl.BlockSpec((n_e, cap, d), lambda b, i, cs: (0, b, 0), pipeline_mode=pl.Buffered(1)),
            pl.BlockSpec((n_sub * tt, d), lambda b, i, cs: (b * n_steps + i, 0)),
            pl.BlockSpec((1, d), lambda b, i, cs: (0, 0)),
            pl.BlockSpec((None, 1, d), lambda b, i, cs: (row_fn(b * n), 0, 0)),
        ],
        out_specs=pl.BlockSpec((n_sub * tt, d), lambda b, i, cs: (b * n_steps + i, 0)),
    )
    return pl.pallas_call(
        functools.partial(_combine_kernel, tt=tt, n_sub=n_sub, win=win, cap=cap, n_e=n_e, n_tiles=n_tiles),
        grid_spec=grid_spec,
        out_shape=jax.ShapeDtypeStruct(h.shape, F32),
        compiler_params=_params("arbitrary", "arbitrary"),
        name="moe_combine",
    )(cstart, posm, y, h, g, gate)


def _moe_dispatch(xf, aff, batch, n):
    cap = CAPACITY_FACTOR * n // N_EXPERTS
    posm, pose, gsel = _route(aff, batch, n, cap)
    xg, gslot = _gather(xf, posm, pose, gsel, batch, n, cap)
    return xg, gslot, (posm, pose, cap)


def _rope_angles(seq, rot_dim):
    n_rows = seq // GRID_W
    row = jnp.repeat(jnp.arange(n_rows), GRID_W)
    col = jnp.tile(jnp.arange(GRID_W), n_rows)
    n_freq = rot_dim // 4
    inv = ROPE_BASE ** (-jnp.arange(n_freq, dtype=F32) / n_freq)
    ang = jnp.concatenate([row[:, None] * inv, col[:, None] * inv], axis=-1)
    return jnp.cos(ang), jnp.sin(ang)


def kernel(x, c, ctx, c_ctx, mod_w, mod_b, norm_g, ev_w_in, ev_w_out, ev_sink, ev_qk_norm, od_w_in, od_q_norm,
           od_kv_norm, od_w_uq, od_w_ukv, od_w_out, router_w, exp_w_gate, exp_w_up, exp_w_down):
    batch, seq, d = x.shape
    n_ctx = ctx.shape[1]
    depth = mod_w.shape[0]
    assert batch < MOD_ROWS and seq % LANES == 0 and n_ctx % LANES == 0 and seq >= 4 * WINDOW

    cs = jnp.concatenate([c, c_ctx[None, :], jnp.zeros((MOD_ROWS - batch - 1, d), F32)], axis=0)
    mods = _modulation(cs, mod_w, mod_b).reshape(depth, MOD_ROWS, N_MOD, 1, d)

    def lat_row(r):
        return r // seq

    def ctx_row(r):
        return batch

    cos_h, sin_h = _rope_angles(seq, HEAD_DIM)
    rope_even = (jnp.concatenate([cos_h, cos_h], axis=-1), jnp.concatenate([-sin_h, sin_h], axis=-1))
    cos_m, sin_m = _rope_angles(seq, MLA_ROPE)
    half = MLA_ROPE // 2
    zeros = lambda w: jnp.zeros((seq, w), F32)
    rope_mla = (
        jnp.concatenate([cos_m, cos_m, zeros(LANES - MLA_ROPE)], axis=-1),
        jnp.concatenate([-sin_m, zeros(LANES - half)], axis=-1),
        jnp.concatenate([zeros(half), sin_m, zeros(LANES - MLA_ROPE)], axis=-1),
    )

    h_lat = x.reshape(batch * seq, d)
    h_ctx = ctx.reshape(batch * n_ctx, d)
    for layer in range(depth):
        with_ctx = layer < depth - 1
        i = layer // 2
        g = norm_g[layer].reshape(4, 1, d)
        m = [mods[layer, :, k] for k in range(N_MOD)]
        wr_t = router_w[layer].T
        if layer % 2 == 0:
            w_in = ev_w_in[i].astype(BF16)
            w_out = ev_w_out[i].astype(BF16)
            qkv_l = _even_inproj(h_lat, g[0], m[0], m[1], w_in, ev_qk_norm[i], rope_even, lat_row, seq)
            qkv_c = _even_inproj(h_ctx, g[0], m[0], m[1], w_in, ev_qk_norm[i], None, ctx_row, n_ctx)
            ga, gb = A_HEADS // A_KV_HEADS, B_HEADS // B_KV_HEADS
            hd = HEAD_DIM
            a_kw = dict(batch=batch, n_heads=A_HEADS, group=ga, dq=hd, dv=hd)
            b_kw = dict(batch=batch, n_heads=B_HEADS, group=gb, dq=hd, dv=hd, band=False)
            a_ctx = (qkv_c, AK0 * hd, qkv_c, AV0 * hd, n_ctx)
            b_ctx = (qkv_c, BK0 * hd, qkv_c, BV0 * hd, n_ctx)
            o_a = _attention(qkv_l, AQ0 * hd, [a_ctx, (qkv_l, AK0 * hd, qkv_l, AV0 * hd, seq)], ev_sink[i],
                             seq=seq, band=True, tq_cap=512, hp=4, **a_kw)
            o_b = _attention(qkv_l, BQ0 * hd, [b_ctx, (qkv_l, BK0 * hd, qkv_l, BV0 * hd, seq)], None,
                             seq=seq, tq_cap=512, hp=4, **b_kw)
            o_lat = [o_a, o_b]
            if with_ctx:
                o_ac = _attention(qkv_c, AQ0 * hd, [a_ctx], ev_sink[i], seq=n_ctx, band=False, tq_cap=512,
                                  hp=4, **a_kw)
                o_bc = _attention(qkv_c, BQ0 * hd, [b_ctx], None, seq=n_ctx, tq_cap=512, hp=4, **b_kw)
                o_ctx = [o_ac, o_bc]
        else:
            w_in = od_w_in[i]
            w_qkv = w_in[:, :MLA_Q_RANK + MLA_KV_RANK].astype(BF16)
            w_kr = jnp.pad(w_in[:, MLA_Q_RANK + MLA_KV_RANK:], ((0, 0), (0, LANES - MLA_ROPE))).astype(BF16)
            w_uq = od_w_uq[i].reshape(MLA_Q_RANK, MLA_HEADS, MLA_NOPE + MLA_ROPE)
            w_uq = jnp.pad(w_uq, ((0, 0), (0, 0), (0, MLA_QK_PAD - MLA_NOPE - MLA_ROPE)))
            w_uq = w_uq.reshape(MLA_Q_RANK, MLA_HEADS * MLA_QK_PAD).astype(BF16)
            w_ukv = od_w_ukv[i].astype(BF16)
            w_out = od_w_out[i].astype(BF16)
            qn, kvn = od_q_norm[i].reshape(1, -1), od_kv_norm[i].reshape(1, -1)
            q_l, k_l, v_l = _mla_proj(h_lat, g[0], m[0], m[1], w_qkv, w_kr, qn, kvn, w_uq, w_ukv, rope_mla,
                                      lat_row, seq, True)
            proj_c = _mla_proj(h_ctx, g[0], m[0], m[1], w_qkv, w_kr, qn, kvn, w_uq, w_ukv, None, ctx_row,
                               n_ctx, with_ctx)
            k_c, v_c = proj_c[-2], proj_c[-1]
            m_kw = dict(batch=batch, n_heads=MLA_HEADS, group=1, dq=MLA_QK_PAD, dv=MLA_V, band=False, tq_cap=512,
                        hp=4, tk=1024)
            o_lat = [_attention(q_l, 0, [(k_c, 0, v_c, 0, n_ctx), (k_l, 0, v_l, 0, seq)], None, seq=seq, **m_kw)]
            if with_ctx:
                o_ctx = [_attention(proj_c[0], 0, [(k_c, 0, v_c, 0, n_ctx)], None, seq=n_ctx, **m_kw)]
        ew = (layer, exp_w_gate, exp_w_up, exp_w_down)
        h_lat, xf, aff = _outproj(o_lat, w_out, h_lat, g[1], m[2], g[2], m[3], m[4], wr_t, lat_row, seq)
        xg, gslot, (posm, pose, cap) = _moe_dispatch(xf, aff, batch, seq)
        if with_ctx:
            h_ctx, xf_c, aff_c = _outproj(o_ctx, w_out, h_ctx, g[1], m[2], g[2], m[3], m[4], wr_t, ctx_row,
                                          n_ctx)
            xg_c, gslot_c, (posm_c, pose_c, cap_c) = _moe_dispatch(xf_c, aff_c, batch, n_ctx)
            y, y_c = _expert_ffn(*ew, xg, gslot, xg_c, gslot_c)
            h_ctx = _combine(y_c, posm_c, pose_c, h_ctx, g[3], m[5], ctx_row, batch, n_ctx, cap_c)
        else:
            (y,) = _expert_ffn(*ew, xg, gslot)
        h_lat = _combine(y, posm, pose, h_lat, g[3], m[5], lat_row, batch, seq, cap)
    return h_lat.reshape(batch, seq, d)
```

```python
import functools

import jax
import jax.numpy as jnp
from jax import lax
from jax.experimental import pallas as pl
from jax.experimental.pallas import tpu as pltpu

F32 = jnp.float32
BF16 = jnp.bfloat16
I32 = jnp.int32

EPS = 1e-6
GRID_W = 64
WINDOW = 128
ROPE_BASE = 10000.0
HEAD_DIM = 128
A_HEADS = 8
A_KV_HEADS = 2
B_HEADS = 8
B_KV_HEADS = 2
MLA_HEADS = 16
MLA_Q_RANK = 512
MLA_KV_RANK = 512
MLA_NOPE = 128
MLA_ROPE = 64
MLA_V = 128
MLA_QK_PAD = 256
N_EXPERTS = 16
CAPACITY_FACTOR = 2
N_MOD = 6
MOD_ROWS = 8

LANES = 128
BF16_SUBLANES = 16
F32_SUBLANES = 8
VMEM_LIMIT_BYTES = 63 * 1024 * 1024 + 512 * 1024
FFN_SUB = 256
ROUTE_SEARCH_STEPS = 64

LOG2E = 1.4426950408889634

EVEN_IN_COLS = (A_HEADS + 2 * A_KV_HEADS + B_HEADS + 2 * B_KV_HEADS) * HEAD_DIM
AQ0 = 0
BQ0 = AQ0 + A_HEADS
AK0 = BQ0 + B_HEADS
BK0 = AK0 + A_KV_HEADS
AV0 = BK0 + B_KV_HEADS
BV0 = AV0 + 2 * A_KV_HEADS
EVEN_SLOTS = BV0 + 2 * B_KV_HEADS
EVEN_COLS = EVEN_SLOTS * HEAD_DIM
_W_AK0 = A_HEADS
_W_AV0 = _W_AK0 + A_KV_HEADS
_W_BQ0 = _W_AV0 + A_KV_HEADS
_W_BK0 = _W_BQ0 + B_HEADS
_W_BV0 = _W_BK0 + B_KV_HEADS


def _even_head(wh):
    if wh < _W_AK0:
        return AQ0 + wh, "aq"
    if wh < _W_AV0:
        return AK0 + wh - _W_AK0, "ak"
    if wh < _W_BQ0:
        return AV0 + 2 * (wh - _W_AV0), "av"
    if wh < _W_BK0:
        return BQ0 + wh - _W_BQ0, "bq"
    if wh < _W_BV0:
        return BK0 + wh - _W_BK0, "bk"
    return BV0 + 2 * (wh - _W_BV0), "bv"


def _params(*sem):
    return pltpu.CompilerParams(dimension_semantics=sem, vmem_limit_bytes=VMEM_LIMIT_BYTES)


def _tile(n, cap):
    t = min(n, cap)
    while n % t:
        t -= 1
    return t


def _const_spec(shape):
    nd = len(shape)
    return pl.BlockSpec(shape, lambda *_: (0,) * nd)


def _rms(x, g):
    return x * lax.rsqrt(jnp.mean(x * x, axis=-1, keepdims=True) + EPS) * g


def _silu(x):
    return x / (1.0 + jnp.exp(-x))


def _dot(a, b):
    return jnp.dot(a, b, preferred_element_type=F32)


def _dot_nt(a, b):
    return lax.dot_general(a, b, (((1,), (1,)), ((), ())), preferred_element_type=F32)


def _split_bf16(x):
    hi = x.astype(BF16)
    return hi, (x - hi.astype(F32)).astype(BF16)


def _mod_kernel(cs_ref, w_ref, b_ref, o_ref):
    s_hi, s_lo = _split_bf16(_silu(cs_ref[...]))
    w_hi, w_lo = _split_bf16(w_ref[0])
    o_ref[0] = _dot(s_hi, w_hi) + (_dot(s_hi, w_lo) + _dot(s_lo, w_hi)) + b_ref[0]


def _modulation(cs, mod_w, mod_b):
    depth, d, n6 = mod_w.shape
    tn = _tile(n6, 1024)
    return pl.pallas_call(
        _mod_kernel,
        grid=(depth, n6 // tn),
        in_specs=[
            _const_spec((MOD_ROWS, d)),
            pl.BlockSpec((1, d, tn), lambda l, j: (l, 0, j)),
            pl.BlockSpec((1, 1, tn), lambda l, j: (l, 0, j)),
        ],
        out_specs=pl.BlockSpec((1, MOD_ROWS, tn), lambda l, j: (l, 0, j)),
        out_shape=jax.ShapeDtypeStruct((depth, MOD_ROWS, n6), F32),
        compiler_params=_params("arbitrary", "arbitrary"),
        name="modulation",
    )(cs, mod_w, mod_b.reshape(depth, 1, n6))


def _mod_spec(d, row_fn):
    return pl.BlockSpec((None, 1, d), lambda i: (row_fn(i), 0, 0))


def _even_inproj_kernel(*refs, rope, scale):
    if rope:
        x_ref, g_ref, sh_ref, sc_ref, w_ref, qkg_ref, cos_ref, sin_ref, o_ref = refs
        cos, sin = cos_ref[...], sin_ref[...]
    else:
        x_ref, g_ref, sh_ref, sc_ref, w_ref, qkg_ref, o_ref = refs
    a = _rms(x_ref[...], g_ref[...]) * (1.0 + sc_ref[...]) + sh_ref[...]
    ab = a.astype(BF16)
    ones = jnp.ones((x_ref.shape[0], HEAD_DIM), BF16)
    for j in range(EVEN_IN_COLS // (2 * HEAD_DIM)):
        acc = _dot(ab, w_ref[:, j * 2 * HEAD_DIM:(j + 1) * 2 * HEAD_DIM])
        for hh in range(2):
            slot, kind = _even_head(2 * j + hh)
            v = acc[:, hh * HEAD_DIM:(hh + 1) * HEAD_DIM]
            if kind == "bq":
                v = _rms(v, qkg_ref[0:1, :])
            elif kind == "bk":
                v = _rms(v, qkg_ref[1:2, :])
            if rope and kind[1] != "v":
                v = v * cos + pltpu.roll(v, HEAD_DIM // 2, 1) * sin
            if kind[1] == "q":
                v = v * scale
            o_ref[:, slot * HEAD_DIM:(slot + 1) * HEAD_DIM] = v.astype(BF16)
            if kind[1] == "v":
                o_ref[:, (slot + 1) * HEAD_DIM:(slot + 2) * HEAD_DIM] = ones


def _even_inproj(h, g, shift, scale_m, w_bf, qk_gain, rope_tabs, row_fn, seq):
    r, d = h.shape
    tm = _tile(seq, 512)
    rope = rope_tabs is not None
    in_specs = [
        pl.BlockSpec((tm, d), lambda i: (i, 0)),
        _const_spec((1, d)),
        _mod_spec(d, lambda i: row_fn(i * tm)),
        _mod_spec(d, lambda i: row_fn(i * tm)),
        _const_spec((d, EVEN_IN_COLS)),
        _const_spec((2, HEAD_DIM)),
    ]
    args = [h, g, shift, scale_m, w_bf, qk_gain]
    if rope:
        nt = seq // tm
        in_specs += [pl.BlockSpec((tm, HEAD_DIM), lambda i: (i % nt, 0))] * 2
        args += list(rope_tabs)
    return pl.pallas_call(
        functools.partial(_even_inproj_kernel, rope=rope, scale=HEAD_DIM ** -0.5 * LOG2E),
        grid=(r // tm,),
        in_specs=in_specs,
        out_specs=pl.BlockSpec((tm, EVEN_COLS), lambda i: (i, 0)),
        out_shape=jax.ShapeDtypeStruct((r, EVEN_COLS), BF16),
        compiler_params=_params("arbitrary"),
        name="even_inproj",
    )(*args)


def _rope_pad(v, c, s1, s2):
    return v * c + pltpu.roll(v, LANES - MLA_ROPE // 2, 1) * s1 + pltpu.roll(v, MLA_ROPE // 2, 1) * s2


def _mla_proj_kernel(*refs, rope, want_q, scale):
    refs = list(refs)
    x_ref, g_ref, sh_ref, sc_ref, win_ref, wkr_ref, qn_ref, kvn_ref = refs[:8]
    refs = refs[8:]
    if want_q:
        wuq_ref = refs.pop(0)
    wukv_ref = refs.pop(0)
    if rope:
        c, s1, s2 = refs[0][...], refs[1][...], refs[2][...]
        refs = refs[3:]
    if want_q:
        q_ref = refs.pop(0)
    k_ref, v_ref = refs
    a = _rms(x_ref[...], g_ref[...]) * (1.0 + sc_ref[...]) + sh_ref[...]
    ab = a.astype(BF16)
    low = _dot(ab, win_ref[...])
    kr = _dot(ab, wkr_ref[...])
    if rope:
        kr = _rope_pad(kr, c, s1, s2)
    krb = kr.astype(BF16)
    if want_q:
        cq = _rms(low[:, :MLA_Q_RANK], qn_ref[...]).astype(BF16)
        for h in range(MLA_HEADS):
            acc = _dot(cq, wuq_ref[:, h * MLA_QK_PAD:(h + 1) * MLA_QK_PAD])
            qr = acc[:, MLA_NOPE:]
            if rope:
                qr = _rope_pad(qr, c, s1, s2)
            q_ref[:, h * MLA_QK_PAD:h * MLA_QK_PAD + MLA_NOPE] = (acc[:, :MLA_NOPE] * scale).astype(BF16)
            q_ref[:, h * MLA_QK_PAD + MLA_NOPE:(h + 1) * MLA_QK_PAD] = (qr * scale).astype(BF16)
    ckv = _rms(low[:, MLA_Q_RANK:], kvn_ref[...]).astype(BF16)
    hw = MLA_NOPE + MLA_V
    for h in range(MLA_HEADS):
        acc = _dot(ckv, wukv_ref[:, h * hw:(h + 1) * hw])
        k_ref[:, h * MLA_QK_PAD:h * MLA_QK_PAD + MLA_NOPE] = acc[:, :MLA_NOPE].astype(BF16)
        k_ref[:, h * MLA_QK_PAD + MLA_NOPE:(h + 1) * MLA_QK_PAD] = krb
        v_ref[:, 2 * h * MLA_V:(2 * h + 1) * MLA_V] = acc[:, MLA_NOPE:].astype(BF16)
        v_ref[:, (2 * h + 1) * MLA_V:(2 * h + 2) * MLA_V] = jnp.ones((x_ref.shape[0], MLA_V), BF16)


def _mla_proj(h, g, shift, scale_m, w_in_bf, w_kr_bf, q_norm, kv_norm, w_uq_bf, w_ukv_bf, rope_tabs,
              row_fn, seq, want_q):
    r, d = h.shape
    tm = _tile(seq, 512)
    rope = rope_tabs is not None
    in_specs = [
        pl.BlockSpec((tm, d), lambda i: (i, 0)),
        _const_spec((1, d)),
        _mod_spec(d, lambda i: row_fn(i * tm)),
        _mod_spec(d, lambda i: row_fn(i * tm)),
        _const_spec(w_in_bf.shape),
        _const_spec(w_kr_bf.shape),
        _const_spec((1, MLA_Q_RANK)),
        _const_spec((1, MLA_KV_RANK)),
    ]
    args = [h, g, shift, scale_m, w_in_bf, w_kr_bf, q_norm, kv_norm]
    if want_q:
        in_specs.append(_const_spec(w_uq_bf.shape))
        args.append(w_uq_bf)
    in_specs.append(_const_spec(w_ukv_bf.shape))
    args.append(w_ukv_bf)
    if rope:
        nt = seq // tm
        in_specs += [pl.BlockSpec((tm, LANES), lambda i: (i % nt, 0))] * 3
        args += list(rope_tabs)
    kcols = MLA_HEADS * MLA_QK_PAD
    vcols = MLA_HEADS * 2 * MLA_V
    out_specs = [pl.BlockSpec((tm, kcols), lambda i: (i, 0)), pl.BlockSpec((tm, vcols), lambda i: (i, 0))]
    out_shape = [jax.ShapeDtypeStruct((r, kcols), BF16), jax.ShapeDtypeStruct((r, vcols), BF16)]
    if want_q:
        out_specs.insert(0, pl.BlockSpec((tm, kcols), lambda i: (i, 0)))
        out_shape.insert(0, jax.ShapeDtypeStruct((r, kcols), BF16))
    return pl.pallas_call(
        functools.partial(_mla_proj_kernel, rope=rope, want_q=want_q,
                          scale=(MLA_NOPE + MLA_ROPE) ** -0.5 * LOG2E),
        grid=(r // tm,),
        in_specs=in_specs,
        out_specs=out_specs,
        out_shape=out_shape,
        compiler_params=_params("arbitrary"),
        name="mla_proj",
    )(*args)


def _attn_kernel(*refs, n_seg, band, use_sink, tq, seq, hp, group, dq, dv, tk):
    refs = list(refs)
    if use_sink:
        sink_ref = refs.pop(0)
    q_ref = refs.pop(0)
    o_ref = refs.pop()
    dvx = 2 * dv
    if band:
        wk = tq + 2 * WINDOW
        q0 = pl.program_id(2) * tq
        start = pl.multiple_of(jnp.clip(q0 - WINDOW, 0, seq - wk), LANES)
        dist = (lax.broadcasted_iota(I32, (tq, wk), 1) - lax.broadcasted_iota(I32, (tq, wk), 0)) + (start - q0)
        valid = jnp.abs(dist) <= WINDOW
    for j in range(hp):
        kv = j // group
        q = q_ref[:, j * dq:(j + 1) * dq]
        m = None
        acc = None
        for s_i in range(n_seg):
            k_ref, v_ref = refs[2 * s_i], refs[2 * s_i + 1]
            slen = k_ref.shape[0]
            masked = band and s_i == n_seg - 1
            chunks = [(start, wk)] if masked else [(c0, min(tk, slen - c0)) for c0 in range(0, slen, tk)]
            for c0, cl in chunks:
                s = _dot_nt(q, k_ref[pl.ds(c0, cl), kv * dq:(kv + 1) * dq])
                if masked:
                    s = jnp.where(valid, s, -jnp.inf)
                v = v_ref[pl.ds(c0, cl), kv * dvx:(kv + 1) * dvx]
                mc = s.max(axis=-1, keepdims=True)
                if m is None:
                    m = mc
                    acc = _dot(jnp.exp2(s - m).astype(BF16), v)
                else:
                    m_new = jnp.maximum(m, mc)
                    acc = jnp.exp2(m - m_new) * acc + _dot(jnp.exp2(s - m_new).astype(BF16), v)
                    m = m_new
        den = acc[:, dv:]
        if use_sink:
            sink = sink_ref[pl.program_id(1) * hp + j] * LOG2E
            den = den + jnp.exp2(sink - m)
        o_ref[:, j * dv:(j + 1) * dv] = (acc[:, :dv] / den).astype(o_ref.dtype)


def _attention(q_arr, q_col0, segs, sink, *, batch, seq, n_heads, group, dq, dv, band, tq_cap, hp, tk=512):
    tq = _tile(seq, tq_cap)
    nq = seq // tq
    use_sink = sink is not None
    if hp <= group:
        assert group % hp == 0
        kvp, kv_of, step_group = 1, (lambda hg: hg // (group // hp)), hp
    else:
        assert hp % group == 0
        kvp, kv_of, step_group = hp // group, (lambda hg: hg), group
    qw, kw, vw = hp * dq, kvp * dq, kvp * 2 * dv
    assert q_col0 % qw == 0
    in_specs, args = [], []
    if use_sink:
        in_specs.append(pl.BlockSpec(memory_space=pltpu.SMEM))
        args.append(sink)
    in_specs.append(pl.BlockSpec((tq, qw), lambda b, hg, i: (b * nq + i, q_col0 // qw + hg)))
    args.append(q_arr)
    for k_arr, k_col0, v_arr, v_col0, slen in segs:
        assert k_col0 % kw == 0 and v_col0 % vw == 0
        in_specs.append(pl.BlockSpec((slen, kw), lambda b, hg, i, c=k_col0 // kw: (b, c + kv_of(hg))))
        in_specs.append(pl.BlockSpec((slen, vw), lambda b, hg, i, c=v_col0 // vw: (b, c + kv_of(hg))))
        args += [k_arr, v_arr]
    return pl.pallas_call(
        functools.partial(_attn_kernel, n_seg=len(segs), band=band, use_sink=use_sink, tq=tq, seq=seq, hp=hp,
                          group=step_group, dq=dq, dv=dv, tk=tk),
        grid=(batch, n_heads // hp, nq),
        in_specs=in_specs,
        out_specs=pl.BlockSpec((tq, hp * dv), lambda b, hg, i: (b * nq + i, hg)),
        out_shape=jax.ShapeDtypeStruct((batch * seq, n_heads * dv), BF16),
        compiler_params=_params("arbitrary", "arbitrary", "arbitrary"),
        name="attention",
    )(*args)


def _outproj_kernel(*refs, n_in):
    o_refs = refs[:n_in]
    (w_ref, h_ref, g1_ref, gate_ref, g2_ref, sh_ref, sc_ref, wr_ref, hn_ref, xf_ref, aff_ref) = refs[n_in:]
    y = None
    k0 = 0
    for o_ref in o_refs:
        kk = o_ref.shape[1]
        part = _dot(o_ref[...], w_ref[k0:k0 + kk, :])
        y = part if y is None else y + part
        k0 += kk
    hn = h_ref[...] + gate_ref[...] * _rms(y, g1_ref[...])
    hn_ref[...] = hn
    xf = _rms(hn, g2_ref[...]) * (1.0 + sc_ref[...]) + sh_ref[...]
    xf_hi, xf_lo = _split_bf16(xf)
    xf_ref[...] = xf_hi
    wr_hi, wr_lo = _split_bf16(wr_ref[...])
    logits = _dot_nt(wr_hi, xf_hi) + (_dot_nt(wr_hi, xf_lo) + _dot_nt(wr_lo, xf_hi))
    e = jnp.exp(logits - logits.max(axis=0, keepdims=True))
    aff_ref[...] = e / e.sum(axis=0, keepdims=True)


def _outproj(o_list, w_bf, h, g1, gate, g2, shift, scale_m, wr_t, row_fn, seq):
    r, d = h.shape
    tm = _tile(seq, 512)
    n_e = wr_t.shape[0]
    in_specs = [pl.BlockSpec((tm, o.shape[1]), lambda i: (i, 0)) for o in o_list]
    in_specs += [
        _const_spec(w_bf.shape),
        pl.BlockSpec((tm, d), lambda i: (i, 0)),
        _const_spec((1, d)),
        _mod_spec(d, lambda i: row_fn(i * tm)),
        _const_spec((1, d)),
        _mod_spec(d, lambda i: row_fn(i * tm)),
        _mod_spec(d, lambda i: row_fn(i * tm)),
        _const_spec((n_e, d)),
    ]
    return pl.pallas_call(
        functools.partial(_outproj_kernel, n_in=len(o_list)),
        grid=(r // tm,),
        in_specs=in_specs,
        out_specs=[
            pl.BlockSpec((tm, d), lambda i: (i, 0)),
            pl.BlockSpec((tm, d), lambda i: (i, 0)),
            pl.BlockSpec((n_e, tm), lambda i: (0, i)),
        ],
        out_shape=[
            jax.ShapeDtypeStruct((r, d), F32),
            jax.ShapeDtypeStruct((r, d), BF16),
            jax.ShapeDtypeStruct((n_e, r), F32),
        ],
        compiler_params=_params("arbitrary"),
        name="outproj",
    )(*o_list, w_bf, h, g1, gate, g2, shift, scale_m, wr_t)


def _route_kernel(aff_ref, posm_ref, pose_ref, gsel_ref, *, cap):
    n_e, n = aff_ref.shape
    aff = aff_ref[...]
    capf = float(cap)
    floor = float(jnp.finfo(F32).tiny)

    def search(_, carry):
        lo, hi = carry
        mid = jnp.sqrt(jnp.maximum(lo, floor) * hi)
        cnt = jnp.sum(jnp.where(aff >= mid, 1.0, 0.0), axis=1, keepdims=True)
        ok = cnt >= capf
        return jnp.where(ok, mid, lo), jnp.where(ok, hi, mid)

    lo0 = jnp.zeros((n_e, 1), F32)
    hi0 = jnp.full((n_e, 1), 2.0, F32)
    lo, hi = lax.fori_loop(0, ROUTE_SEARCH_STEPS, search, (lo0, hi0))
    need = capf - jnp.sum(jnp.where(aff >= hi, 1.0, 0.0), axis=1, keepdims=True)
    upper = jnp.where(lax.broadcasted_iota(I32, (LANES, LANES), 0) < lax.broadcasted_iota(I32, (LANES, LANES), 1),
                      1.0, 0.0).astype(BF16)
    run_eq = jnp.zeros((n_e, 1), F32)
    run_sel = jnp.zeros((n_e, 1), F32)
    for j in range(n // LANES):
        sl = slice(j * LANES, (j + 1) * LANES)
        a = aff_ref[:, sl]
        above = a >= hi
        tie = (a >= lo) & (a < hi)
        eq = jnp.where(tie, 1.0, 0.0)
        rank = _dot(eq.astype(BF16), upper) + run_eq
        run_eq = run_eq + eq.sum(axis=1, keepdims=True)
        sel = above | (tie & (rank < need))
        self_f = jnp.where(sel, 1.0, 0.0)
        pos = _dot(self_f.astype(BF16), upper) + run_sel
        run_sel = run_sel + self_f.sum(axis=1, keepdims=True)
        pos_i = pos.astype(I32)
        pose_ref[0, :, sl] = pos_i
        posm_ref[0, :, sl] = jnp.where(sel, pos_i, -1)
        gsel_ref[0, :, sl] = jnp.where(sel, a, 0.0)


def _route(aff, batch, n, cap):
    n_e = aff.shape[0]
    spec = pl.BlockSpec((1, n_e, n), lambda b: (b, 0, 0))
    return pl.pallas_call(
        functools.partial(_route_kernel, cap=cap),
        grid=(batch,),
        in_specs=[pl.BlockSpec((n_e, n), lambda b: (0, b))],
        out_specs=[spec, spec, spec],
        out_shape=[
            jax.ShapeDtypeStruct((batch, n_e, n), I32),
            jax.ShapeDtypeStruct((batch, n_e, n), I32),
            jax.ShapeDtypeStruct((batch, n_e, n), F32),
        ],
        compiler_params=_params("arbitrary"),
        name="route",
    )(aff)


def _gather_kernel(cs_ref, posm_ref, gsel_ref, x_ref, xg_ref, gs_ref, acc_ref, gacc_ref, *, win, tc, n_chunks,
                   n_e, cap):
    b, e = pl.program_id(0), pl.program_id(1)
    base = (b * n_e + e) * (n_chunks + 1)
    acc_ref[...] = jnp.zeros_like(acc_ref)
    gacc_ref[...] = jnp.zeros_like(gacc_ref)
    row = lax.broadcasted_iota(I32, (win, tc), 0)

    def fill(c, a0, w):
        first = a0 + w * win
        start = pl.multiple_of(jnp.minimum(first, cap - win), F32_SUBLANES)
        slot = start + row
        hit = (posm_ref[0, 0, c:c + 1, :] == slot) & (slot >= first)
        onehot = jnp.where(hit, 1.0, 0.0).astype(BF16)
        acc_ref[pl.ds(start, win), :] += _dot(onehot, x_ref[c * tc:(c + 1) * tc, :])
        gacc_ref[pl.ds(start, win), :] += jnp.where(hit, gsel_ref[0, 0, c:c + 1, :], 0.0).sum(
            axis=1, keepdims=True)

    starts = [(cs_ref[base + c] // F32_SUBLANES) * F32_SUBLANES for c in range(n_chunks)]
    for c in range(n_chunks):
        fill(c, starts[c], 0)
    for c in range(n_chunks):
        n_win = (cs_ref[base + c + 1] - starts[c] + win - 1) // win

        def more(w, carry, c=c):
            fill(c, starts[c], w)
            return carry

        lax.fori_loop(1, n_win, more, 0)

    xg_ref[0] = acc_ref[...].astype(BF16)
    gs_ref[0] = gacc_ref[...]


def _gather(xf, posm, pose, gsel, batch, n, cap):
    d = xf.shape[1]
    n_e = posm.shape[1]
    tc = _tile(n, 512)
    win = min(cap, LANES)
    n_chunks = n // tc
    cstart = jnp.concatenate([pose[:, :, ::tc], jnp.full((batch, n_e, 1), cap, I32)], axis=-1).reshape(-1)
    posm4 = posm.reshape(batch, n_e, n_chunks, tc)
    gsel4 = gsel.reshape(batch, n_e, n_chunks, tc)
    grid_spec = pltpu.PrefetchScalarGridSpec(
        num_scalar_prefetch=1,
        grid=(batch, n_e),
        in_specs=[
            pl.BlockSpec((1, 1, n_chunks, tc), lambda b, e, cs: (b, e, 0, 0)),
            pl.BlockSpec((1, 1, n_chunks, tc), lambda b, e, cs: (b, e, 0, 0)),
            pl.BlockSpec((n, d), lambda b, e, cs: (b, 0)),
        ],
        out_specs=[
            pl.BlockSpec((1, cap, d), lambda b, e, cs: (e, b, 0)),
            pl.BlockSpec((1, cap, 1), lambda b, e, cs: (e, b, 0)),
        ],
        scratch_shapes=[pltpu.VMEM((cap, d), F32), pltpu.VMEM((cap, 1), F32)],
    )
    return pl.pallas_call(
        functools.partial(_gather_kernel, win=win, tc=tc, n_chunks=n_chunks, n_e=n_e, cap=cap),
        grid_spec=grid_spec,
        out_shape=[
            jax.ShapeDtypeStruct((n_e, batch * cap, d), BF16),
            jax.ShapeDtypeStruct((n_e, batch * cap, 1), F32),
        ],
        compiler_params=_params("arbitrary", "arbitrary"),
        name="moe_gather",
    )(cstart, posm4, gsel4, xf)


def _ffn_kernel(*refs, with_ctx):
    if with_ctx:
        xl_ref, xc_ref, wg_ref, wu_ref, wd_ref, gl_ref, gc_ref, yl_ref, yc_ref, acc_ref, xcat_ref = refs
    else:
        xl_ref, wg_ref, wu_ref, wd_ref, gl_ref, yl_ref, acc_ref = refs
    i, f = pl.program_id(1), pl.program_id(2)
    last_f = pl.num_programs(2) - 1
    tf = wg_ref.shape[2]
    fw = min(tf, FFN_SUB)
    tm = xl_ref.shape[1]

    def run(x, rows, finish):
        @pl.when(f == 0)
        def _():
            acc_ref[:rows, :] = jnp.zeros((rows, acc_ref.shape[1]), F32)

        hid = []
        for c0 in range(0, tf, fw):
            hg = _dot(x, wg_ref[0, :, c0:c0 + fw].astype(BF16))
            hu = _dot(x, wu_ref[0, :, c0:c0 + fw].astype(BF16))
            hid.append((_silu(hg) * hu).astype(BF16))
        acc_ref[:rows, :] += _dot(jnp.concatenate(hid, axis=1), wd_ref[0].astype(BF16))
        pl.when(f == last_f)(finish)

    def finish_lat():
        yl_ref[0] = (acc_ref[:tm, :] * gl_ref[0]).astype(BF16)

    if not with_ctx:
        run(xl_ref[0], tm, finish_lat)
        return

    rows_cat = xcat_ref.shape[0]
    last_i = pl.num_programs(1) - 1

    def finish_both():
        finish_lat()
        yc_ref[0] = (acc_ref[tm:rows_cat, :] * gc_ref[0]).astype(BF16)

    @pl.when(i < last_i)
    def _():
        run(xl_ref[0], tm, finish_lat)

    @pl.when(i == last_i)
    def _():
        @pl.when(f == 0)
        def _():
            xcat_ref[:tm, :] = xl_ref[0]
            xcat_ref[tm:, :] = xc_ref[0]

        run(xcat_ref[...], rows_cat, finish_both)


def _expert_ffn(layer, w_gate, w_up, w_down, xg, gslot, xg_c=None, gslot_c=None):
    n_e, m, d = xg.shape
    ff = w_gate.shape[3]
    tm = _tile(m, 1024)
    tf = _tile(ff, 512)
    with_ctx = xg_c is not None
    x_spec = pl.BlockSpec((1, tm, d), lambda e, i, f: (e, i, 0))
    g_spec = pl.BlockSpec((1, tm, 1), lambda e, i, f: (e, i, 0))
    w_specs = [
        pl.BlockSpec((None, 1, d, tf), lambda e, i, f: (layer, e, 0, f)),
        pl.BlockSpec((None, 1, d, tf), lambda e, i, f: (layer, e, 0, f)),
        pl.BlockSpec((None, 1, tf, d), lambda e, i, f: (layer, e, f, 0)),
    ]
    out_specs = [x_spec]
    out_shape = [jax.ShapeDtypeStruct((n_e, m, d), BF16)]
    mc = xg_c.shape[1] if with_ctx else 0
    scratch = [pltpu.VMEM((tm + mc, d), F32)]
    if with_ctx:
        xc_spec = pl.BlockSpec((1, mc, d), lambda e, i, f: (e, 0, 0))
        gc_spec = pl.BlockSpec((1, mc, 1), lambda e, i, f: (e, 0, 0))
        in_specs = [x_spec, xc_spec] + w_specs + [g_spec, gc_spec]
        args = [xg, xg_c, w_gate, w_up, w_down, gslot, gslot_c]
        out_specs.append(xc_spec)
        out_shape.append(jax.ShapeDtypeStruct((n_e, mc, d), BF16))
        scratch.append(pltpu.VMEM((tm + mc, d), BF16))
    else:
        in_specs = [x_spec] + w_specs + [g_spec]
        args = [xg, w_gate, w_up, w_down, gslot]
    return pl.pallas_call(
        functools.partial(_ffn_kernel, with_ctx=with_ctx),
        grid=(n_e, m // tm, ff // tf),
        in_specs=in_specs,
        out_specs=out_specs,
        out_shape=out_shape,
        scratch_shapes=scratch,
        compiler_params=_params("arbitrary", "arbitrary", "arbitrary"),
        name="expert_ffn",
    )(*args)


def _combine_kernel(cs_ref, posm_ref, y_ref, h_ref, g_ref, gate_ref, o_ref, *, tt, n_sub, win, cap, n_e,
                    n_tiles):
    b, i = pl.program_id(0), pl.program_id(1)
    for sub in range(n_sub):
        cols = slice(sub * tt, (sub + 1) * tt)
        acc = None
        for e in range(n_e):
            if cap <= win:
                a0 = 0
                yw = y_ref[e]
            else:
                s0 = cs_ref[(b * n_e + e) * n_tiles + i * n_sub + sub]
                a0 = jnp.minimum((s0 // BF16_SUBLANES) * BF16_SUBLANES, cap - win)
                a0 = pl.multiple_of(a0, BF16_SUBLANES)
                yw = y_ref[e, pl.ds(a0, win), :]
            wn = yw.shape[0]
            slot = a0 + lax.broadcasted_iota(I32, (wn, tt), 0)
            onehot = jnp.where(posm_ref[0, e:e + 1, cols] == slot, 1.0, 0.0).astype(BF16)
            part = lax.dot_general(onehot, yw, (((0,), (0,)), ((), ())), preferred_element_type=F32)
            acc = part if acc is None else acc + part
        o_ref[cols, :] = h_ref[cols, :] + gate_ref[...] * _rms(acc, g_ref[...])


def _combine(y, posm, pose, h, g, gate, row_fn, batch, n, cap):
    n_e = y.shape[0]
    d = y.shape[2]
    tt = LANES
    win = tt + BF16_SUBLANES
    n_tiles = n // tt
    n_sub = 2 if n_tiles % 2 == 0 else 1
    n_steps = n_tiles // n_sub
    cstart = pose[:, :, ::tt].reshape(-1)
    grid_spec = pltpu.PrefetchScalarGridSpec(
        num_scalar_prefetch=1,
        grid=(batch, n_steps),
        in_specs=[
            pl.BlockSpec((1, n_e, n_sub * tt), lambda b, i, cs: (b, 0, i)),
            pl.BlockSpec((n_e, cap, d), lambda b, i, cs: (0, b, 0), pipeline_mode=pl.Buffered(1)),
            pl.BlockSpec((n_sub * tt, d), lambda b, i, cs: (b * n_steps + i, 0)),
            pl.BlockSpec((1, d), lambda b, i, cs: (0, 0)),
            pl.BlockSpec((None, 1, d), lambda b, i, cs: (row_fn(b * n), 0, 0)),
        ],
        out_specs=pl.BlockSpec((n_sub * tt, d), lambda b, i, cs: (b * n_steps + i, 0)),
    )
    return pl.pallas_call(
        functools.partial(_combine_kernel, tt=tt, n_sub=n_sub, win=win, cap=cap, n_e=n_e, n_tiles=n_tiles),
        grid_spec=grid_spec,
        out_shape=jax.ShapeDtypeStruct(h.shape, F32),
        compiler_params=_params("arbitrary", "arbitrary"),
        name="moe_combine",
    )(cstart, posm, y, h, g, gate)


def _moe_dispatch(xf, aff, batch, n):
    cap = CAPACITY_FACTOR * n // N_EXPERTS
    posm, pose, gsel = _route(aff, batch, n, cap)
    xg, gslot = _gather(xf, posm, pose, gsel, batch, n, cap)
    return xg, gslot, (posm, pose, cap)


def _rope_angles(seq, rot_dim):
    n_rows = seq // GRID_W
    row = jnp.repeat(jnp.arange(n_rows), GRID_W)
    col = jnp.tile(jnp.arange(GRID_W), n_rows)
    n_freq = rot_dim // 4
    inv = ROPE_BASE ** (-jnp.arange(n_freq, dtype=F32) / n_freq)
    ang = jnp.concatenate([row[:, None] * inv, col[:, None] * inv], axis=-1)
    return jnp.cos(ang), jnp.sin(ang)


def kernel(x, c, ctx, c_ctx, mod_w, mod_b, norm_g, ev_w_in, ev_w_out, ev_sink, ev_qk_norm, od_w_in, od_q_norm,
           od_kv_norm, od_w_uq, od_w_ukv, od_w_out, router_w, exp_w_gate, exp_w_up, exp_w_down):
    batch, seq, d = x.shape
    n_ctx = ctx.shape[1]
    depth = mod_w.shape[0]
    assert batch < MOD_ROWS and seq % LANES == 0 and n_ctx % LANES == 0 and seq >= 4 * WINDOW

    cs = jnp.concatenate([c, c_ctx[None, :], jnp.zeros((MOD_ROWS - batch - 1, d), F32)], axis=0)
    mods = _modulation(cs, mod_w, mod_b).reshape(depth, MOD_ROWS, N_MOD, 1, d)

    def lat_row(r):
        return r // seq

    def ctx_row(r):
        return batch

    cos_h, sin_h = _rope_angles(seq, HEAD_DIM)
    rope_even = (jnp.concatenate([cos_h, cos_h], axis=-1), jnp.concatenate([-sin_h, sin_h], axis=-1))
    cos_m, sin_m = _rope_angles(seq, MLA_ROPE)
    half = MLA_ROPE // 2
    zeros = lambda w: jnp.zeros((seq, w), F32)
    rope_mla = (
        jnp.concatenate([cos_m, cos_m, zeros(LANES - MLA_ROPE)], axis=-1),
        jnp.concatenate([-sin_m, zeros(LANES - half)], axis=-1),
        jnp.concatenate([zeros(half), sin_m, zeros(LANES - MLA_ROPE)], axis=-1),
    )

    h_lat = x.reshape(batch * seq, d)
    h_ctx = ctx.reshape(batch * n_ctx, d)
    for layer in range(depth):
        with_ctx = layer < depth - 1
        i = layer // 2
        g = norm_g[layer].reshape(4, 1, d)
        m = [mods[layer, :, k] for k in range(N_MOD)]
        wr_t = router_w[layer].T
        if layer % 2 == 0:
            w_in = ev_w_in[i].astype(BF16)
            w_out = ev_w_out[i].astype(BF16)
            qkv_l = _even_inproj(h_lat, g[0], m[0], m[1], w_in, ev_qk_norm[i], rope_even, lat_row, seq)
            qkv_c = _even_inproj(h_ctx, g[0], m[0], m[1], w_in, ev_qk_norm[i], None, ctx_row, n_ctx)
            ga, gb = A_HEADS // A_KV_HEADS, B_HEADS // B_KV_HEADS
            hd = HEAD_DIM
            a_kw = dict(batch=batch, n_heads=A_HEADS, group=ga, dq=hd, dv=hd)
            b_kw = dict(batch=batch, n_heads=B_HEADS, group=gb, dq=hd, dv=hd, band=False)
            a_ctx = (qkv_c, AK0 * hd, qkv_c, AV0 * hd, n_ctx)
            b_ctx = (qkv_c, BK0 * hd, qkv_c, BV0 * hd, n_ctx)
            o_a = _attention(qkv_l, AQ0 * hd, [a_ctx, (qkv_l, AK0 * hd, qkv_l, AV0 * hd, seq)], ev_sink[i],
                             seq=seq, band=True, tq_cap=512, hp=4, **a_kw)
            o_b = _attention(qkv_l, BQ0 * hd, [b_ctx, (qkv_l, BK0 * hd, qkv_l, BV0 * hd, seq)], None,
                             seq=seq, tq_cap=512, hp=4, **b_kw)
            o_lat = [o_a, o_b]
            if with_ctx:
                o_ac = _attention(qkv_c, AQ0 * hd, [a_ctx], ev_sink[i], seq=n_ctx, band=False, tq_cap=512,
                                  hp=4, **a_kw)
                o_bc = _attention(qkv_c, BQ0 * hd, [b_ctx], None, seq=n_ctx, tq_cap=512, hp=4, **b_kw)
                o_ctx = [o_ac, o_bc]
        else:
            w_in = od_w_in[i]
            w_qkv = w_in[:, :MLA_Q_RANK + MLA_KV_RANK].astype(BF16)
            w_kr = jnp.pad(w_in[:, MLA_Q_RANK + MLA_KV_RANK:], ((0, 0), (0, LANES - MLA_ROPE))).astype(BF16)
            w_uq = od_w_uq[i].reshape(MLA_Q_RANK, MLA_HEADS, MLA_NOPE + MLA_ROPE)
            w_uq = jnp.pad(w_uq, ((0, 0), (0, 0), (0, MLA_QK_PAD - MLA_NOPE - MLA_ROPE)))
            w_uq = w_uq.reshape(MLA_Q_RANK, MLA_HEADS * MLA_QK_PAD).astype(BF16)
            w_ukv = od_w_ukv[i].astype(BF16)
            w_out = od_w_out[i].astype(BF16)
            qn, kvn = od_q_norm[i].reshape(1, -1), od_kv_norm[i].reshape(1, -1)
            q_l, k_l, v_l = _mla_proj(h_lat, g[0], m[0], m[1], w_qkv, w_kr, qn, kvn, w_uq, w_ukv, rope_mla,
                                      lat_row, seq, True)
            proj_c = _mla_proj(h_ctx, g[0], m[0], m[1], w_qkv, w_kr, qn, kvn, w_uq, w_ukv, None, ctx_row,
                               n_ctx, with_ctx)
            k_c, v_c = proj_c[-2], proj_c[-1]
            m_kw = dict(batch=batch, n_heads=MLA_HEADS, group=1, dq=MLA_QK_PAD, dv=MLA_V, band=False, tq_cap=512,
                        hp=4, tk=1024)
            o_lat = [_attention(q_l, 0, [(k_c, 0, v_c, 0, n_ctx), (k_l, 0, v_l, 0, seq)], None, seq=seq, **m_kw)]
            if with_ctx:
                o_ctx = [_attention(proj_c[0], 0, [(k_c, 0, v_c, 0, n_ctx)], None, seq=n_ctx, **m_kw)]
        ew = (layer, exp_w_gate, exp_w_up, exp_w_down)
        h_lat, xf, aff = _outproj(o_lat, w_out, h_lat, g[1], m[2], g[2], m[3], m[4], wr_t, lat_row, seq)
        xg, gslot, (posm, pose, cap) = _moe_dispatch(xf, aff, batch, seq)
        if with_ctx:
            h_ctx, xf_c, aff_c = _outproj(o_ctx, w_out, h_ctx, g[1], m[2], g[2], m[3], m[4], wr_t, ctx_row,
                                          n_ctx)
            xg_c, gslot_c, (posm_c, pose_c, cap_c) = _moe_dispatch(xf_c, aff_c, batch, n_ctx)
            y, y_c = _expert_ffn(*ew, xg, gslot, xg_c, gslot_c)
            h_ctx = _combine(y_c, posm_c, pose_c, h_ctx, g[3], m[5], ctx_row, batch, n_ctx, cap_c)
        else:
            (y,) = _expert_ffn(*ew, xg, gslot)
        h_lat = _combine(y, posm, pose, h_lat, g[3], m[5], lat_row, batch, seq, cap)
    return h_lat.reshape(batch, seq, d)
```

```python
import functools

import jax
import jax.numpy as jnp
from jax import lax
from jax.experimental import pallas as pl
from jax.experimental.pallas import tpu as pltpu

F32 = jnp.float32
BF16 = jnp.bfloat16
I32 = jnp.int32

EPS = 1e-6
GRID_W = 64
WINDOW = 128
ROPE_BASE = 10000.0
HEAD_DIM = 128
A_HEADS = 8
A_KV_HEADS = 2
B_HEADS = 8
B_KV_HEADS = 2
MLA_HEADS = 16
MLA_Q_RANK = 512
MLA_KV_RANK = 512
MLA_NOPE = 128
MLA_ROPE = 64
MLA_V = 128
MLA_QK_PAD = 256
N_EXPERTS = 16
CAPACITY_FACTOR = 2
N_MOD = 6
MOD_ROWS = 8

LANES = 128
BF16_SUBLANES = 16
F32_SUBLANES = 8
VMEM_LIMIT_BYTES = 60 * 1024 * 1024
FFN_SUB = 512
ROUTE_SEARCH_STEPS = 64

LOG2E = 1.4426950408889634

EVEN_IN_COLS = (A_HEADS + 2 * A_KV_HEADS + B_HEADS + 2 * B_KV_HEADS) * HEAD_DIM
AQ0 = 0
BQ0 = AQ0 + A_HEADS
AK0 = BQ0 + B_HEADS
BK0 = AK0 + A_KV_HEADS
AV0 = BK0 + B_KV_HEADS
BV0 = AV0 + 2 * A_KV_HEADS
EVEN_SLOTS = BV0 + 2 * B_KV_HEADS
EVEN_COLS = EVEN_SLOTS * HEAD_DIM
_W_AK0 = A_HEADS
_W_AV0 = _W_AK0 + A_KV_HEADS
_W_BQ0 = _W_AV0 + A_KV_HEADS
_W_BK0 = _W_BQ0 + B_HEADS
_W_BV0 = _W_BK0 + B_KV_HEADS


def _even_head(wh):
    if wh < _W_AK0:
        return AQ0 + wh, "aq"
    if wh < _W_AV0:
        return AK0 + wh - _W_AK0, "ak"
    if wh < _W_BQ0:
        return AV0 + 2 * (wh - _W_AV0), "av"
    if wh < _W_BK0:
        return BQ0 + wh - _W_BQ0, "bq"
    if wh < _W_BV0:
        return BK0 + wh - _W_BK0, "bk"
    return BV0 + 2 * (wh - _W_BV0), "bv"


def _params(*sem):
    return pltpu.CompilerParams(dimension_semantics=sem, vmem_limit_bytes=VMEM_LIMIT_BYTES)


def _tile(n, cap):
    t = min(n, cap)
    while n % t:
        t -= 1
    return t


def _const_spec(shape):
    nd = len(shape)
    return pl.BlockSpec(shape, lambda *_: (0,) * nd)


def _rms(x, g):
    return x * lax.rsqrt(jnp.mean(x * x, axis=-1, keepdims=True) + EPS) * g


def _silu(x):
    return x / (1.0 + jnp.exp(-x))


def _dot(a, b):
    return jnp.dot(a, b, preferred_element_type=F32)


def _dot_nt(a, b):
    return lax.dot_general(a, b, (((1,), (1,)), ((), ())), preferred_element_type=F32)


def _split_bf16(x):
    hi = x.astype(BF16)
    return hi, (x - hi.astype(F32)).astype(BF16)


def _mod_kernel(cs_ref, w_ref, b_ref, o_ref):
    s_hi, s_lo = _split_bf16(_silu(cs_ref[...]))
    w_hi, w_lo = _split_bf16(w_ref[0])
    o_ref[0] = _dot(s_hi, w_hi) + (_dot(s_hi, w_lo) + _dot(s_lo, w_hi)) + b_ref[0]


def _modulation(cs, mod_w, mod_b):
    depth, d, n6 = mod_w.shape
    tn = _tile(n6, 1024)
    return pl.pallas_call(
        _mod_kernel,
        grid=(depth, n6 // tn),
        in_specs=[
            _const_spec((MOD_ROWS, d)),
            pl.BlockSpec((1, d, tn), lambda l, j: (l, 0, j)),
            pl.BlockSpec((1, 1, tn), lambda l, j: (l, 0, j)),
        ],
        out_specs=pl.BlockSpec((1, MOD_ROWS, tn), lambda l, j: (l, 0, j)),
        out_shape=jax.ShapeDtypeStruct((depth, MOD_ROWS, n6), F32),
        compiler_params=_params("arbitrary", "arbitrary"),
        name="modulation",
    )(cs, mod_w, mod_b.reshape(depth, 1, n6))


def _mod_spec(d, row_fn):
    return pl.BlockSpec((None, 1, d), lambda i: (row_fn(i), 0, 0))


def _even_inproj_kernel(*refs, rope, scale):
    if rope:
        x_ref, g_ref, sh_ref, sc_ref, w_ref, qkg_ref, cos_ref, sin_ref, o_ref = refs
        cos, sin = cos_ref[...], sin_ref[...]
    else:
        x_ref, g_ref, sh_ref, sc_ref, w_ref, qkg_ref, o_ref = refs
    a = _rms(x_ref[...], g_ref[...]) * (1.0 + sc_ref[...]) + sh_ref[...]
    ab = a.astype(BF16)
    ones = jnp.ones((x_ref.shape[0], HEAD_DIM), BF16)
    for j in range(EVEN_IN_COLS // (2 * HEAD_DIM)):
        acc = _dot(ab, w_ref[:, j * 2 * HEAD_DIM:(j + 1) * 2 * HEAD_DIM])
        for hh in range(2):
            slot, kind = _even_head(2 * j + hh)
            v = acc[:, hh * HEAD_DIM:(hh + 1) * HEAD_DIM]
            if kind == "bq":
                v = _rms(v, qkg_ref[0:1, :])
            elif kind == "bk":
                v = _rms(v, qkg_ref[1:2, :])
            if rope and kind[1] != "v":
                v = v * cos + pltpu.roll(v, HEAD_DIM // 2, 1) * sin
            if kind[1] == "q":
                v = v * scale
            o_ref[:, slot * HEAD_DIM:(slot + 1) * HEAD_DIM] = v.astype(BF16)
            if kind[1] == "v":
                o_ref[:, (slot + 1) * HEAD_DIM:(slot + 2) * HEAD_DIM] = ones


def _even_inproj(h, g, shift, scale_m, w_bf, qk_gain, rope_tabs, row_fn, seq):
    r, d = h.shape
    tm = _tile(seq, 512)
    rope = rope_tabs is not None
    in_specs = [
        pl.BlockSpec((tm, d), lambda i: (i, 0)),
        _const_spec((1, d)),
        _mod_spec(d, lambda i: row_fn(i * tm)),
        _mod_spec(d, lambda i: row_fn(i * tm)),
        _const_spec((d, EVEN_IN_COLS)),
        _const_spec((2, HEAD_DIM)),
    ]
    args = [h, g, shift, scale_m, w_bf, qk_gain]
    if rope:
        nt = seq // tm
        in_specs += [pl.BlockSpec((tm, HEAD_DIM), lambda i: (i % nt, 0))] * 2
        args += list(rope_tabs)
    return pl.pallas_call(
        functools.partial(_even_inproj_kernel, rope=rope, scale=HEAD_DIM ** -0.5 * LOG2E),
        grid=(r // tm,),
        in_specs=in_specs,
        out_specs=pl.BlockSpec((tm, EVEN_COLS), lambda i: (i, 0)),
        out_shape=jax.ShapeDtypeStruct((r, EVEN_COLS), BF16),
        compiler_params=_params("arbitrary"),
        name="even_inproj",
    )(*args)


def _rope_pad(v, c, s1, s2):
    return v * c + pltpu.roll(v, LANES - MLA_ROPE // 2, 1) * s1 + pltpu.roll(v, MLA_ROPE // 2, 1) * s2


def _mla_proj_kernel(*refs, rope, want_q, scale):
    refs = list(refs)
    x_ref, g_ref, sh_ref, sc_ref, win_ref, wkr_ref, qn_ref, kvn_ref = refs[:8]
    refs = refs[8:]
    if want_q:
        wuq_ref = refs.pop(0)
    wukv_ref = refs.pop(0)
    if rope:
        c, s1, s2 = refs[0][...], refs[1][...], refs[2][...]
        refs = refs[3:]
    if want_q:
        q_ref = refs.pop(0)
    k_ref, v_ref = refs
    a = _rms(x_ref[...], g_ref[...]) * (1.0 + sc_ref[...]) + sh_ref[...]
    ab = a.astype(BF16)
    low = _dot(ab, win_ref[...])
    kr = _dot(ab, wkr_ref[...])
    if rope:
        kr = _rope_pad(kr, c, s1, s2)
    krb = kr.astype(BF16)
    if want_q:
        cq = _rms(low[:, :MLA_Q_RANK], qn_ref[...]).astype(BF16)
        for h in range(MLA_HEADS):
            acc = _dot(cq, wuq_ref[:, h * MLA_QK_PAD:(h + 1) * MLA_QK_PAD])
            qr = acc[:, MLA_NOPE:]
            if rope:
                qr = _rope_pad(qr, c, s1, s2)
            q_ref[:, h * MLA_QK_PAD:h * MLA_QK_PAD + MLA_NOPE] = (acc[:, :MLA_NOPE] * scale).astype(BF16)
            q_ref[:, h * MLA_QK_PAD + MLA_NOPE:(h + 1) * MLA_QK_PAD] = (qr * scale).astype(BF16)
    ckv = _rms(low[:, MLA_Q_RANK:], kvn_ref[...]).astype(BF16)
    hw = MLA_NOPE + MLA_V
    for h in range(MLA_HEADS):
        acc = _dot(ckv, wukv_ref[:, h * hw:(h + 1) * hw])
        k_ref[:, h * MLA_QK_PAD:h * MLA_QK_PAD + MLA_NOPE] = acc[:, :MLA_NOPE].astype(BF16)
        k_ref[:, h * MLA_QK_PAD + MLA_NOPE:(h + 1) * MLA_QK_PAD] = krb
        v_ref[:, 2 * h * MLA_V:(2 * h + 1) * MLA_V] = acc[:, MLA_NOPE:].astype(BF16)
        v_ref[:, (2 * h + 1) * MLA_V:(2 * h + 2) * MLA_V] = jnp.ones((x_ref.shape[0], MLA_V), BF16)


def _mla_proj(h, g, shift, scale_m, w_in_bf, w_kr_bf, q_norm, kv_norm, w_uq_bf, w_ukv_bf, rope_tabs,
              row_fn, seq, want_q):
    r, d = h.shape
    tm = _tile(seq, 512)
    rope = rope_tabs is not None
    in_specs = [
        pl.BlockSpec((tm, d), lambda i: (i, 0)),
        _const_spec((1, d)),
        _mod_spec(d, lambda i: row_fn(i * tm)),
        _mod_spec(d, lambda i: row_fn(i * tm)),
        _const_spec(w_in_bf.shape),
        _const_spec(w_kr_bf.shape),
        _const_spec((1, MLA_Q_RANK)),
        _const_spec((1, MLA_KV_RANK)),
    ]
    args = [h, g, shift, scale_m, w_in_bf, w_kr_bf, q_norm, kv_norm]
    if want_q:
        in_specs.append(_const_spec(w_uq_bf.shape))
        args.append(w_uq_bf)
    in_specs.append(_const_spec(w_ukv_bf.shape))
    args.append(w_ukv_bf)
    if rope:
        nt = seq // tm
        in_specs += [pl.BlockSpec((tm, LANES), lambda i: (i % nt, 0))] * 3
        args += list(rope_tabs)
    kcols = MLA_HEADS * MLA_QK_PAD
    vcols = MLA_HEADS * 2 * MLA_V
    out_specs = [pl.BlockSpec((tm, kcols), lambda i: (i, 0)), pl.BlockSpec((tm, vcols), lambda i: (i, 0))]
    out_shape = [jax.ShapeDtypeStruct((r, kcols), BF16), jax.ShapeDtypeStruct((r, vcols), BF16)]
    if want_q:
        out_specs.insert(0, pl.BlockSpec((tm, kcols), lambda i: (i, 0)))
        out_shape.insert(0, jax.ShapeDtypeStruct((r, kcols), BF16))
    return pl.pallas_call(
        functools.partial(_mla_proj_kernel, rope=rope, want_q=want_q,
                          scale=(MLA_NOPE + MLA_ROPE) ** -0.5 * LOG2E),
        grid=(r // tm,),
        in_specs=in_specs,
        out_specs=out_specs,
        out_shape=out_shape,
        compiler_params=_params("arbitrary"),
        name="mla_proj",
    )(*args)


def _attn_kernel(*refs, n_seg, band, use_sink, tq, seq, hp, group, dq, dv, tk):
    refs = list(refs)
    if use_sink:
        sink_ref = refs.pop(0)
    q_ref = refs.pop(0)
    o_ref = refs.pop()
    dvx = 2 * dv
    if band:
        wk = tq + 2 * WINDOW
        q0 = pl.program_id(2) * tq
        start = pl.multiple_of(jnp.clip(q0 - WINDOW, 0, seq - wk), LANES)
        dist = (lax.broadcasted_iota(I32, (tq, wk), 1) - lax.broadcasted_iota(I32, (tq, wk), 0)) + (start - q0)
        valid = jnp.abs(dist) <= WINDOW
    for j in range(hp):
        kv = j // group
        q = q_ref[:, j * dq:(j + 1) * dq]
        m = None
        acc = None
        for s_i in range(n_seg):
            k_ref, v_ref = refs[2 * s_i], refs[2 * s_i + 1]
            slen = k_ref.shape[0]
            masked = band and s_i == n_seg - 1
            chunks = [(start, wk)] if masked else [(c0, min(tk, slen - c0)) for c0 in range(0, slen, tk)]
            for c0, cl in chunks:
                s = _dot_nt(q, k_ref[pl.ds(c0, cl), kv * dq:(kv + 1) * dq])
                if masked:
                    s = jnp.where(valid, s, -jnp.inf)
                v = v_ref[pl.ds(c0, cl), kv * dvx:(kv + 1) * dvx]
                mc = s.max(axis=-1, keepdims=True)
                if m is None:
                    m = mc
                    acc = _dot(jnp.exp2(s - m).astype(BF16), v)
                else:
                    m_new = jnp.maximum(m, mc)
                    acc = jnp.exp2(m - m_new) * acc + _dot(jnp.exp2(s - m_new).astype(BF16), v)
                    m = m_new
        den = acc[:, dv:]
        if use_sink:
            sink = sink_ref[pl.program_id(1) * hp + j] * LOG2E
            den = den + jnp.exp2(sink - m)
        o_ref[:, j * dv:(j + 1) * dv] = (acc[:, :dv] / den).astype(o_ref.dtype)


def _attention(q_arr, q_col0, segs, sink, *, batch, seq, n_heads, group, dq, dv, band, tq_cap, hp, tk=512):
    tq = _tile(seq, tq_cap)
    nq = seq // tq
    use_sink = sink is not None
    if hp <= group:
        assert group % hp == 0
        kvp, kv_of, step_group = 1, (lambda hg: hg // (group // hp)), hp
    else:
        assert hp % group == 0
        kvp, kv_of, step_group = hp // group, (lambda hg: hg), group
    qw, kw, vw = hp * dq, kvp * dq, kvp * 2 * dv
    assert q_col0 % qw == 0
    in_specs, args = [], []
    if use_sink:
        in_specs.append(pl.BlockSpec(memory_space=pltpu.SMEM))
        args.append(sink)
    in_specs.append(pl.BlockSpec((tq, qw), lambda b, hg, i: (b * nq + i, q_col0 // qw + hg)))
    args.append(q_arr)
    for k_arr, k_col0, v_arr, v_col0, slen in segs:
        assert k_col0 % kw == 0 and v_col0 % vw == 0
        in_specs.append(pl.BlockSpec((slen, kw), lambda b, hg, i, c=k_col0 // kw: (b, c + kv_of(hg))))
        in_specs.append(pl.BlockSpec((slen, vw), lambda b, hg, i, c=v_col0 // vw: (b, c + kv_of(hg))))
        args += [k_arr, v_arr]
    return pl.pallas_call(
        functools.partial(_attn_kernel, n_seg=len(segs), band=band, use_sink=use_sink, tq=tq, seq=seq, hp=hp,
                          group=step_group, dq=dq, dv=dv, tk=tk),
        grid=(batch, n_heads // hp, nq),
        in_specs=in_specs,
        out_specs=pl.BlockSpec((tq, hp * dv), lambda b, hg, i: (b * nq + i, hg)),
        out_shape=jax.ShapeDtypeStruct((batch * seq, n_heads * dv), BF16),
        compiler_params=_params("arbitrary", "arbitrary", "arbitrary"),
        name="attention",
    )(*args)


def _outproj_kernel(*refs, n_in):
    o_refs = refs[:n_in]
    (w_ref, h_ref, g1_ref, gate_ref, g2_ref, sh_ref, sc_ref, wr_ref, hn_ref, xf_ref, aff_ref) = refs[n_in:]
    y = None
    k0 = 0
    for o_ref in o_refs:
        kk = o_ref.shape[1]
        part = _dot(o_ref[...], w_ref[k0:k0 + kk, :])
        y = part if y is None else y + part
        k0 += kk
    hn = h_ref[...] + gate_ref[...] * _rms(y, g1_ref[...])
    hn_ref[...] = hn
    xf = _rms(hn, g2_ref[...]) * (1.0 + sc_ref[...]) + sh_ref[...]
    xf_hi, xf_lo = _split_bf16(xf)
    xf_ref[...] = xf_hi
    wr_hi, wr_lo = _split_bf16(wr_ref[...])
    logits = _dot_nt(wr_hi, xf_hi) + (_dot_nt(wr_hi, xf_lo) + _dot_nt(wr_lo, xf_hi))
    e = jnp.exp(logits - logits.max(axis=0, keepdims=True))
    aff_ref[...] = e / e.sum(axis=0, keepdims=True)


def _outproj(o_list, w_bf, h, g1, gate, g2, shift, scale_m, wr_t, row_fn, seq):
    r, d = h.shape
    tm = _tile(seq, 512)
    n_e = wr_t.shape[0]
    in_specs = [pl.BlockSpec((tm, o.shape[1]), lambda i: (i, 0)) for o in o_list]
    in_specs += [
        _const_spec(w_bf.shape),
        pl.BlockSpec((tm, d), lambda i: (i, 0)),
        _const_spec((1, d)),
        _mod_spec(d, lambda i: row_fn(i * tm)),
        _const_spec((1, d)),
        _mod_spec(d, lambda i: row_fn(i * tm)),
        _mod_spec(d, lambda i: row_fn(i * tm)),
        _const_spec((n_e, d)),
    ]
    return pl.pallas_call(
        functools.partial(_outproj_kernel, n_in=len(o_list)),
        grid=(r // tm,),
        in_specs=in_specs,
        out_specs=[
            pl.BlockSpec((tm, d), lambda i: (i, 0)),
            pl.BlockSpec((tm, d), lambda i: (i, 0)),
            pl.BlockSpec((n_e, tm), lambda i: (0, i)),
        ],
        out_shape=[
            jax.ShapeDtypeStruct((r, d), F32),
            jax.ShapeDtypeStruct((r, d), BF16),
            jax.ShapeDtypeStruct((n_e, r), F32),
        ],
        compiler_params=_params("arbitrary"),
        name="outproj",
    )(*o_list, w_bf, h, g1, gate, g2, shift, scale_m, wr_t)


def _route_kernel(aff_ref, posm_ref, pose_ref, gsel_ref, *, cap):
    n_e, n = aff_ref.shape
    aff = aff_ref[...]
    capf = float(cap)
    floor = float(jnp.finfo(F32).tiny)

    def search(_, carry):
        lo, hi = carry
        mid = jnp.sqrt(jnp.maximum(lo, floor) * hi)
        cnt = jnp.sum(jnp.where(aff >= mid, 1.0, 0.0), axis=1, keepdims=True)
        ok = cnt >= capf
        return jnp.where(ok, mid, lo), jnp.where(ok, hi, mid)

    lo0 = jnp.zeros((n_e, 1), F32)
    hi0 = jnp.full((n_e, 1), 2.0, F32)
    lo, hi = lax.fori_loop(0, ROUTE_SEARCH_STEPS, search, (lo0, hi0))
    need = capf - jnp.sum(jnp.where(aff >= hi, 1.0, 0.0), axis=1, keepdims=True)
    upper = jnp.where(lax.broadcasted_iota(I32, (LANES, LANES), 0) < lax.broadcasted_iota(I32, (LANES, LANES), 1),
                      1.0, 0.0).astype(BF16)
    run_eq = jnp.zeros((n_e, 1), F32)
    run_sel = jnp.zeros((n_e, 1), F32)
    for j in range(n // LANES):
        sl = slice(j * LANES, (j + 1) * LANES)
        a = aff_ref[:, sl]
        above = a >= hi
        tie = (a >= lo) & (a < hi)
        eq = jnp.where(tie, 1.0, 0.0)
        rank = _dot(eq.astype(BF16), upper) + run_eq
        run_eq = run_eq + eq.sum(axis=1, keepdims=True)
        sel = above | (tie & (rank < need))
        self_f = jnp.where(sel, 1.0, 0.0)
        pos = _dot(self_f.astype(BF16), upper) + run_sel
        run_sel = run_sel + self_f.sum(axis=1, keepdims=True)
        pos_i = pos.astype(I32)
        pose_ref[0, :, sl] = pos_i
        posm_ref[0, :, sl] = jnp.where(sel, pos_i, -1)
        gsel_ref[0, :, sl] = jnp.where(sel, a, 0.0)


def _route(aff, batch, n, cap):
    n_e = aff.shape[0]
    spec = pl.BlockSpec((1, n_e, n), lambda b: (b, 0, 0))
    return pl.pallas_call(
        functools.partial(_route_kernel, cap=cap),
        grid=(batch,),
        in_specs=[pl.BlockSpec((n_e, n), lambda b: (0, b))],
        out_specs=[spec, spec, spec],
        out_shape=[
            jax.ShapeDtypeStruct((batch, n_e, n), I32),
            jax.ShapeDtypeStruct((batch, n_e, n), I32),
            jax.ShapeDtypeStruct((batch, n_e, n), F32),
        ],
        compiler_params=_params("arbitrary"),
        name="route",
    )(aff)


def _gather_kernel(cs_ref, posm_ref, gsel_ref, x_ref, xg_ref, gs_ref, acc_ref, gacc_ref, *, win, tc, n_chunks,
                   n_e, cap):
    b, e = pl.program_id(0), pl.program_id(1)
    base = (b * n_e + e) * (n_chunks + 1)
    acc_ref[...] = jnp.zeros_like(acc_ref)
    gacc_ref[...] = jnp.zeros_like(gacc_ref)
    row = lax.broadcasted_iota(I32, (win, tc), 0)

    def fill(c, a0, w):
        first = a0 + w * win
        start = pl.multiple_of(jnp.minimum(first, cap - win), F32_SUBLANES)
        slot = start + row
        hit = (posm_ref[0, 0, c:c + 1, :] == slot) & (slot >= first)
        onehot = jnp.where(hit, 1.0, 0.0).astype(BF16)
        acc_ref[pl.ds(start, win), :] += _dot(onehot, x_ref[c * tc:(c + 1) * tc, :])
        gacc_ref[pl.ds(start, win), :] += jnp.where(hit, gsel_ref[0, 0, c:c + 1, :], 0.0).sum(
            axis=1, keepdims=True)

    starts = [(cs_ref[base + c] // F32_SUBLANES) * F32_SUBLANES for c in range(n_chunks)]
    for c in range(n_chunks):
        fill(c, starts[c], 0)
    for c in range(n_chunks):
        n_win = (cs_ref[base + c + 1] - starts[c] + win - 1) // win

        def more(w, carry, c=c):
            fill(c, starts[c], w)
            return carry

        lax.fori_loop(1, n_win, more, 0)

    xg_ref[0] = acc_ref[...].astype(BF16)
    gs_ref[0] = gacc_ref[...]


def _gather(xf, posm, pose, gsel, batch, n, cap):
    d = xf.shape[1]
    n_e = posm.shape[1]
    tc = _tile(n, 512)
    win = min(cap, LANES)
    n_chunks = n // tc
    cstart = jnp.concatenate([pose[:, :, ::tc], jnp.full((batch, n_e, 1), cap, I32)], axis=-1).reshape(-1)
    posm4 = posm.reshape(batch, n_e, n_chunks, tc)
    gsel4 = gsel.reshape(batch, n_e, n_chunks, tc)
    grid_spec = pltpu.PrefetchScalarGridSpec(
        num_scalar_prefetch=1,
        grid=(batch, n_e),
        in_specs=[
            pl.BlockSpec((1, 1, n_chunks, tc), lambda b, e, cs: (b, e, 0, 0)),
            pl.BlockSpec((1, 1, n_chunks, tc), lambda b, e, cs: (b, e, 0, 0)),
            pl.BlockSpec((n, d), lambda b, e, cs: (b, 0)),
        ],
        out_specs=[
            pl.BlockSpec((1, cap, d), lambda b, e, cs: (e, b, 0)),
            pl.BlockSpec((1, cap, 1), lambda b, e, cs: (e, b, 0)),
        ],
        scratch_shapes=[pltpu.VMEM((cap, d), F32), pltpu.VMEM((cap, 1), F32)],
    )
    return pl.pallas_call(
        functools.partial(_gather_kernel, win=win, tc=tc, n_chunks=n_chunks, n_e=n_e, cap=cap),
        grid_spec=grid_spec,
        out_shape=[
            jax.ShapeDtypeStruct((n_e, batch * cap, d), BF16),
            jax.ShapeDtypeStruct((n_e, batch * cap, 1), F32),
        ],
        compiler_params=_params("arbitrary", "arbitrary"),
        name="moe_gather",
    )(cstart, posm4, gsel4, xf)


def _ffn_kernel(*refs, with_ctx):
    if with_ctx:
        xl_ref, xc_ref, wg_ref, wu_ref, wd_ref, gl_ref, gc_ref, yl_ref, yc_ref, accl_ref, accc_ref = refs
    else:
        xl_ref, wg_ref, wu_ref, wd_ref, gl_ref, yl_ref, accl_ref = refs
    i, f = pl.program_id(1), pl.program_id(2)
    last_f = pl.num_programs(2) - 1
    tf = wg_ref.shape[2]
    fw = min(tf, FFN_SUB)

    def run(x_ref, gs_ref, y_ref, acc_ref):
        @pl.when(f == 0)
        def _():
            acc_ref[...] = jnp.zeros_like(acc_ref)

        x = x_ref[0]
        hid = []
        for c0 in range(0, tf, fw):
            hg = _dot(x, wg_ref[0, :, c0:c0 + fw].astype(BF16))
            hu = _dot(x, wu_ref[0, :, c0:c0 + fw].astype(BF16))
            hid.append((_silu(hg) * hu).astype(BF16))
        acc_ref[...] += _dot(jnp.concatenate(hid, axis=1), wd_ref[0].astype(BF16))

        @pl.when(f == last_f)
        def _():
            y_ref[0] = (acc_ref[...] * gs_ref[0]).astype(BF16)

    run(xl_ref, gl_ref, yl_ref, accl_ref)
    if with_ctx:
        @pl.when(i == pl.num_programs(1) - 1)
        def _():
            run(xc_ref, gc_ref, yc_ref, accc_ref)


def _expert_ffn(layer, w_gate, w_up, w_down, xg, gslot, xg_c=None, gslot_c=None):
    n_e, m, d = xg.shape
    ff = w_gate.shape[3]
    tm = _tile(m, 1024)
    tf = _tile(ff, 512)
    with_ctx = xg_c is not None
    x_spec = pl.BlockSpec((1, tm, d), lambda e, i, f: (e, i, 0))
    g_spec = pl.BlockSpec((1, tm, 1), lambda e, i, f: (e, i, 0))
    w_specs = [
        pl.BlockSpec((None, 1, d, tf), lambda e, i, f: (layer, e, 0, f)),
        pl.BlockSpec((None, 1, d, tf), lambda e, i, f: (layer, e, 0, f)),
        pl.BlockSpec((None, 1, tf, d), lambda e, i, f: (layer, e, f, 0)),
    ]
    out_specs = [x_spec]
    out_shape = [jax.ShapeDtypeStruct((n_e, m, d), BF16)]
    scratch = [pltpu.VMEM((tm, d), F32)]
    if with_ctx:
        mc = xg_c.shape[1]
        xc_spec = pl.BlockSpec((1, mc, d), lambda e, i, f: (e, 0, 0))
        gc_spec = pl.BlockSpec((1, mc, 1), lambda e, i, f: (e, 0, 0))
        in_specs = [x_spec, xc_spec] + w_specs + [g_spec, gc_spec]
        args = [xg, xg_c, w_gate, w_up, w_down, gslot, gslot_c]
        out_specs.append(xc_spec)
        out_shape.append(jax.ShapeDtypeStruct((n_e, mc, d), BF16))
        scratch.append(pltpu.VMEM((mc, d), F32))
    else:
        in_specs = [x_spec] + w_specs + [g_spec]
        args = [xg, w_gate, w_up, w_down, gslot]
    return pl.pallas_call(
        functools.partial(_ffn_kernel, with_ctx=with_ctx),
        grid=(n_e, m // tm, ff // tf),
        in_specs=in_specs,
        out_specs=out_specs,
        out_shape=out_shape,
        scratch_shapes=scratch,
        compiler_params=_params("arbitrary", "arbitrary", "arbitrary"),
        name="expert_ffn",
    )(*args)


def _combine_kernel(cs_ref, posm_ref, y_ref, h_ref, g_ref, gate_ref, o_ref, *, tt, n_sub, win, cap, n_e,
                    n_tiles):
    b, i = pl.program_id(0), pl.program_id(1)
    for sub in range(n_sub):
        cols = slice(sub * tt, (sub + 1) * tt)
        acc = None
        for e in range(n_e):
            if cap <= win:
                a0 = 0
                yw = y_ref[e]
            else:
                s0 = cs_ref[(b * n_e + e) * n_tiles + i * n_sub + sub]
                a0 = jnp.minimum((s0 // BF16_SUBLANES) * BF16_SUBLANES, cap - win)
                a0 = pl.multiple_of(a0, BF16_SUBLANES)
                yw = y_ref[e, pl.ds(a0, win), :]
            wn = yw.shape[0]
            slot = a0 + lax.broadcasted_iota(I32, (wn, tt), 0)
            onehot = jnp.where(posm_ref[0, e:e + 1, cols] == slot, 1.0, 0.0).astype(BF16)
            part = lax.dot_general(onehot, yw, (((0,), (0,)), ((), ())), preferred_element_type=F32)
            acc = part if acc is None else acc + part
        o_ref[cols, :] = h_ref[cols, :] + gate_ref[...] * _rms(acc, g_ref[...])


def _combine(y, posm, pose, h, g, gate, row_fn, batch, n, cap):
    n_e = y.shape[0]
    d = y.shape[2]
    tt = LANES
    win = tt + BF16_SUBLANES
    n_tiles = n // tt
    n_sub = 2 if n_tiles % 2 == 0 else 1
    n_steps = n_tiles // n_sub
    cstart = pose[:, :, ::tt].reshape(-1)
    grid_spec = pltpu.PrefetchScalarGridSpec(
        num_scalar_prefetch=1,
        grid=(batch, n_steps),
        in_specs=[
            pl.BlockSpec((1, n_e, n_sub * tt), lambda b, i, cs: (b, 0, i)),
            pl.BlockSpec((n_e, cap, d), lambda b, i, cs: (0, b, 0), pipeline_mode=pl.Buffered(1)),
            pl.BlockSpec((n_sub * tt, d), lambda b, i, cs: (b * n_steps + i, 0)),
            pl.BlockSpec((1, d), lambda b, i, cs: (0, 0)),
            pl.BlockSpec((None, 1, d), lambda b, i, cs: (row_fn(b * n), 0, 0)),
        ],
        out_specs=pl.BlockSpec((n_sub * tt, d), lambda b, i, cs: (b * n_steps + i, 0)),
    )
    return pl.pallas_call(
        functools.partial(_combine_kernel, tt=tt, n_sub=n_sub, win=win, cap=cap, n_e=n_e, n_tiles=n_tiles),
        grid_spec=grid_spec,
        out_shape=jax.ShapeDtypeStruct(h.shape, F32),
        compiler_params=_params("arbitrary", "arbitrary"),
        name="moe_combine",
    )(cstart, posm, y, h, g, gate)


def _moe_dispatch(xf, aff, batch, n):
    cap = CAPACITY_FACTOR * n // N_EXPERTS
    posm, pose, gsel = _route(aff, batch, n, cap)
    xg, gslot = _gather(xf, posm, pose, gsel, batch, n, cap)
    return xg, gslot, (posm, pose, cap)


def _rope_angles(seq, rot_dim):
    n_rows = seq // GRID_W
    row = jnp.repeat(jnp.arange(n_rows), GRID_W)
    col = jnp.tile(jnp.arange(GRID_W), n_rows)
    n_freq = rot_dim // 4
    inv = ROPE_BASE ** (-jnp.arange(n_freq, dtype=F32) / n_freq)
    ang = jnp.concatenate([row[:, None] * inv, col[:, None] * inv], axis=-1)
    return jnp.cos(ang), jnp.sin(ang)


def kernel(x, c, ctx, c_ctx, mod_w, mod_b, norm_g, ev_w_in, ev_w_out, ev_sink, ev_qk_norm, od_w_in, od_q_norm,
           od_kv_norm, od_w_uq, od_w_ukv, od_w_out, router_w, exp_w_gate, exp_w_up, exp_w_down):
    batch, seq, d = x.shape
    n_ctx = ctx.shape[1]
    depth = mod_w.shape[0]
    assert batch < MOD_ROWS and seq % LANES == 0 and n_ctx % LANES == 0 and seq >= 4 * WINDOW

    cs = jnp.concatenate([c, c_ctx[None, :], jnp.zeros((MOD_ROWS - batch - 1, d), F32)], axis=0)
    mods = _modulation(cs, mod_w, mod_b).reshape(depth, MOD_ROWS, N_MOD, 1, d)

    def lat_row(r):
        return r // seq

    def ctx_row(r):
        return batch

    cos_h, sin_h = _rope_angles(seq, HEAD_DIM)
    rope_even = (jnp.concatenate([cos_h, cos_h], axis=-1), jnp.concatenate([-sin_h, sin_h], axis=-1))
    cos_m, sin_m = _rope_angles(seq, MLA_ROPE)
    half = MLA_ROPE // 2
    zeros = lambda w: jnp.zeros((seq, w), F32)
    rope_mla = (
        jnp.concatenate([cos_m, cos_m, zeros(LANES - MLA_ROPE)], axis=-1),
        jnp.concatenate([-sin_m, zeros(LANES - half)], axis=-1),
        jnp.concatenate([zeros(half), sin_m, zeros(LANES - MLA_ROPE)], axis=-1),
    )

    h_lat = x.reshape(batch * seq, d)
    h_ctx = ctx.reshape(batch * n_ctx, d)
    for layer in range(depth):
        with_ctx = layer < depth - 1
        i = layer // 2
        g = norm_g[layer].reshape(4, 1, d)
        m = [mods[layer, :, k] for k in range(N_MOD)]
        wr_t = router_w[layer].T
        if layer % 2 == 0:
            w_in = ev_w_in[i].astype(BF16)
            w_out = ev_w_out[i].astype(BF16)
            qkv_l = _even_inproj(h_lat, g[0], m[0], m[1], w_in, ev_qk_norm[i], rope_even, lat_row, seq)
            qkv_c = _even_inproj(h_ctx, g[0], m[0], m[1], w_in, ev_qk_norm[i], None, ctx_row, n_ctx)
            ga, gb = A_HEADS // A_KV_HEADS, B_HEADS // B_KV_HEADS
            hd = HEAD_DIM
            a_kw = dict(batch=batch, n_heads=A_HEADS, group=ga, dq=hd, dv=hd)
            b_kw = dict(batch=batch, n_heads=B_HEADS, group=gb, dq=hd, dv=hd, band=False)
            a_ctx = (qkv_c, AK0 * hd, qkv_c, AV0 * hd, n_ctx)
            b_ctx = (qkv_c, BK0 * hd, qkv_c, BV0 * hd, n_ctx)
            o_a = _attention(qkv_l, AQ0 * hd, [a_ctx, (qkv_l, AK0 * hd, qkv_l, AV0 * hd, seq)], ev_sink[i],
                             seq=seq, band=True, tq_cap=512, hp=4, **a_kw)
            o_b = _attention(qkv_l, BQ0 * hd, [b_ctx, (qkv_l, BK0 * hd, qkv_l, BV0 * hd, seq)], None,
                             seq=seq, tq_cap=512, hp=4, **b_kw)
            o_lat = [o_a, o_b]
            if with_ctx:
                o_ac = _attention(qkv_c, AQ0 * hd, [a_ctx], ev_sink[i], seq=n_ctx, band=False, tq_cap=512,
                                  hp=4, **a_kw)
                o_bc = _attention(qkv_c, BQ0 * hd, [b_ctx], None, seq=n_ctx, tq_cap=512, hp=4, **b_kw)
                o_ctx = [o_ac, o_bc]
        else:
            w_in = od_w_in[i]
            w_qkv = w_in[:, :MLA_Q_RANK + MLA_KV_RANK].astype(BF16)
            w_kr = jnp.pad(w_in[:, MLA_Q_RANK + MLA_KV_RANK:], ((0, 0), (0, LANES - MLA_ROPE))).astype(BF16)
            w_uq = od_w_uq[i].reshape(MLA_Q_RANK, MLA_HEADS, MLA_NOPE + MLA_ROPE)
            w_uq = jnp.pad(w_uq, ((0, 0), (0, 0), (0, MLA_QK_PAD - MLA_NOPE - MLA_ROPE)))
            w_uq = w_uq.reshape(MLA_Q_RANK, MLA_HEADS * MLA_QK_PAD).astype(BF16)
            w_ukv = od_w_ukv[i].astype(BF16)
            w_out = od_w_out[i].astype(BF16)
            qn, kvn = od_q_norm[i].reshape(1, -1), od_kv_norm[i].reshape(1, -1)
            q_l, k_l, v_l = _mla_proj(h_lat, g[0], m[0], m[1], w_qkv, w_kr, qn, kvn, w_uq, w_ukv, rope_mla,
                                      lat_row, seq, True)
            proj_c = _mla_proj(h_ctx, g[0], m[0], m[1], w_qkv, w_kr, qn, kvn, w_uq, w_ukv, None, ctx_row,
                               n_ctx, with_ctx)
            k_c, v_c = proj_c[-2], proj_c[-1]
            m_kw = dict(batch=batch, n_heads=MLA_HEADS, group=1, dq=MLA_QK_PAD, dv=MLA_V, band=False, tq_cap=512,
                        hp=4, tk=1024)
            o_lat = [_attention(q_l, 0, [(k_c, 0, v_c, 0, n_ctx), (k_l, 0, v_l, 0, seq)], None, seq=seq, **m_kw)]
            if with_ctx:
                o_ctx = [_attention(proj_c[0], 0, [(k_c, 0, v_c, 0, n_ctx)], None, seq=n_ctx, **m_kw)]
        ew = (layer, exp_w_gate, exp_w_up, exp_w_down)
        h_lat, xf, aff = _outproj(o_lat, w_out, h_lat, g[1], m[2], g[2], m[3], m[4], wr_t, lat_row, seq)
        xg, gslot, (posm, pose, cap) = _moe_dispatch(xf, aff, batch, seq)
        if with_ctx:
            h_ctx, xf_c, aff_c = _outproj(o_ctx, w_out, h_ctx, g[1], m[2], g[2], m[3], m[4], wr_t, ctx_row,
                                          n_ctx)
            xg_c, gslot_c, (posm_c, pose_c, cap_c) = _moe_dispatch(xf_c, aff_c, batch, n_ctx)
            y, y_c = _expert_ffn(*ew, xg, gslot, xg_c, gslot_c)
            h_ctx = _combine(y_c, posm_c, pose_c, h_ctx, g[3], m[5], ctx_row, batch, n_ctx, cap_c)
        else:
            (y,) = _expert_ffn(*ew, xg, gslot)
        h_lat = _combine(y, posm, pose, h_lat, g[3], m[5], lat_row, batch, seq, cap)
    return h_lat.reshape(batch, seq, d)
```

```python
import functools

import jax
import jax.numpy as jnp
from jax import lax
from jax.experimental import pallas as pl
from jax.experimental.pallas import tpu as pltpu

F32 = jnp.float32
BF16 = jnp.bfloat16
I32 = jnp.int32

EPS = 1e-6
GRID_W = 64
WINDOW = 128
ROPE_BASE = 10000.0
HEAD_DIM = 128
A_HEADS = 8
A_KV_HEADS = 2
B_HEADS = 8
B_KV_HEADS = 2
MLA_HEADS = 16
MLA_Q_RANK = 512
MLA_KV_RANK = 512
MLA_NOPE = 128
MLA_ROPE = 64
MLA_V = 128
MLA_QK_PAD = 256
N_EXPERTS = 16
CAPACITY_FACTOR = 2
N_MOD = 6
MOD_ROWS = 8

LANES = 128
BF16_SUBLANES = 16
F32_SUBLANES = 8
VMEM_LIMIT_BYTES = 60 * 1024 * 1024
FFN_SUB = 256
ROUTE_SEARCH_STEPS = 64

LOG2E = 1.4426950408889634

EVEN_IN_COLS = (A_HEADS + 2 * A_KV_HEADS + B_HEADS + 2 * B_KV_HEADS) * HEAD_DIM
AQ0 = 0
BQ0 = AQ0 + A_HEADS
AK0 = BQ0 + B_HEADS
BK0 = AK0 + A_KV_HEADS
AV0 = BK0 + B_KV_HEADS
BV0 = AV0 + 2 * A_KV_HEADS
EVEN_SLOTS = BV0 + 2 * B_KV_HEADS
EVEN_COLS = EVEN_SLOTS * HEAD_DIM
_W_AK0 = A_HEADS
_W_AV0 = _W_AK0 + A_KV_HEADS
_W_BQ0 = _W_AV0 + A_KV_HEADS
_W_BK0 = _W_BQ0 + B_HEADS
_W_BV0 = _W_BK0 + B_KV_HEADS


def _even_head(wh):
    if wh < _W_AK0:
        return AQ0 + wh, "aq"
    if wh < _W_AV0:
        return AK0 + wh - _W_AK0, "ak"
    if wh < _W_BQ0:
        return AV0 + 2 * (wh - _W_AV0), "av"
    if wh < _W_BK0:
        return BQ0 + wh - _W_BQ0, "bq"
    if wh < _W_BV0:
        return BK0 + wh - _W_BK0, "bk"
    return BV0 + 2 * (wh - _W_BV0), "bv"


def _params(*sem):
    return pltpu.CompilerParams(dimension_semantics=sem, vmem_limit_bytes=VMEM_LIMIT_BYTES)


def _tile(n, cap):
    t = min(n, cap)
    while n % t:
        t -= 1
    return t


def _const_spec(shape):
    nd = len(shape)
    return pl.BlockSpec(shape, lambda *_: (0,) * nd)


def _rms(x, g):
    return x * lax.rsqrt(jnp.mean(x * x, axis=-1, keepdims=True) + EPS) * g


def _silu(x):
    return x / (1.0 + jnp.exp(-x))


def _dot(a, b):
    return jnp.dot(a, b, preferred_element_type=F32)


def _dot_nt(a, b):
    return lax.dot_general(a, b, (((1,), (1,)), ((), ())), preferred_element_type=F32)


def _split_bf16(x):
    hi = x.astype(BF16)
    return hi, (x - hi.astype(F32)).astype(BF16)


def _mod_kernel(cs_ref, w_ref, b_ref, o_ref):
    s_hi, s_lo = _split_bf16(_silu(cs_ref[...]))
    w_hi, w_lo = _split_bf16(w_ref[0])
    o_ref[0] = _dot(s_hi, w_hi) + (_dot(s_hi, w_lo) + _dot(s_lo, w_hi)) + b_ref[0]


def _modulation(cs, mod_w, mod_b):
    depth, d, n6 = mod_w.shape
    tn = _tile(n6, 1024)
    return pl.pallas_call(
        _mod_kernel,
        grid=(depth, n6 // tn),
        in_specs=[
            _const_spec((MOD_ROWS, d)),
            pl.BlockSpec((1, d, tn), lambda l, j: (l, 0, j)),
            pl.BlockSpec((1, 1, tn), lambda l, j: (l, 0, j)),
        ],
        out_specs=pl.BlockSpec((1, MOD_ROWS, tn), lambda l, j: (l, 0, j)),
        out_shape=jax.ShapeDtypeStruct((depth, MOD_ROWS, n6), F32),
        compiler_params=_params("arbitrary", "arbitrary"),
        name="modulation",
    )(cs, mod_w, mod_b.reshape(depth, 1, n6))


def _mod_spec(d, row_fn):
    return pl.BlockSpec((None, 1, d), lambda i: (row_fn(i), 0, 0))


def _even_inproj_kernel(*refs, rope, scale):
    if rope:
        x_ref, g_ref, sh_ref, sc_ref, w_ref, qkg_ref, cos_ref, sin_ref, o_ref = refs
        cos, sin = cos_ref[...], sin_ref[...]
    else:
        x_ref, g_ref, sh_ref, sc_ref, w_ref, qkg_ref, o_ref = refs
    a = _rms(x_ref[...], g_ref[...]) * (1.0 + sc_ref[...]) + sh_ref[...]
    ab = a.astype(BF16)
    ones = jnp.ones((x_ref.shape[0], HEAD_DIM), BF16)
    for j in range(EVEN_IN_COLS // (2 * HEAD_DIM)):
        acc = _dot(ab, w_ref[:, j * 2 * HEAD_DIM:(j + 1) * 2 * HEAD_DIM])
        for hh in range(2):
            slot, kind = _even_head(2 * j + hh)
            v = acc[:, hh * HEAD_DIM:(hh + 1) * HEAD_DIM]
            if kind == "bq":
                v = _rms(v, qkg_ref[0:1, :])
            elif kind == "bk":
                v = _rms(v, qkg_ref[1:2, :])
            if rope and kind[1] != "v":
                v = v * cos + pltpu.roll(v, HEAD_DIM // 2, 1) * sin
            if kind[1] == "q":
                v = v * scale
            o_ref[:, slot * HEAD_DIM:(slot + 1) * HEAD_DIM] = v.astype(BF16)
            if kind[1] == "v":
                o_ref[:, (slot + 1) * HEAD_DIM:(slot + 2) * HEAD_DIM] = ones


def _even_inproj(h, g, shift, scale_m, w_bf, qk_gain, rope_tabs, row_fn, seq):
    r, d = h.shape
    tm = _tile(seq, 512)
    rope = rope_tabs is not None
    in_specs = [
        pl.BlockSpec((tm, d), lambda i: (i, 0)),
        _const_spec((1, d)),
        _mod_spec(d, lambda i: row_fn(i * tm)),
        _mod_spec(d, lambda i: row_fn(i * tm)),
        _const_spec((d, EVEN_IN_COLS)),
        _const_spec((2, HEAD_DIM)),
    ]
    args = [h, g, shift, scale_m, w_bf, qk_gain]
    if rope:
        nt = seq // tm
        in_specs += [pl.BlockSpec((tm, HEAD_DIM), lambda i: (i % nt, 0))] * 2
        args += list(rope_tabs)
    return pl.pallas_call(
        functools.partial(_even_inproj_kernel, rope=rope, scale=HEAD_DIM ** -0.5 * LOG2E),
        grid=(r // tm,),
        in_specs=in_specs,
        out_specs=pl.BlockSpec((tm, EVEN_COLS), lambda i: (i, 0)),
        out_shape=jax.ShapeDtypeStruct((r, EVEN_COLS), BF16),
        compiler_params=_params("arbitrary"),
        name="even_inproj",
    )(*args)


def _rope_pad(v, c, s1, s2):
    return v * c + pltpu.roll(v, LANES - MLA_ROPE // 2, 1) * s1 + pltpu.roll(v, MLA_ROPE // 2, 1) * s2


def _mla_proj_kernel(*refs, rope, want_q, scale):
    refs = list(refs)
    x_ref, g_ref, sh_ref, sc_ref, win_ref, wkr_ref, qn_ref, kvn_ref = refs[:8]
    refs = refs[8:]
    if want_q:
        wuq_ref = refs.pop(0)
    wukv_ref = refs.pop(0)
    if rope:
        c, s1, s2 = refs[0][...], refs[1][...], refs[2][...]
        refs = refs[3:]
    if want_q:
        q_ref = refs.pop(0)
    k_ref, v_ref = refs
    a = _rms(x_ref[...], g_ref[...]) * (1.0 + sc_ref[...]) + sh_ref[...]
    ab = a.astype(BF16)
    low = _dot(ab, win_ref[...])
    kr = _dot(ab, wkr_ref[...])
    if rope:
        kr = _rope_pad(kr, c, s1, s2)
    krb = kr.astype(BF16)
    if want_q:
        cq = _rms(low[:, :MLA_Q_RANK], qn_ref[...]).astype(BF16)
        for h in range(MLA_HEADS):
            acc = _dot(cq, wuq_ref[:, h * MLA_QK_PAD:(h + 1) * MLA_QK_PAD])
            qr = acc[:, MLA_NOPE:]
            if rope:
                qr = _rope_pad(qr, c, s1, s2)
            q_ref[:, h * MLA_QK_PAD:h * MLA_QK_PAD + MLA_NOPE] = (acc[:, :MLA_NOPE] * scale).astype(BF16)
            q_ref[:, h * MLA_QK_PAD + MLA_NOPE:(h + 1) * MLA_QK_PAD] = (qr * scale).astype(BF16)
    ckv = _rms(low[:, MLA_Q_RANK:], kvn_ref[...]).astype(BF16)
    hw = MLA_NOPE + MLA_V
    for h in range(MLA_HEADS):
        acc = _dot(ckv, wukv_ref[:, h * hw:(h + 1) * hw])
        k_ref[:, h * MLA_QK_PAD:h * MLA_QK_PAD + MLA_NOPE] = acc[:, :MLA_NOPE].astype(BF16)
        k_ref[:, h * MLA_QK_PAD + MLA_NOPE:(h + 1) * MLA_QK_PAD] = krb
        v_ref[:, 2 * h * MLA_V:(2 * h + 1) * MLA_V] = acc[:, MLA_NOPE:].astype(BF16)
        v_ref[:, (2 * h + 1) * MLA_V:(2 * h + 2) * MLA_V] = jnp.ones((x_ref.shape[0], MLA_V), BF16)


def _mla_proj(h, g, shift, scale_m, w_in_bf, w_kr_bf, q_norm, kv_norm, w_uq_bf, w_ukv_bf, rope_tabs,
              row_fn, seq, want_q):
    r, d = h.shape
    tm = _tile(seq, 512)
    rope = rope_tabs is not None
    in_specs = [
        pl.BlockSpec((tm, d), lambda i: (i, 0)),
        _const_spec((1, d)),
        _mod_spec(d, lambda i: row_fn(i * tm)),
        _mod_spec(d, lambda i: row_fn(i * tm)),
        _const_spec(w_in_bf.shape),
        _const_spec(w_kr_bf.shape),
        _const_spec((1, MLA_Q_RANK)),
        _const_spec((1, MLA_KV_RANK)),
    ]
    args = [h, g, shift, scale_m, w_in_bf, w_kr_bf, q_norm, kv_norm]
    if want_q:
        in_specs.append(_const_spec(w_uq_bf.shape))
        args.append(w_uq_bf)
    in_specs.append(_const_spec(w_ukv_bf.shape))
    args.append(w_ukv_bf)
    if rope:
        nt = seq // tm
        in_specs += [pl.BlockSpec((tm, LANES), lambda i: (i % nt, 0))] * 3
        args += list(rope_tabs)
    kcols = MLA_HEADS * MLA_QK_PAD
    vcols = MLA_HEADS * 2 * MLA_V
    out_specs = [pl.BlockSpec((tm, kcols), lambda i: (i, 0)), pl.BlockSpec((tm, vcols), lambda i: (i, 0))]
    out_shape = [jax.ShapeDtypeStruct((r, kcols), BF16), jax.ShapeDtypeStruct((r, vcols), BF16)]
    if want_q:
        out_specs.insert(0, pl.BlockSpec((tm, kcols), lambda i: (i, 0)))
        out_shape.insert(0, jax.ShapeDtypeStruct((r, kcols), BF16))
    return pl.pallas_call(
        functools.partial(_mla_proj_kernel, rope=rope, want_q=want_q,
                          scale=(MLA_NOPE + MLA_ROPE) ** -0.5 * LOG2E),
        grid=(r // tm,),
        in_specs=in_specs,
        out_specs=out_specs,
        out_shape=out_shape,
        compiler_params=_params("arbitrary"),
        name="mla_proj",
    )(*args)


def _attn_kernel(*refs, n_seg, band, use_sink, tq, seq, hp, group, dq, dv, tk):
    refs = list(refs)
    if use_sink:
        sink_ref = refs.pop(0)
    q_ref = refs.pop(0)
    o_ref = refs.pop()
    dvx = 2 * dv
    if band:
        wk = tq + 2 * WINDOW
        q0 = pl.program_id(2) * tq
        start = pl.multiple_of(jnp.clip(q0 - WINDOW, 0, seq - wk), LANES)
        dist = (lax.broadcasted_iota(I32, (tq, wk), 1) - lax.broadcasted_iota(I32, (tq, wk), 0)) + (start - q0)
        valid = jnp.abs(dist) <= WINDOW
    for j in range(hp):
        kv = j // group
        q = q_ref[:, j * dq:(j + 1) * dq]
        m = None
        acc = None
        for s_i in range(n_seg):
            k_ref, v_ref = refs[2 * s_i], refs[2 * s_i + 1]
            slen = k_ref.shape[0]
            masked = band and s_i == n_seg - 1
            chunks = [(start, wk)] if masked else [(c0, min(tk, slen - c0)) for c0 in range(0, slen, tk)]
            for c0, cl in chunks:
                s = _dot_nt(q, k_ref[pl.ds(c0, cl), kv * dq:(kv + 1) * dq])
                if masked:
                    s = jnp.where(valid, s, -jnp.inf)
                v = v_ref[pl.ds(c0, cl), kv * dvx:(kv + 1) * dvx]
                mc = s.max(axis=-1, keepdims=True)
                if m is None:
                    m = mc
                    acc = _dot(jnp.exp2(s - m).astype(BF16), v)
                else:
                    m_new = jnp.maximum(m, mc)
                    acc = jnp.exp2(m - m_new) * acc + _dot(jnp.exp2(s - m_new).astype(BF16), v)
                    m = m_new
        den = acc[:, dv:]
        if use_sink:
            sink = sink_ref[pl.program_id(1) * hp + j] * LOG2E
            den = den + jnp.exp2(sink - m)
        o_ref[:, j * dv:(j + 1) * dv] = (acc[:, :dv] / den).astype(o_ref.dtype)


def _attention(q_arr, q_col0, segs, sink, *, batch, seq, n_heads, group, dq, dv, band, tq_cap, hp, tk=512):
    tq = _tile(seq, tq_cap)
    nq = seq // tq
    use_sink = sink is not None
    if hp <= group:
        assert group % hp == 0
        kvp, kv_of, step_group = 1, (lambda hg: hg // (group // hp)), hp
    else:
        assert hp % group == 0
        kvp, kv_of, step_group = hp // group, (lambda hg: hg), group
    qw, kw, vw = hp * dq, kvp * dq, kvp * 2 * dv
    assert q_col0 % qw == 0
    in_specs, args = [], []
    if use_sink:
        in_specs.append(pl.BlockSpec(memory_space=pltpu.SMEM))
        args.append(sink)
    in_specs.append(pl.BlockSpec((tq, qw), lambda b, hg, i: (b * nq + i, q_col0 // qw + hg)))
    args.append(q_arr)
    for k_arr, k_col0, v_arr, v_col0, slen in segs:
        assert k_col0 % kw == 0 and v_col0 % vw == 0
        in_specs.append(pl.BlockSpec((slen, kw), lambda b, hg, i, c=k_col0 // kw: (b, c + kv_of(hg))))
        in_specs.append(pl.BlockSpec((slen, vw), lambda b, hg, i, c=v_col0 // vw: (b, c + kv_of(hg))))
        args += [k_arr, v_arr]
    return pl.pallas_call(
        functools.partial(_attn_kernel, n_seg=len(segs), band=band, use_sink=use_sink, tq=tq, seq=seq, hp=hp,
                          group=step_group, dq=dq, dv=dv, tk=tk),
        grid=(batch, n_heads // hp, nq),
        in_specs=in_specs,
        out_specs=pl.BlockSpec((tq, hp * dv), lambda b, hg, i: (b * nq + i, hg)),
        out_shape=jax.ShapeDtypeStruct((batch * seq, n_heads * dv), BF16),
        compiler_params=_params("arbitrary", "arbitrary", "arbitrary"),
        name="attention",
    )(*args)


def _outproj_kernel(*refs, n_in):
    o_refs = refs[:n_in]
    (w_ref, h_ref, g1_ref, gate_ref, g2_ref, sh_ref, sc_ref, wr_ref, hn_ref, xf_ref, aff_ref) = refs[n_in:]
    y = None
    k0 = 0
    for o_ref in o_refs:
        kk = o_ref.shape[1]
        part = _dot(o_ref[...], w_ref[k0:k0 + kk, :])
        y = part if y is None else y + part
        k0 += kk
    hn = h_ref[...] + gate_ref[...] * _rms(y, g1_ref[...])
    hn_ref[...] = hn
    xf = _rms(hn, g2_ref[...]) * (1.0 + sc_ref[...]) + sh_ref[...]
    xf_hi, xf_lo = _split_bf16(xf)
    xf_ref[...] = xf_hi
    wr_hi, wr_lo = _split_bf16(wr_ref[...])
    logits = _dot_nt(wr_hi, xf_hi) + (_dot_nt(wr_hi, xf_lo) + _dot_nt(wr_lo, xf_hi))
    e = jnp.exp(logits - logits.max(axis=0, keepdims=True))
    aff_ref[...] = e / e.sum(axis=0, keepdims=True)


def _outproj(o_list, w_bf, h, g1, gate, g2, shift, scale_m, wr_t, row_fn, seq):
    r, d = h.shape
    tm = _tile(seq, 512)
    n_e = wr_t.shape[0]
    in_specs = [pl.BlockSpec((tm, o.shape[1]), lambda i: (i, 0)) for o in o_list]
    in_specs += [
        _const_spec(w_bf.shape),
        pl.BlockSpec((tm, d), lambda i: (i, 0)),
        _const_spec((1, d)),
        _mod_spec(d, lambda i: row_fn(i * tm)),
        _const_spec((1, d)),
        _mod_spec(d, lambda i: row_fn(i * tm)),
        _mod_spec(d, lambda i: row_fn(i * tm)),
        _const_spec((n_e, d)),
    ]
    return pl.pallas_call(
        functools.partial(_outproj_kernel, n_in=len(o_list)),
        grid=(r // tm,),
        in_specs=in_specs,
        out_specs=[
            pl.BlockSpec((tm, d), lambda i: (i, 0)),
            pl.BlockSpec((tm, d), lambda i: (i, 0)),
            pl.BlockSpec((n_e, tm), lambda i: (0, i)),
        ],
        out_shape=[
            jax.ShapeDtypeStruct((r, d), F32),
            jax.ShapeDtypeStruct((r, d), BF16),
            jax.ShapeDtypeStruct((n_e, r), F32),
        ],
        compiler_params=_params("arbitrary"),
        name="outproj",
    )(*o_list, w_bf, h, g1, gate, g2, shift, scale_m, wr_t)


def _route_kernel(aff_ref, posm_ref, pose_ref, gsel_ref, *, cap):
    n_e, n = aff_ref.shape
    aff = aff_ref[...]
    capf = float(cap)
    floor = float(jnp.finfo(F32).tiny)

    def search(_, carry):
        lo, hi = carry
        mid = jnp.sqrt(jnp.maximum(lo, floor) * hi)
        cnt = jnp.sum(jnp.where(aff >= mid, 1.0, 0.0), axis=1, keepdims=True)
        ok = cnt >= capf
        return jnp.where(ok, mid, lo), jnp.where(ok, hi, mid)

    lo0 = jnp.zeros((n_e, 1), F32)
    hi0 = jnp.full((n_e, 1), 2.0, F32)
    lo, hi = lax.fori_loop(0, ROUTE_SEARCH_STEPS, search, (lo0, hi0))
    need = capf - jnp.sum(jnp.where(aff >= hi, 1.0, 0.0), axis=1, keepdims=True)
    upper = jnp.where(lax.broadcasted_iota(I32, (LANES, LANES), 0) < lax.broadcasted_iota(I32, (LANES, LANES), 1),
                      1.0, 0.0).astype(BF16)
    run_eq = jnp.zeros((n_e, 1), F32)
    run_sel = jnp.zeros((n_e, 1), F32)
    for j in range(n // LANES):
        sl = slice(j * LANES, (j + 1) * LANES)
        a = aff_ref[:, sl]
        above = a >= hi
        tie = (a >= lo) & (a < hi)
        eq = jnp.where(tie, 1.0, 0.0)
        rank = _dot(eq.astype(BF16), upper) + run_eq
        run_eq = run_eq + eq.sum(axis=1, keepdims=True)
        sel = above | (tie & (rank < need))
        self_f = jnp.where(sel, 1.0, 0.0)
        pos = _dot(self_f.astype(BF16), upper) + run_sel
        run_sel = run_sel + self_f.sum(axis=1, keepdims=True)
        pos_i = pos.astype(I32)
        pose_ref[0, :, sl] = pos_i
        posm_ref[0, :, sl] = jnp.where(sel, pos_i, -1)
        gsel_ref[0, :, sl] = jnp.where(sel, a, 0.0)


def _route(aff, batch, n, cap):
    n_e = aff.shape[0]
    spec = pl.BlockSpec((1, n_e, n), lambda b: (b, 0, 0))
    return pl.pallas_call(
        functools.partial(_route_kernel, cap=cap),
        grid=(batch,),
        in_specs=[pl.BlockSpec((n_e, n), lambda b: (0, b))],
        out_specs=[spec, spec, spec],
        out_shape=[
            jax.ShapeDtypeStruct((batch, n_e, n), I32),
            jax.ShapeDtypeStruct((batch, n_e, n), I32),
            jax.ShapeDtypeStruct((batch, n_e, n), F32),
        ],
        compiler_params=_params("arbitrary"),
        name="route",
    )(aff)


def _gather_kernel(cs_ref, posm_ref, gsel_ref, x_ref, xg_ref, gs_ref, acc_ref, gacc_ref, *, win, tc, n_chunks,
                   n_e, cap):
    b, e = pl.program_id(0), pl.program_id(1)
    base = (b * n_e + e) * (n_chunks + 1)
    acc_ref[...] = jnp.zeros_like(acc_ref)
    gacc_ref[...] = jnp.zeros_like(gacc_ref)
    row = lax.broadcasted_iota(I32, (win, tc), 0)

    def fill(c, a0, w):
        first = a0 + w * win
        start = pl.multiple_of(jnp.minimum(first, cap - win), F32_SUBLANES)
        slot = start + row
        hit = (posm_ref[0, 0, c:c + 1, :] == slot) & (slot >= first)
        onehot = jnp.where(hit, 1.0, 0.0).astype(BF16)
        acc_ref[pl.ds(start, win), :] += _dot(onehot, x_ref[c * tc:(c + 1) * tc, :])
        gacc_ref[pl.ds(start, win), :] += jnp.where(hit, gsel_ref[0, 0, c:c + 1, :], 0.0).sum(
            axis=1, keepdims=True)

    starts = [(cs_ref[base + c] // F32_SUBLANES) * F32_SUBLANES for c in range(n_chunks)]
    for c in range(n_chunks):
        fill(c, starts[c], 0)
    for c in range(n_chunks):
        n_win = (cs_ref[base + c + 1] - starts[c] + win - 1) // win

        def more(w, carry, c=c):
            fill(c, starts[c], w)
            return carry

        lax.fori_loop(1, n_win, more, 0)

    xg_ref[0] = acc_ref[...].astype(BF16)
    gs_ref[0] = gacc_ref[...]


def _gather(xf, posm, pose, gsel, batch, n, cap):
    d = xf.shape[1]
    n_e = posm.shape[1]
    tc = _tile(n, 512)
    win = min(cap, LANES)
    n_chunks = n // tc
    cstart = jnp.concatenate([pose[:, :, ::tc], jnp.full((batch, n_e, 1), cap, I32)], axis=-1).reshape(-1)
    posm4 = posm.reshape(batch, n_e, n_chunks, tc)
    gsel4 = gsel.reshape(batch, n_e, n_chunks, tc)
    grid_spec = pltpu.PrefetchScalarGridSpec(
        num_scalar_prefetch=1,
        grid=(batch, n_e),
        in_specs=[
            pl.BlockSpec((1, 1, n_chunks, tc), lambda b, e, cs: (b, e, 0, 0)),
            pl.BlockSpec((1, 1, n_chunks, tc), lambda b, e, cs: (b, e, 0, 0)),
            pl.BlockSpec((n, d), lambda b, e, cs: (b, 0)),
        ],
        out_specs=[
            pl.BlockSpec((1, cap, d), lambda b, e, cs: (e, b, 0)),
            pl.BlockSpec((1, cap, 1), lambda b, e, cs: (e, b, 0)),
        ],
        scratch_shapes=[pltpu.VMEM((cap, d), F32), pltpu.VMEM((cap, 1), F32)],
    )
    return pl.pallas_call(
        functools.partial(_gather_kernel, win=win, tc=tc, n_chunks=n_chunks, n_e=n_e, cap=cap),
        grid_spec=grid_spec,
        out_shape=[
            jax.ShapeDtypeStruct((n_e, batch * cap, d), BF16),
            jax.ShapeDtypeStruct((n_e, batch * cap, 1), F32),
        ],
        compiler_params=_params("arbitrary", "arbitrary"),
        name="moe_gather",
    )(cstart, posm4, gsel4, xf)


def _ffn_kernel(*refs, with_ctx):
    if with_ctx:
        xl_ref, xc_ref, wg_ref, wu_ref, wd_ref, gl_ref, gc_ref, yl_ref, yc_ref, accl_ref, accc_ref = refs
    else:
        xl_ref, wg_ref, wu_ref, wd_ref, gl_ref, yl_ref, accl_ref = refs
    i, f = pl.program_id(1), pl.program_id(2)
    last_f = pl.num_programs(2) - 1
    tf = wg_ref.shape[2]
    fw = min(tf, FFN_SUB)

    def run(x_ref, gs_ref, y_ref, acc_ref):
        @pl.when(f == 0)
        def _():
            acc_ref[...] = jnp.zeros_like(acc_ref)

        x = x_ref[0]
        hid = []
        for c0 in range(0, tf, fw):
            hg = _dot(x, wg_ref[0, :, c0:c0 + fw].astype(BF16))
            hu = _dot(x, wu_ref[0, :, c0:c0 + fw].astype(BF16))
            hid.append((_silu(hg) * hu).astype(BF16))
        acc_ref[...] += _dot(jnp.concatenate(hid, axis=1), wd_ref[0].astype(BF16))

        @pl.when(f == last_f)
        def _():
            y_ref[0] = (acc_ref[...] * gs_ref[0]).astype(BF16)

    run(xl_ref, gl_ref, yl_ref, accl_ref)
    if with_ctx:
        @pl.when(i == pl.num_programs(1) - 1)
        def _():
            run(xc_ref, gc_ref, yc_ref, accc_ref)


def _expert_ffn(layer, w_gate, w_up, w_down, xg, gslot, xg_c=None, gslot_c=None):
    n_e, m, d = xg.shape
    ff = w_gate.shape[3]
    tm = _tile(m, 1024)
    tf = _tile(ff, 512)
    with_ctx = xg_c is not None
    x_spec = pl.BlockSpec((1, tm, d), lambda e, i, f: (e, i, 0))
    g_spec = pl.BlockSpec((1, tm, 1), lambda e, i, f: (e, i, 0))
    w_specs = [
        pl.BlockSpec((None, 1, d, tf), lambda e, i, f: (layer, e, 0, f)),
        pl.BlockSpec((None, 1, d, tf), lambda e, i, f: (layer, e, 0, f)),
        pl.BlockSpec((None, 1, tf, d), lambda e, i, f: (layer, e, f, 0)),
    ]
    out_specs = [x_spec]
    out_shape = [jax.ShapeDtypeStruct((n_e, m, d), BF16)]
    scratch = [pltpu.VMEM((tm, d), F32)]
    if with_ctx:
        mc = xg_c.shape[1]
        xc_spec = pl.BlockSpec((1, mc, d), lambda e, i, f: (e, 0, 0))
        gc_spec = pl.BlockSpec((1, mc, 1), lambda e, i, f: (e, 0, 0))
        in_specs = [x_spec, xc_spec] + w_specs + [g_spec, gc_spec]
        args = [xg, xg_c, w_gate, w_up, w_down, gslot, gslot_c]
        out_specs.append(xc_spec)
        out_shape.append(jax.ShapeDtypeStruct((n_e, mc, d), BF16))
        scratch.append(pltpu.VMEM((mc, d), F32))
    else:
        in_specs = [x_spec] + w_specs + [g_spec]
        args = [xg, w_gate, w_up, w_down, gslot]
    return pl.pallas_call(
        functools.partial(_ffn_kernel, with_ctx=with_ctx),
        grid=(n_e, m // tm, ff // tf),
        in_specs=in_specs,
        out_specs=out_specs,
        out_shape=out_shape,
        scratch_shapes=scratch,
        compiler_params=_params("arbitrary", "arbitrary", "arbitrary"),
        name="expert_ffn",
    )(*args)


def _combine_kernel(cs_ref, posm_ref, y_ref, h_ref, g_ref, gate_ref, o_ref, *, tt, n_sub, win, cap, n_e,
                    n_tiles):
    b, i = pl.program_id(0), pl.program_id(1)
    for sub in range(n_sub):
        cols = slice(sub * tt, (sub + 1) * tt)
        acc = None
        for e in range(n_e):
            if cap <= win:
                a0 = 0
                yw = y_ref[e]
            else:
                s0 = cs_ref[(b * n_e + e) * n_tiles + i * n_sub + sub]
                a0 = jnp.minimum((s0 // BF16_SUBLANES) * BF16_SUBLANES, cap - win)
                a0 = pl.multiple_of(a0, BF16_SUBLANES)
                yw = y_ref[e, pl.ds(a0, win), :]
            wn = yw.shape[0]
            slot = a0 + lax.broadcasted_iota(I32, (wn, tt), 0)
            onehot = jnp.where(posm_ref[0, e:e + 1, cols] == slot, 1.0, 0.0).astype(BF16)
            part = lax.dot_general(onehot, yw, (((0,), (0,)), ((), ())), preferred_element_type=F32)
            acc = part if acc is None else acc + part
        o_ref[cols, :] = h_ref[cols, :] + gate_ref[...] * _rms(acc, g_ref[...])


def _combine(y, posm, pose, h, g, gate, row_fn, batch, n, cap):
    n_e = y.shape[0]
    d = y.shape[2]
    tt = LANES
    win = tt + BF16_SUBLANES
    n_tiles = n // tt
    n_sub = 4 if n_tiles % 4 == 0 else (2 if n_tiles % 2 == 0 else 1)
    n_steps = n_tiles // n_sub
    cstart = pose[:, :, ::tt].reshape(-1)
    grid_spec = pltpu.PrefetchScalarGridSpec(
        num_scalar_prefetch=1,
        grid=(batch, n_steps),
        in_specs=[
            pl.BlockSpec((1, n_e, n_sub * tt), lambda b, i, cs: (b, 0, i)),
            pl.BlockSpec((n_e, cap, d), lambda b, i, cs: (0, b, 0), pipeline_mode=pl.Buffered(1)),
            pl.BlockSpec((n_sub * tt, d), lambda b, i, cs: (b * n_steps + i, 0)),
            pl.BlockSpec((1, d), lambda b, i, cs: (0, 0)),
            pl.BlockSpec((None, 1, d), lambda b, i, cs: (row_fn(b * n), 0, 0)),
        ],
        out_specs=pl.BlockSpec((n_sub * tt, d), lambda b, i, cs: (b * n_steps + i, 0)),
    )
    return pl.pallas_call(
        functools.partial(_combine_kernel, tt=tt, n_sub=n_sub, win=win, cap=cap, n_e=n_e, n_tiles=n_tiles),
        grid_spec=grid_spec,
        out_shape=jax.ShapeDtypeStruct(h.shape, F32),
        compiler_params=_params("arbitrary", "arbitrary"),
        name="moe_combine",
    )(cstart, posm, y, h, g, gate)


def _moe_dispatch(xf, aff, batch, n):
    cap = CAPACITY_FACTOR * n // N_EXPERTS
    posm, pose, gsel = _route(aff, batch, n, cap)
    xg, gslot = _gather(xf, posm, pose, gsel, batch, n, cap)
    return xg, gslot, (posm, pose, cap)


def _rope_angles(seq, rot_dim):
    n_rows = seq // GRID_W
    row = jnp.repeat(jnp.arange(n_rows), GRID_W)
    col = jnp.tile(jnp.arange(GRID_W), n_rows)
    n_freq = rot_dim // 4
    inv = ROPE_BASE ** (-jnp.arange(n_freq, dtype=F32) / n_freq)
    ang = jnp.concatenate([row[:, None] * inv, col[:, None] * inv], axis=-1)
    return jnp.cos(ang), jnp.sin(ang)


def kernel(x, c, ctx, c_ctx, mod_w, mod_b, norm_g, ev_w_in, ev_w_out, ev_sink, ev_qk_norm, od_w_in, od_q_norm,
           od_kv_norm, od_w_uq, od_w_ukv, od_w_out, router_w, exp_w_gate, exp_w_up, exp_w_down):
    batch, seq, d = x.shape
    n_ctx = ctx.shape[1]
    depth = mod_w.shape[0]
    assert batch < MOD_ROWS and seq % LANES == 0 and n_ctx % LANES == 0 and seq >= 4 * WINDOW

    cs = jnp.concatenate([c, c_ctx[None, :], jnp.zeros((MOD_ROWS - batch - 1, d), F32)], axis=0)
    mods = _modulation(cs, mod_w, mod_b).reshape(depth, MOD_ROWS, N_MOD, 1, d)

    def lat_row(r):
        return r // seq

    def ctx_row(r):
        return batch

    cos_h, sin_h = _rope_angles(seq, HEAD_DIM)
    rope_even = (jnp.concatenate([cos_h, cos_h], axis=-1), jnp.concatenate([-sin_h, sin_h], axis=-1))
    cos_m, sin_m = _rope_angles(seq, MLA_ROPE)
    half = MLA_ROPE // 2
    zeros = lambda w: jnp.zeros((seq, w), F32)
    rope_mla = (
        jnp.concatenate([cos_m, cos_m, zeros(LANES - MLA_ROPE)], axis=-1),
        jnp.concatenate([-sin_m, zeros(LANES - half)], axis=-1),
        jnp.concatenate([zeros(half), sin_m, zeros(LANES - MLA_ROPE)], axis=-1),
    )

    h_lat = x.reshape(batch * seq, d)
    h_ctx = ctx.reshape(batch * n_ctx, d)
    for layer in range(depth):
        with_ctx = layer < depth - 1
        i = layer // 2
        g = norm_g[layer].reshape(4, 1, d)
        m = [mods[layer, :, k] for k in range(N_MOD)]
        wr_t = router_w[layer].T
        if layer % 2 == 0:
            w_in = ev_w_in[i].astype(BF16)
            w_out = ev_w_out[i].astype(BF16)
            qkv_l = _even_inproj(h_lat, g[0], m[0], m[1], w_in, ev_qk_norm[i], rope_even, lat_row, seq)
            qkv_c = _even_inproj(h_ctx, g[0], m[0], m[1], w_in, ev_qk_norm[i], None, ctx_row, n_ctx)
            ga, gb = A_HEADS // A_KV_HEADS, B_HEADS // B_KV_HEADS
            hd = HEAD_DIM
            a_kw = dict(batch=batch, n_heads=A_HEADS, group=ga, dq=hd, dv=hd)
            b_kw = dict(batch=batch, n_heads=B_HEADS, group=gb, dq=hd, dv=hd, band=False)
            a_ctx = (qkv_c, AK0 * hd, qkv_c, AV0 * hd, n_ctx)
            b_ctx = (qkv_c, BK0 * hd, qkv_c, BV0 * hd, n_ctx)
            o_a = _attention(qkv_l, AQ0 * hd, [a_ctx, (qkv_l, AK0 * hd, qkv_l, AV0 * hd, seq)], ev_sink[i],
                             seq=seq, band=True, tq_cap=512, hp=4, **a_kw)
            o_b = _attention(qkv_l, BQ0 * hd, [b_ctx, (qkv_l, BK0 * hd, qkv_l, BV0 * hd, seq)], None,
                             seq=seq, tq_cap=512, hp=4, **b_kw)
            o_lat = [o_a, o_b]
            if with_ctx:
                o_ac = _attention(qkv_c, AQ0 * hd, [a_ctx], ev_sink[i], seq=n_ctx, band=False, tq_cap=512,
                                  hp=4, **a_kw)
                o_bc = _attention(qkv_c, BQ0 * hd, [b_ctx], None, seq=n_ctx, tq_cap=512, hp=4, **b_kw)
                o_ctx = [o_ac, o_bc]
        else:
            w_in = od_w_in[i]
            w_qkv = w_in[:, :MLA_Q_RANK + MLA_KV_RANK].astype(BF16)
            w_kr = jnp.pad(w_in[:, MLA_Q_RANK + MLA_KV_RANK:], ((0, 0), (0, LANES - MLA_ROPE))).astype(BF16)
            w_uq = od_w_uq[i].reshape(MLA_Q_RANK, MLA_HEADS, MLA_NOPE + MLA_ROPE)
            w_uq = jnp.pad(w_uq, ((0, 0), (0, 0), (0, MLA_QK_PAD - MLA_NOPE - MLA_ROPE)))
            w_uq = w_uq.reshape(MLA_Q_RANK, MLA_HEADS * MLA_QK_PAD).astype(BF16)
            w_ukv = od_w_ukv[i].astype(BF16)
            w_out = od_w_out[i].astype(BF16)
            qn, kvn = od_q_norm[i].reshape(1, -1), od_kv_norm[i].reshape(1, -1)
            q_l, k_l, v_l = _mla_proj(h_lat, g[0], m[0], m[1], w_qkv, w_kr, qn, kvn, w_uq, w_ukv, rope_mla,
                                      lat_row, seq, True)
            proj_c = _mla_proj(h_ctx, g[0], m[0], m[1], w_qkv, w_kr, qn, kvn, w_uq, w_ukv, None, ctx_row,
                               n_ctx, with_ctx)
            k_c, v_c = proj_c[-2], proj_c[-1]
            m_kw = dict(batch=batch, n_heads=MLA_HEADS, group=1, dq=MLA_QK_PAD, dv=MLA_V, band=False, tq_cap=512,
                        hp=4, tk=1024)
            o_lat = [_attention(q_l, 0, [(k_c, 0, v_c, 0, n_ctx), (k_l, 0, v_l, 0, seq)], None, seq=seq, **m_kw)]
            if with_ctx:
                o_ctx = [_attention(proj_c[0], 0, [(k_c, 0, v_c, 0, n_ctx)], None, seq=n_ctx, **m_kw)]
        ew = (layer, exp_w_gate, exp_w_up, exp_w_down)
        h_lat, xf, aff = _outproj(o_lat, w_out, h_lat, g[1], m[2], g[2], m[3], m[4], wr_t, lat_row, seq)
        xg, gslot, (posm, pose, cap) = _moe_dispatch(xf, aff, batch, seq)
        if with_ctx:
            h_ctx, xf_c, aff_c = _outproj(o_ctx, w_out, h_ctx, g[1], m[2], g[2], m[3], m[4], wr_t, ctx_row,
                                          n_ctx)
            xg_c, gslot_c, (posm_c, pose_c, cap_c) = _moe_dispatch(xf_c, aff_c, batch, n_ctx)
            y, y_c = _expert_ffn(*ew, xg, gslot, xg_c, gslot_c)
            h_ctx = _combine(y_c, posm_c, pose_c, h_ctx, g[3], m[5], ctx_row, batch, n_ctx, cap_c)
        else:
            (y,) = _expert_ffn(*ew, xg, gslot)
        h_lat = _combine(y, posm, pose, h_lat, g[3], m[5], lat_row, batch, seq, cap)
    return h_lat.reshape(batch, seq, d)
```
